```python
import jax, jax.numpy as jnp
from jax import lax
import numpy as np

D_MODEL = 4096
BATCH = 4
SEQ = 2048
DEPTH = 2
DEC_BATCH = 8
DEC_SEQ = 1
PAST_LEN = 16384
PAGE_SIZE = 128

RET_HEADS = 8
RET_DK = 128
RET_DV = 128
ATT_HEADS = 8
ATT_KV_HEADS = 2
ATT_HEAD_DIM = 128
IDX_HEADS = 16
IDX_DIM = 128
IDX_ROPE_DIM = 64
TOPK_MAX = 256
Q_BLOCK = 128
HG_HEADS = 8
HG_DK = 128
HG_DV = 128
CHUNK = 64
D_FF = 11008
ROPE_THETA = 10000.0
EPS = 1e-6
NEG_BIG = -1e30

RET_QK_W = RET_HEADS * RET_DK
RET_W = RET_HEADS * RET_DV
ATT_W = ATT_HEADS * ATT_HEAD_DIM
KV_W = ATT_KV_HEADS * ATT_HEAD_DIM
HG_K_W = HG_HEADS * HG_DK
HG_W = HG_HEADS * HG_DV
IN_SPLITS = (RET_QK_W, RET_QK_W, RET_W, RET_W,
             ATT_W, KV_W, KV_W, IDX_HEADS * IDX_DIM, IDX_DIM, IDX_HEADS,
             HG_K_W, HG_K_W, HG_W, HG_W,
             D_MODEL, D_MODEL, D_MODEL)
N_IN = sum(IN_SPLITS)

kernel_name = 'hybrid_ret_dsa_hgrn2_macaron_step'


def rms_norm(x, g):
    xf = x.astype(jnp.float32)
    y = xf * lax.rsqrt(jnp.mean(xf * xf, axis=-1, keepdims=True) + EPS)
    return (y * g.astype(jnp.float32)).astype(x.dtype)


def layer_norm(x, g, b):
    xf = x.astype(jnp.float32)
    mu = jnp.mean(xf, axis=-1, keepdims=True)
    var = jnp.mean(jnp.square(xf - mu), axis=-1, keepdims=True)
    y = (xf - mu) * lax.rsqrt(var + EPS)
    return (y * g.astype(jnp.float32) + b.astype(jnp.float32)).astype(x.dtype)


def split_cols(z):
    cuts = [int(c) for c in np.cumsum(IN_SPLITS)[:-1]]
    return jnp.split(z, cuts, axis=-1)


def rope_freqs(dim):
    return ROPE_THETA ** (-jnp.arange(0, dim, 2, dtype=jnp.float32) / dim)


def retnet_freqs(dim):
    return 1.0 / (ROPE_THETA ** jnp.linspace(0.0, 1.0, dim // 2, dtype=jnp.float32))


def apply_rotary(x, pos, freqs):
    half = x.shape[-1] // 2
    ang = pos.astype(jnp.float32)[:, None] * freqs[None, :]
    cos = jnp.cos(ang)[:, None, :]
    sin = jnp.sin(ang)[:, None, :]
    xf = x.astype(jnp.float32)
    x1, x2 = xf[..., :half], xf[..., half:]
    return jnp.concatenate([x1 * cos - x2 * sin, x2 * cos + x1 * sin], axis=-1).astype(x.dtype)


def swiglu(h, w1, w3, w2):
    return (jax.nn.silu(h @ w1) * (h @ w3)) @ w2


def chunk_size(t):
    return CHUNK if t % CHUNK == 0 else t


def to_chunks(a, c):
    b, t, h, d = a.shape
    return a.reshape(b, t // c, c, h, d).transpose(1, 0, 3, 2, 4)


def from_chunks(a):
    n, b, h, c, d = a.shape
    return a.transpose(1, 0, 3, 2, 4).reshape(b, n * c, h, d)


def retention_chunkwise(q, k, v, s0):
    f32 = jnp.float32
    c = chunk_size(q.shape[1])
    lg = jnp.log(1.0 - 2.0 ** (-5.0 - jnp.arange(RET_HEADS, dtype=f32)))
    tl = jnp.arange(c, dtype=f32)
    diff = tl[:, None] - tl[None, :]
    intra = jnp.where(diff >= 0, jnp.exp(lg[:, None, None] * jnp.maximum(diff, 0.0)), 0.0)
    q_dec = jnp.exp(lg[:, None] * (tl[None, :] + 1.0))[..., None]
    k_dec = jnp.exp(lg[:, None] * (c - 1.0 - tl[None, :]))[..., None]
    s_dec = jnp.exp(lg * c)[:, None, None]

    def step(s, inp):
        qc, kc, vc = inp
        a = jnp.einsum('bhtd,bhsd->bhts', qc, kc) * intra
        o = jnp.einsum('bhts,bhsv->bhtv', a, vc) + jnp.einsum('bhtd,bhdv->bhtv', qc * q_dec, s)
        s = s * s_dec + jnp.einsum('bhsd,bhsv->bhdv', kc * k_dec, vc)
        return s, o

    s, o = lax.scan(step, s0.astype(f32),
                    (to_chunks(q.astype(f32), c), to_chunks(k.astype(f32), c), to_chunks(v.astype(f32), c)))
    return from_chunks(o), s


def gla_chunkwise(q, k, v, log_f, s0):
    f32 = jnp.float32
    c = chunk_size(q.shape[1])
    causal = (jnp.arange(c)[:, None] >= jnp.arange(c)[None, :])[:, :, None]

    def step(s, inp):
        qc, kc, vc, gc = inp
        b = jnp.cumsum(gc, axis=2)
        b_last = b[:, :, -1, :]
        diff = b[:, :, :, None, :] - b[:, :, None, :, :]
        decay = jnp.where(causal, jnp.exp(jnp.minimum(diff, 0.0)), 0.0)
        a = jnp.einsum('bhtk,bhtsk,bhsk->bhts', qc, decay, kc)
        o = jnp.einsum('bhts,bhsv->bhtv', a, vc) + jnp.einsum('bhtk,bhkv->bhtv', qc * jnp.exp(b), s)
        s = s * jnp.exp(b_last)[..., None] + jnp.einsum(
            'bhsk,bhsv->bhkv', kc * jnp.exp(b_last[:, :, None, :] - b), vc)
        return s, o

    s, o = lax.scan(step, s0.astype(f32),
                    (to_chunks(q.astype(f32), c), to_chunks(k.astype(f32), c),
                     to_chunks(v.astype(f32), c), to_chunks(log_f.astype(f32), c)))
    return from_chunks(o), s


def dsa_attend(q, qi, wi, q_pos, k, v, ki, n_sel):
    b, t, h, d = q.shape
    n_kv = k.shape[2]
    key_pos = jnp.arange(k.shape[1], dtype=jnp.int32)
    visible = key_pos[None, :] <= q_pos[:, None]
    rel = jax.nn.relu(jnp.einsum('bthd,bsd->bths', qi, ki).astype(jnp.float32))
    score = jnp.einsum('bths,bth->bts', rel, wi.astype(jnp.float32))
    score = jnp.where(visible[None], score, NEG_BIG)
    _, sel = lax.top_k(score, n_sel)
    sel_ok = sel <= q_pos[None, :, None]
    take = jax.vmap(lambda arr, idx: arr[idx])
    k_sel = take(k, sel)
    v_sel = take(v, sel)
    qg = q.reshape(b, t, n_kv, h // n_kv, d)
    s = jnp.einsum('btngd,btsnd->btngs', qg, k_sel).astype(jnp.float32) * (d ** -0.5)
    s = jnp.where(sel_ok[:, :, None, None, :], s, NEG_BIG)
    p = jax.nn.softmax(s, axis=-1).astype(v.dtype)
    o = jnp.einsum('btngs,btsnd->btngd', p, v_sel)
    return o.reshape(b, t, h * d)


def dsa_sparse_attention(q, qi, wi, q_pos, k, v, ki):
    n_sel = min(TOPK_MAX, k.shape[1] // 4)
    b, t = q.shape[0], q.shape[1]
    if t % Q_BLOCK != 0:
        return dsa_attend(q, qi, wi, q_pos, k, v, ki, n_sel)
    nb = t // Q_BLOCK

    def blocks(a):
        return a.reshape((b, nb, Q_BLOCK) + a.shape[2:]).swapaxes(0, 1)

    out = lax.map(lambda xs: dsa_attend(xs[0], xs[1], xs[2], xs[3], k, v, ki, n_sel),
                  (blocks(q), blocks(qi), blocks(wi), q_pos.reshape(nb, Q_BLOCK)))
    return out.swapaxes(0, 1).reshape(b, t, -1)


def gather_pages(pool, page_table):
    g = pool[page_table]
    return g.reshape((g.shape[0], g.shape[1] * g.shape[2]) + g.shape[3:])


def token_mixing(h, pos, ret_state, hg_state, past_k, past_v, past_ik, lb,
                 w_in, ret_norm, q_norm, k_norm, idx_k_g, idx_k_b, hg_norm,
                 w_up_ret, w_up_att, w_up_hg, w_out):
    b, t, _ = h.shape
    f32 = jnp.float32
    (r_q, r_k, r_v, r_g, a_q, a_k, a_v, i_q, i_k, i_w,
     h_f, h_q, h_i, h_g, g_ret, g_att, g_hg) = split_cols(h @ w_in)

    rf = retnet_freqs(RET_DK)
    rq = apply_rotary(r_q.reshape(b, t, RET_HEADS, RET_DK), pos, rf)
    rk = apply_rotary(r_k.reshape(b, t, RET_HEADS, RET_DK), pos, rf) * (RET_DK ** -0.5)
    ro, ret_new = retention_chunkwise(rq, rk, r_v.reshape(b, t, RET_HEADS, RET_DV), ret_state)
    ro = rms_norm(ro, ret_norm.reshape(RET_HEADS, RET_DV)).reshape(b, t, RET_W)
    u_ret = (ro * jax.nn.silu(r_g.astype(f32))).astype(h.dtype) @ w_up_ret

    af = rope_freqs(ATT_HEAD_DIM)
    aq = apply_rotary(rms_norm(a_q.reshape(b, t, ATT_HEADS, ATT_HEAD_DIM), q_norm), pos, af)
    ak = apply_rotary(rms_norm(a_k.reshape(b, t, ATT_KV_HEADS, ATT_HEAD_DIM), k_norm), pos, af)
    av = a_v.reshape(b, t, ATT_KV_HEADS, ATT_HEAD_DIM)
    xf = rope_freqs(IDX_ROPE_DIM)
    iq = i_q.reshape(b, t, IDX_HEADS, IDX_DIM)
    iq = jnp.concatenate([apply_rotary(iq[..., :IDX_ROPE_DIM], pos, xf), iq[..., IDX_ROPE_DIM:]], axis=-1)
    ik = layer_norm(i_k, idx_k_g, idx_k_b)
    ik = jnp.concatenate([apply_rotary(ik[:, :, None, :IDX_ROPE_DIM], pos, xf)[:, :, 0],
                          ik[..., IDX_ROPE_DIM:]], axis=-1)
    if past_k is None:
        k_all, v_all, ik_all = ak, av, ik
    else:
        k_all = jnp.concatenate([past_k, ak], axis=1)
        v_all = jnp.concatenate([past_v, av], axis=1)
        ik_all = jnp.concatenate([past_ik, ik], axis=1)
    ao = dsa_sparse_attention(aq, iq * (IDX_DIM ** -0.5), i_w * (IDX_HEADS ** -0.5), pos,
                              k_all, v_all, ik_all)
    u_att = ao @ w_up_att

    fa = h_f.astype(f32)
    log_f = jax.nn.log_sigmoid(fa) + jnp.log1p(lb * jnp.exp(-fa))
    hk = (1.0 - lb) * jax.nn.sigmoid(-fa)
    hq = jax.nn.silu(h_q.astype(f32))
    shp = (b, t, HG_HEADS, HG_DK)
    ho, hg_new = gla_chunkwise(hq.reshape(shp), hk.reshape(shp),
                               h_i.reshape(b, t, HG_HEADS, HG_DV), log_f.reshape(shp), hg_state)
    ho = rms_norm(ho, hg_norm.reshape(HG_HEADS, HG_DV)).reshape(b, t, HG_W)
    u_hg = (ho * jax.nn.sigmoid(h_g.astype(f32))).astype(h.dtype) @ w_up_hg

    merged = jax.nn.sigmoid(g_ret) * u_ret + jax.nn.sigmoid(g_att) * u_att + jax.nn.sigmoid(g_hg) * u_hg
    return merged @ w_out, ret_new, hg_new, ak, av, ik


def decoder_layer(x, pos, ret_state, hg_state, past_k, past_v, past_ik, lb, lw):
    (ffn1_norm, ffn1_w1, ffn1_w3, ffn1_w2, mix_norm, w_in, ret_norm, q_norm, k_norm,
     idx_k_g, idx_k_b, hg_norm, w_up_ret, w_up_att, w_up_hg, w_out,
     ffn2_norm, ffn2_w1, ffn2_w3, ffn2_w2) = lw
    x = x + 0.5 * swiglu(rms_norm(x, ffn1_norm), ffn1_w1, ffn1_w3, ffn1_w2)
    y, ret_new, hg_new, k_new, v_new, ik_new = token_mixing(
        rms_norm(x, mix_norm), pos, ret_state, hg_state, past_k, past_v, past_ik, lb,
        w_in, ret_norm, q_norm, k_norm, idx_k_g, idx_k_b, hg_norm,
        w_up_ret, w_up_att, w_up_hg, w_out)
    x = x + y
    x = x + 0.5 * swiglu(rms_norm(x, ffn2_norm), ffn2_w1, ffn2_w3, ffn2_w2)
    return x, ret_new, hg_new, k_new, v_new, ik_new


def setup_inputs(seed: int = 0) -> dict:
    key = jax.random.key(seed)
    keys = iter(jax.random.split(key, 48))
    f32 = jnp.float32

    def nrm(shape, scale):
        return jax.random.normal(next(keys), shape, f32) * scale

    def gain(shape):
        return 1.0 + nrm(shape, 0.02)

    n_pages = PAST_LEN // PAGE_SIZE
    n_used = DEC_BATCH * n_pages
    n_pool = n_used + max(1, n_used // 4)
    page_table = jax.random.permutation(next(keys), n_pool)[:n_used].reshape(DEC_BATCH, n_pages).astype(jnp.int32)
    d = D_MODEL
    return {
        'x_prompt': nrm((BATCH, SEQ, d), 1.0),
        'x_sample': nrm((DEC_BATCH, DEC_SEQ, d), 1.0),
        'state_ret': nrm((DEPTH, DEC_BATCH, RET_HEADS, RET_DK, RET_DV), 1.0),
        'state_hgrn': nrm((DEPTH, DEC_BATCH, HG_HEADS, HG_DK, HG_DV), 0.5),
        'cache_k': nrm((DEPTH, n_pool, PAGE_SIZE, ATT_KV_HEADS, ATT_HEAD_DIM), 1.0),
        'cache_v': nrm((DEPTH, n_pool, PAGE_SIZE, ATT_KV_HEADS, ATT_HEAD_DIM), 1.0),
        'cache_idx_k': nrm((DEPTH, n_pool, PAGE_SIZE, IDX_DIM), 1.0),
        'page_table': page_table,
        'ffn1_norm': gain((DEPTH, d)),
        'ffn1_w1': nrm((DEPTH, d, D_FF), d ** -0.5),
        'ffn1_w3': nrm((DEPTH, d, D_FF), d ** -0.5),
        'ffn1_w2': nrm((DEPTH, D_FF, d), D_FF ** -0.5),
        'mix_norm': gain((DEPTH, d)),
        'w_in': nrm((DEPTH, d, N_IN), d ** -0.5),
        'ret_norm': gain((DEPTH, RET_W)),
        'q_norm': gain((DEPTH, ATT_HEAD_DIM)),
        'k_norm': gain((DEPTH, ATT_HEAD_DIM)),
        'idx_k_g': gain((DEPTH, IDX_DIM)),
        'idx_k_b': nrm((DEPTH, IDX_DIM), 0.02),
        'hg_lb_raw': nrm((DEPTH, HG_K_W), 0.5),
        'hg_norm': gain((DEPTH, HG_W)),
        'w_up_ret': nrm((DEPTH, RET_W, d), RET_W ** -0.5),
        'w_up_att': nrm((DEPTH, ATT_W, d), ATT_W ** -0.5),
        'w_up_hg': nrm((DEPTH, HG_W, d), HG_W ** -0.5),
        'w_out': nrm((DEPTH, d, d), d ** -0.5),
        'ffn2_norm': gain((DEPTH, d)),
        'ffn2_w1': nrm((DEPTH, d, D_FF), d ** -0.5),
        'ffn2_w3': nrm((DEPTH, d, D_FF), d ** -0.5),
        'ffn2_w2': nrm((DEPTH, D_FF, d), D_FF ** -0.5),
    }


def reference(x_prompt, x_sample, state_ret, state_hgrn, cache_k, cache_v, cache_idx_k, page_table,
              ffn1_norm, ffn1_w1, ffn1_w3, ffn1_w2, mix_norm, w_in, ret_norm, q_norm, k_norm,
              idx_k_g, idx_k_b, hg_lb_raw, hg_norm, w_up_ret, w_up_att, w_up_hg, w_out,
              ffn2_norm, ffn2_w1, ffn2_w3, ffn2_w2):
    pos_p = jnp.arange(x_prompt.shape[1], dtype=jnp.int32)
    pos_s = PAST_LEN + jnp.arange(x_sample.shape[1], dtype=jnp.int32)
    lb_soft = jax.nn.softmax(hg_lb_raw.astype(jnp.float32), axis=0)
    lb_all = jnp.cumsum(lb_soft, axis=0) - lb_soft[0]
    bp = x_prompt.shape[0]
    xp, xs = x_prompt, x_sample
    rp_l, rs_l, hp_l, hs_l = [], [], [], []
    kp_l, vp_l, ip_l, ks_l, vs_l, is_l = [], [], [], [], [], []
    for l in range(DEPTH):
        lw = (ffn1_norm[l], ffn1_w1[l], ffn1_w3[l], ffn1_w2[l], mix_norm[l], w_in[l], ret_norm[l],
              q_norm[l], k_norm[l], idx_k_g[l], idx_k_b[l], hg_norm[l], w_up_ret[l], w_up_att[l],
              w_up_hg[l], w_out[l], ffn2_norm[l], ffn2_w1[l], ffn2_w3[l], ffn2_w2[l])
        zr = jnp.zeros((bp, RET_HEADS, RET_DK, RET_DV), jnp.float32)
        zh = jnp.zeros((bp, HG_HEADS, HG_DK, HG_DV), jnp.float32)
        xp, r_p, h_p, k_p, v_p, i_p = decoder_layer(xp, pos_p, zr, zh, None, None, None, lb_all[l], lw)
        past_k = gather_pages(cache_k[l], page_table)
        past_v = gather_pages(cache_v[l], page_table)
        past_ik = gather_pages(cache_idx_k[l], page_table)
        xs, r_s, h_s, k_s, v_s, i_s = decoder_layer(xs, pos_s, state_ret[l], state_hgrn[l],
                                                    past_k, past_v, past_ik, lb_all[l], lw)
        rp_l.append(r_p); rs_l.append(r_s); hp_l.append(h_p); hs_l.append(h_s)
        kp_l.append(k_p); vp_l.append(v_p); ip_l.append(i_p)
        ks_l.append(k_s); vs_l.append(v_s); is_l.append(i_s)
    return (xp, xs,
            jnp.stack(rp_l).astype(state_ret.dtype), jnp.stack(rs_l).astype(state_ret.dtype),
            jnp.stack(hp_l).astype(state_hgrn.dtype), jnp.stack(hs_l).astype(state_hgrn.dtype),
            jnp.stack(kp_l), jnp.stack(vp_l), jnp.stack(ip_l),
            jnp.stack(ks_l), jnp.stack(vs_l), jnp.stack(is_l))
```

```python
import functools
import math

import numpy as np
import jax
import jax.numpy as jnp
from jax import lax
from jax.experimental import pallas as pl
from jax.experimental.pallas import tpu as pltpu

D_MODEL = 4096
BATCH = 4
SEQ = 2048
DEPTH = 2
DEC_BATCH = 8
DEC_SEQ = 1
PAST_LEN = 16384
PAGE_SIZE = 128

RET_HEADS = 8
RET_DK = 128
RET_DV = 128
ATT_HEADS = 8
ATT_KV_HEADS = 2
ATT_HEAD_DIM = 128
IDX_HEADS = 16
IDX_DIM = 128
IDX_ROPE_DIM = 64
TOPK_MAX = 256
Q_BLOCK = 128
HG_HEADS = 8
HG_DK = 128
HG_DV = 128
D_FF = 11008
ROPE_THETA = 10000.0
EPS = 1e-6
NEG_BIG = -1e30

F32 = jnp.float32
BF16 = jnp.bfloat16
LANE = 128
HD = 128
INT_MIN = -(2 ** 31)
VMEM_LIMIT = 56 * 1024 * 1024
GLA_CHUNK = 64
GLA_SUB = 16
PAGES_PER_STEP = 8

_NT = (((1,), (1,)), ((), ()))
_TN = (((0,), (0,)), ((), ()))


def _params(sem):
    return pltpu.CompilerParams(dimension_semantics=sem, vmem_limit_bytes=VMEM_LIMIT)


def _bdot(a, b, dims=None):
    a = a.astype(BF16)
    b = b.astype(BF16)
    if dims is None:
        return jnp.dot(a, b, preferred_element_type=F32)
    return lax.dot_general(a, b, dims, preferred_element_type=F32)


def _silu(x):
    return x * jax.nn.sigmoid(x)


def _segments():
    ret_qk = RET_HEADS * RET_DK
    ret_w = RET_HEADS * RET_DV
    att_w = ATT_HEADS * ATT_HEAD_DIM
    kv_w = ATT_KV_HEADS * ATT_HEAD_DIM
    hg_k = HG_HEADS * HG_DK
    hg_w = HG_HEADS * HG_DV
    names = ['r_q', 'r_k', 'r_v', 'r_g', 'a_q', 'a_k', 'a_v', 'i_q', 'i_k', 'i_w',
             'h_f', 'h_q', 'h_i', 'h_g', 'g_ret', 'g_att', 'g_hg']
    widths = [ret_qk, ret_qk, ret_w, ret_w, att_w, kv_w, kv_w, IDX_HEADS * IDX_DIM, IDX_DIM, IDX_HEADS,
              hg_k, hg_k, hg_w, hg_w, D_MODEL, D_MODEL, D_MODEL]
    src = {}
    off = 0
    for n, w in zip(names, widths):
        src[n] = (off, w)
        off += w
    padded = {n: -(-w // LANE) * LANE for n, w in zip(names, widths)}
    order = ['g_ret', 'g_att', 'g_hg'] + sorted(
        [n for n in names if not n.startswith('g_')], key=lambda n: -padded[n])
    dst = {}
    off = 0
    for n in order:
        dst[n] = off
        off += padded[n]
    total = -(-off // 256) * 256
    for n in order:
        assert dst[n] % padded[n] == 0 or n.startswith('g_'), (n, dst[n], padded[n])
    return src, dst, padded, order, total


def _permute_w_in(w_in):
    src, dst, padded, order, total = _segments()
    parts = []
    off = 0
    for n in order:
        s, w = src[n]
        parts.append(w_in[:, :, s:s + w])
        if padded[n] != w:
            parts.append(jnp.zeros(w_in.shape[:2] + (padded[n] - w,), w_in.dtype))
        off += padded[n]
    if total != off:
        parts.append(jnp.zeros(w_in.shape[:2] + (total - off,), w_in.dtype))
    return jnp.concatenate(parts, axis=-1)


def _rmsnorm_kernel(x_ref, g_ref, o_ref):
    x = x_ref[...]
    ms = jnp.mean(x * x, axis=-1, keepdims=True)
    o_ref[...] = (x * lax.rsqrt(ms + EPS) * g_ref[...]).astype(o_ref.dtype)


def _rmsnorm(x, g_all, l, out_dtype):
    m, d = x.shape
    tr = min(m, 256)
    return pl.pallas_call(
        _rmsnorm_kernel,
        grid=(m // tr,),
        in_specs=[pl.BlockSpec((tr, d), lambda i: (i, 0)),
                  pl.BlockSpec((None, 1, d), lambda i: (l, 0, 0))],
        out_specs=pl.BlockSpec((tr, d), lambda i: (i, 0)),
        out_shape=jax.ShapeDtypeStruct((m, d), out_dtype),
        compiler_params=_params(("parallel",)),
        name="rmsnorm",
    )(x, g_all.reshape(g_all.shape[0], 1, d))


def _mm_kernel(*refs, n_w, nk, epilogue, scale):
    lhs_ref = refs[0]
    w_refs = refs[1:1 + n_w]
    pos = 1 + n_w
    res_ref = None
    if epilogue == 'resid':
        res_ref = refs[pos]
        pos += 1
    out_ref = refs[pos]
    acc_refs = refs[pos + 1:]

    lhs = lhs_ref[...].astype(BF16)
    parts = [jnp.dot(lhs, w[...].astype(BF16), preferred_element_type=F32) for w in w_refs]

    def finish(vals):
        if epilogue == 'swiglu':
            out = _silu(vals[0]) * vals[1]
        elif epilogue == 'resid':
            out = res_ref[...] + scale * vals[0]
        else:
            out = vals[0]
        out_ref[...] = out.astype(out_ref.dtype)

    if nk == 1:
        finish(parts)
        return

    k = pl.program_id(2)

    @pl.when(k == 0)
    def _():
        for a, p in zip(acc_refs, parts):
            a[...] = p

    @pl.when(k > 0)
    def _():
        for a, p in zip(acc_refs, parts):
            a[...] += p

    @pl.when(k == nk - 1)
    def _():
        finish([a[...] for a in acc_refs])


def _pick(n, cands):
    for c in cands:
        if n % c == 0:
            return c
    return n


def _matmul(lhs, ws, l, *, epilogue='plain', res=None, scale=1.0, out_dtype=F32, tm=1024, tn=256, tk=None):
    m, kdim = lhs.shape
    n = ws[0].shape[-1]
    tm = min(tm, m)
    tn = _pick(n, (tn, 256, 128))
    if tk is None:
        tk = _pick(kdim, (2048, 1024, 512, 256, 128))
    assert m % tm == 0 and n % tn == 0 and kdim % tk == 0
    nk = kdim // tk
    in_specs = [pl.BlockSpec((tm, tk), lambda i, j, k: (i, k))]
    in_specs += [pl.BlockSpec((None, tk, tn), lambda i, j, k: (l, k, j)) for _ in ws]
    args = [lhs] + list(ws)
    if epilogue == 'resid':
        in_specs.append(pl.BlockSpec((tm, tn), lambda i, j, k: (i, j)))
        args.append(res)
    scratch = [pltpu.VMEM((tm, tn), F32) for _ in ws] if nk > 1 else []
    return pl.pallas_call(
        functools.partial(_mm_kernel, n_w=len(ws), nk=nk, epilogue=epilogue, scale=scale),
        grid=(m // tm, n // tn, nk),
        in_specs=in_specs,
        out_specs=pl.BlockSpec((tm, tn), lambda i, j, k: (i, j)),
        out_shape=jax.ShapeDtypeStruct((m, n), out_dtype),
        scratch_shapes=scratch,
        compiler_params=_params(("parallel", "parallel", "arbitrary")),
        name="mm_" + epilogue,
    )(*args)


def _merge_kernel(ar_ref, aa_ref, ah_ref, wr_ref, wa_ref, wh_ref, gr_ref, ga_ref, gh_ref, o_ref):
    u_r = _bdot(ar_ref[...], wr_ref[...])
    u_a = _bdot(aa_ref[...], wa_ref[...])
    u_h = _bdot(ah_ref[...], wh_ref[...])
    out = (jax.nn.sigmoid(gr_ref[...]) * u_r + jax.nn.sigmoid(ga_ref[...]) * u_a
           + jax.nn.sigmoid(gh_ref[...]) * u_h)
    o_ref[...] = out.astype(o_ref.dtype)


def _merge(a_ret, a_att, a_hg, w_r, w_a, w_h, z, l, out_dtype):
    m = a_ret.shape[0]
    d = w_r.shape[-1]
    tm = min(m, 1024)
    tn = _pick(d, (256, 128))
    nb = d // tn
    lhs_spec = lambda a: pl.BlockSpec((tm, a.shape[1]), lambda i, j: (i, 0))
    w_spec = lambda w: pl.BlockSpec((None, w.shape[1], tn), lambda i, j: (l, 0, j))
    gate_spec = lambda g: pl.BlockSpec((tm, tn), lambda i, j: (i, g * nb + j))
    return pl.pallas_call(
        _merge_kernel,
        grid=(m // tm, nb),
        in_specs=[lhs_spec(a_ret), lhs_spec(a_att), lhs_spec(a_hg), w_spec(w_r), w_spec(w_a), w_spec(w_h),
                  gate_spec(0), gate_spec(1), gate_spec(2)],
        out_specs=pl.BlockSpec((tm, tn), lambda i, j: (i, j)),
        out_shape=jax.ShapeDtypeStruct((m, d), out_dtype),
        compiler_params=_params(("parallel", "parallel")),
        name="merge",
    )(a_ret, a_att, a_hg, w_r, w_a, w_h, z, z, z)


def _rot(x, cos, sin, half):
    up = pltpu.roll(x, LANE - half, 1)
    dn = pltpu.roll(x, half, 1)
    lane = lax.broadcasted_iota(jnp.int32, x.shape, 1)
    return x * cos + jnp.where(lane < half, up, dn) * sin


def _rope_tables(pos, freqs, width):
    ang = pos.astype(F32)[:, None] * freqs[None, :]
    cos, sin = jnp.cos(ang), jnp.sin(ang)
    t = pos.shape[0]
    pad_c = jnp.ones((t, LANE - width), F32)
    pad_s = jnp.zeros((t, LANE - width), F32)
    return (jnp.concatenate([cos, cos, pad_c], axis=1), jnp.concatenate([-sin, sin, pad_s], axis=1))


def _prelude_kernel(zq_ref, zk_ref, zv_ref, ziq_ref, zik_ref, ziw_ref, ca_ref, sa_ref, ci_ref, si_ref,
                    qn_ref, kn_ref, ig_ref, ib_ref,
                    aq_ref, ak_ref, av_ref, akb_ref, avb_ref, iq_ref, ik_ref, ikb_ref, iw_ref):
    ca, sa, ci, si = ca_ref[...], sa_ref[...], ci_ref[...], si_ref[...]

    def head_norm(x, g):
        return x * lax.rsqrt(jnp.mean(x * x, axis=-1, keepdims=True) + EPS) * g

    for h in range(ATT_HEADS):
        sl = slice(h * HD, (h + 1) * HD)
        aq_ref[:, sl] = _rot(head_norm(zq_ref[:, sl], qn_ref[...]), ca, sa, HD // 2).astype(aq_ref.dtype)
    for h in range(ATT_KV_HEADS):
        sl = slice(h * HD, (h + 1) * HD)
        k = _rot(head_norm(zk_ref[:, sl], kn_ref[...]), ca, sa, HD // 2)
        ak_ref[:, sl] = k
        akb_ref[:, sl] = k.astype(akb_ref.dtype)
    v = zv_ref[...]
    av_ref[...] = v
    avb_ref[...] = v.astype(avb_ref.dtype)
    for h in range(IDX_HEADS):
        sl = slice(h * HD, (h + 1) * HD)
        iq_ref[:, sl] = (_rot(ziq_ref[:, sl], ci, si, IDX_ROPE_DIM // 2) * (IDX_DIM ** -0.5)).astype(iq_ref.dtype)
    x = zik_ref[...]
    mu = jnp.mean(x, axis=-1, keepdims=True)
    var = jnp.mean(jnp.square(x - mu), axis=-1, keepdims=True)
    ik = _rot((x - mu) * lax.rsqrt(var + EPS) * ig_ref[...] + ib_ref[...], ci, si, IDX_ROPE_DIM // 2)
    ik_ref[...] = ik
    ikb_ref[...] = ik.astype(ikb_ref.dtype)
    iw_ref[...] = ziw_ref[...] * (IDX_HEADS ** -0.5)


def _prelude(z, l, tabs, q_norm, k_norm, idx_g, idx_b, t_len):
    m = z.shape[0]
    _, dst, padded, _, _ = _segments()
    tr = min(m, 256)
    nt = t_len // tr if t_len >= tr else 1
    lowp = BF16 if tr >= 16 else F32
    aw, kvw, iw = padded['a_q'], padded['a_k'], padded['i_q']

    def zspec(name):
        w = padded[name]
        return pl.BlockSpec((tr, w), lambda i: (i, dst[name] // w))

    tab_spec = pl.BlockSpec((tr, LANE), lambda i: (i % nt, 0))
    vec_spec = pl.BlockSpec((None, 1, HD), lambda i: (l, 0, 0))
    row = lambda w: pl.BlockSpec((tr, w), lambda i: (i, 0))
    shp = lambda w, dt: jax.ShapeDtypeStruct((m, w), dt)
    vec = lambda a: a.reshape(a.shape[0], 1, HD)
    return pl.pallas_call(
        _prelude_kernel,
        grid=(m // tr,),
        in_specs=[zspec('a_q'), zspec('a_k'), zspec('a_v'), zspec('i_q'), zspec('i_k'), zspec('i_w'),
                  tab_spec, tab_spec, tab_spec, tab_spec, vec_spec, vec_spec, vec_spec, vec_spec],
        out_specs=[row(aw), row(kvw), row(kvw), row(kvw), row(kvw), row(iw), row(HD), row(HD), row(LANE)],
        out_shape=[shp(aw, lowp), shp(kvw, F32), shp(kvw, F32), shp(kvw, lowp), shp(kvw, lowp),
                   shp(iw, lowp), shp(HD, F32), shp(HD, lowp), shp(LANE, F32)],
        compiler_params=_params(("parallel",)),
        name="attn_prelude",
    )(z, z, z, z, z, z, *tabs, vec(q_norm), vec(k_norm), vec(idx_g), vec(idx_b))


def _ret_gamma_log(h):
    return math.log(1.0 - 2.0 ** (-5.0 - h))


def _retention_kernel(lg_ref, q_ref, k_ref, v_ref, g_ref, cos_ref, sin_ref, nrm_ref, o_ref, st_ref, s_scr, *, nc):
    c = pl.program_id(2)
    cr = q_ref.shape[0]
    lg = lg_ref[:, 0:1]

    @pl.when(c == 0)
    def _():
        s_scr[...] = jnp.zeros_like(s_scr)

    cos, sin = cos_ref[...], sin_ref[...]
    q = _rot(q_ref[...], cos, sin, RET_DK // 2)
    k = _rot(k_ref[...], cos, sin, RET_DK // 2) * (RET_DK ** -0.5)
    v = v_ref[...]
    ti = lax.broadcasted_iota(jnp.int32, (cr, cr), 0)
    si = lax.broadcasted_iota(jnp.int32, (cr, cr), 1)
    diff = (ti - si).astype(F32)
    intra = jnp.where(diff >= 0, jnp.exp(lg * jnp.maximum(diff, 0.0)), 0.0)
    tcol = lax.broadcasted_iota(jnp.int32, (cr, 1), 0).astype(F32)
    q_dec = jnp.exp(lg * (tcol + 1.0))
    k_dec = jnp.exp(lg * (cr - 1.0 - tcol))
    s_dec = jnp.exp(lg * cr)
    s = s_scr[...]
    a = _bdot(q, k, _NT) * intra
    o = _bdot(a, v) + _bdot(q * q_dec, s)
    s_new = s * s_dec + _bdot(k * k_dec, v, _TN)
    s_scr[...] = s_new
    y = o * lax.rsqrt(jnp.mean(o * o, axis=-1, keepdims=True) + EPS) * nrm_ref[...]
    o_ref[...] = (y * _silu(g_ref[...])).astype(o_ref.dtype)

    @pl.when(c == nc - 1)
    def _():
        st_ref[...] = s_new


def _retention(z, l, cos, sin, ret_norm, b, t):
    _, dst, _, _, _ = _segments()
    cr = min(t, 256)
    nc = t // cr
    lg = jnp.broadcast_to(
        jnp.asarray([_ret_gamma_log(h) for h in range(RET_HEADS)], F32)[:, None, None], (RET_HEADS, 1, LANE))

    def zspec(name):
        return pl.BlockSpec((cr, HD), lambda bi, h, c: (bi * nc + c, dst[name] // HD + h))

    tab = pl.BlockSpec((cr, LANE), lambda bi, h, c: (c, 0))
    return pl.pallas_call(
        functools.partial(_retention_kernel, nc=nc),
        grid=(b, RET_HEADS, nc),
        in_specs=[pl.BlockSpec((None, 1, LANE), lambda bi, h, c: (h, 0, 0)),
                  zspec('r_q'), zspec('r_k'), zspec('r_v'), zspec('r_g'), tab, tab,
                  pl.BlockSpec((None, 1, HD), lambda bi, h, c: (l, 0, h))],
        out_specs=[pl.BlockSpec((cr, HD), lambda bi, h, c: (bi * nc + c, h)),
                   pl.BlockSpec((None, None, RET_DK, RET_DV), lambda bi, h, c: (bi, h, 0, 0))],
        out_shape=[jax.ShapeDtypeStruct((b * t, RET_HEADS * RET_DV), BF16),
                   jax.ShapeDtypeStruct((b, RET_HEADS, RET_DK, RET_DV), F32)],
        scratch_shapes=[pltpu.VMEM((RET_DK, RET_DV), F32)],
        compiler_params=_params(("parallel", "parallel", "arbitrary")),
        name="retention",
    )(lg, z, z, z, z, cos, sin, ret_norm.reshape(ret_norm.shape[0], 1, -1))


def _hgrn_gates(fa, lb):
    log_f = jnp.minimum(fa, 0.0) - jnp.log1p(jnp.exp(-jnp.abs(fa))) + jnp.log1p(lb * jnp.exp(-fa))
    hk = (1.0 - lb) * jax.nn.sigmoid(-fa)
    return log_f, hk


def _split3(x):
    hi = x.astype(BF16)
    r1 = x - hi.astype(F32)
    mid = r1.astype(BF16)
    lo = (r1 - mid.astype(F32)).astype(BF16)
    return hi, mid, lo


def _gla_chunk(q, k, v, g, st):
    c = GLA_CHUNK
    ti = lax.broadcasted_iota(jnp.int32, (c, c), 0)
    si = lax.broadcasted_iota(jnp.int32, (c, c), 1)
    tri = (ti >= si).astype(BF16)
    hi, mid, lo = _split3(g)
    b = (jnp.dot(tri, hi, preferred_element_type=F32) + jnp.dot(tri, mid, preferred_element_type=F32)
         + jnp.dot(tri, lo, preferred_element_type=F32))
    b_last = b[c - 1:c, :]
    o = _bdot(q * jnp.exp(b), st, _NT)
    kd = k * jnp.exp(b_last - b)
    st_new = st * jnp.exp(b_last) + _bdot(v, kd, _TN)

    nsub = c // GLA_SUB
    a_off = jnp.zeros((c, c), F32)
    for i in range(1, nsub):
        m = b[i * GLA_SUB - 1:i * GLA_SUB, :]
        qs = q * jnp.exp(jnp.minimum(b - m, 0.0))
        ks = k * jnp.exp(jnp.minimum(m - b, 0.0))
        blk = (ti // GLA_SUB == i) & (si < i * GLA_SUB)
        a_off = a_off + jnp.where(blk, _bdot(qs, ks, _NT), 0.0)
    o = o + _bdot(a_off, v)

    rows = lax.broadcasted_iota(jnp.int32, (GLA_SUB, 1), 0)
    diag = []
    for i in range(nsub):
        sl = slice(i * GLA_SUB, (i + 1) * GLA_SUB)
        qi, ki, vi, bi = q[sl], k[sl], v[sl], b[sl]
        oi = jnp.zeros((GLA_SUB, HD), F32)
        for s in range(GLA_SUB):
            d = jnp.exp(jnp.minimum(bi - bi[s:s + 1], 0.0)) * qi * ki[s:s + 1]
            w = jnp.where(rows >= s, jnp.sum(d, axis=-1, keepdims=True), 0.0)
            oi = oi + w * vi[s:s + 1]
        diag.append(oi)
    return o + jnp.concatenate(diag, axis=0), st_new


def _gla_kernel(f_ref, q_ref, i_ref, g_ref, lb_ref, nrm_ref, o_ref, st_ref, s_scr, *, nc, n_inner):
    c = pl.program_id(2)

    @pl.when(c == 0)
    def _():
        s_scr[...] = jnp.zeros_like(s_scr)

    lb = lb_ref[...]
    nrm = nrm_ref[...]

    def body(ci, carry):
        rows = pl.ds(pl.multiple_of(ci * GLA_CHUNK, GLA_CHUNK), GLA_CHUNK)
        log_f, hk = _hgrn_gates(f_ref[rows, :], lb)
        hq = _silu(q_ref[rows, :])
        o, st_new = _gla_chunk(hq, hk, i_ref[rows, :], log_f, s_scr[...])
        s_scr[...] = st_new
        y = o * lax.rsqrt(jnp.mean(o * o, axis=-1, keepdims=True) + EPS) * nrm
        o_ref[rows, :] = (y * jax.nn.sigmoid(g_ref[rows, :])).astype(o_ref.dtype)
        return carry

    lax.fori_loop(0, n_inner, body, 0)

    @pl.when(c == nc - 1)
    def _():
        st_ref[...] = s_scr[...].T


def _gla(z, l, lb, hg_norm, b, t):
    _, dst, _, _, _ = _segments()
    blk = min(t, 256)
    assert blk % GLA_CHUNK == 0
    nc = t // blk

    def zspec(name):
        return pl.BlockSpec((blk, HD), lambda bi, h, c: (bi * nc + c, dst[name] // HD + h))

    vec = lambda: pl.BlockSpec((None, 1, HD), lambda bi, h, c: (l, 0, h))
    return pl.pallas_call(
        functools.partial(_gla_kernel, nc=nc, n_inner=blk // GLA_CHUNK),
        grid=(b, HG_HEADS, nc),
        in_specs=[zspec('h_f'), zspec('h_q'), zspec('h_i'), zspec('h_g'), vec(), vec()],
        out_specs=[pl.BlockSpec((blk, HD), lambda bi, h, c: (bi * nc + c, h)),
                   pl.BlockSpec((None, None, HG_DK, HG_DV), lambda bi, h, c: (bi, h, 0, 0))],
        out_shape=[jax.ShapeDtypeStruct((b * t, HG_HEADS * HG_DV), BF16),
                   jax.ShapeDtypeStruct((b, HG_HEADS, HG_DK, HG_DV), F32)],
        scratch_shapes=[pltpu.VMEM((HG_DV, HG_DK), F32)],
        compiler_params=_params(("parallel", "parallel", "arbitrary")),
        name="hgrn2",
    )(z, z, z, z, lb.reshape(lb.shape[0], 1, -1), hg_norm.reshape(hg_norm.shape[0], 1, -1))


def _to_col(row):
    n = row.shape[1]
    eye = lax.broadcasted_iota(jnp.int32, (n, n), 0) == lax.broadcasted_iota(jnp.int32, (n, n), 1)
    return jnp.sum(jnp.where(eye, row, 0.0), axis=1, keepdims=True)


def _decode_rec_kernel(rq_ref, rk_ref, rv_ref, rg_ref, hf_ref, hq_ref, hi_ref, hg_ref,
                       cos_ref, sin_ref, rn_ref, hn_ref, lb_ref, sr_ref, sh_ref,
                       or_ref, oh_ref, nr_ref, nh_ref):
    cos, sin = cos_ref[...], sin_ref[...]

    def rms(o, g):
        return o * lax.rsqrt(jnp.mean(o * o, axis=-1, keepdims=True) + EPS) * g

    for h in range(RET_HEADS):
        sl = slice(h * HD, (h + 1) * HD)
        gamma = math.exp(_ret_gamma_log(h))
        q = _rot(rq_ref[:, sl], cos, sin, RET_DK // 2)
        k = _rot(rk_ref[:, sl], cos, sin, RET_DK // 2) * (RET_DK ** -0.5)
        v = rv_ref[:, sl]
        s = sr_ref[h]
        o = jnp.sum(q * k, axis=-1, keepdims=True) * v + jnp.sum(_to_col(q * gamma) * s, axis=0, keepdims=True)
        nr_ref[h] = s * gamma + _to_col(k) * v
        or_ref[:, sl] = rms(o, rn_ref[:, sl]) * _silu(rg_ref[:, sl])

    for h in range(HG_HEADS):
        sl = slice(h * HD, (h + 1) * HD)
        log_f, k = _hgrn_gates(hf_ref[:, sl], lb_ref[:, sl])
        q = _silu(hq_ref[:, sl])
        v = hi_ref[:, sl]
        s = sh_ref[h]
        eb = jnp.exp(log_f)
        o = jnp.sum(q * k, axis=-1, keepdims=True) * v + jnp.sum(_to_col(q * eb) * s, axis=0, keepdims=True)
        nh_ref[h] = s * _to_col(eb) + _to_col(k) * v
        oh_ref[:, sl] = rms(o, hn_ref[:, sl]) * jax.nn.sigmoid(hg_ref[:, sl])


def _decode_rec(z, l, cos, sin, ret_norm, hg_norm, lb, state_ret, state_hgrn):
    _, dst, padded, _, _ = _segments()
    nb = z.shape[0]

    z = z.reshape(nb, 1, z.shape[1])

    def zspec(name):
        w = padded[name]
        return pl.BlockSpec((None, 1, w), lambda bi: (bi, 0, dst[name] // w))

    one = lambda w: pl.BlockSpec((1, w), lambda bi: (0, 0))
    vec = lambda a: pl.BlockSpec((None, 1, a.shape[-1]), lambda bi: (l, 0, 0))
    st = lambda a: pl.BlockSpec((None, None) + a.shape[2:], lambda bi: (l, bi, 0, 0, 0))
    st_out = lambda a: pl.BlockSpec((None,) + a.shape[2:], lambda bi: (bi, 0, 0, 0))
    rw, hw = RET_HEADS * RET_DV, HG_HEADS * HG_DV
    r3 = lambda a: a.reshape(a.shape[0], 1, -1)
    return pl.pallas_call(
        _decode_rec_kernel,
        grid=(nb,),
        in_specs=[zspec('r_q'), zspec('r_k'), zspec('r_v'), zspec('r_g'),
                  zspec('h_f'), zspec('h_q'), zspec('h_i'), zspec('h_g'),
                  one(LANE), one(LANE), vec(ret_norm), vec(hg_norm), vec(lb), st(state_ret), st(state_hgrn)],
        out_specs=[pl.BlockSpec((None, 1, rw), lambda bi: (bi, 0, 0)),
                   pl.BlockSpec((None, 1, hw), lambda bi: (bi, 0, 0)),
                   st_out(state_ret), st_out(state_hgrn)],
        out_shape=[jax.ShapeDtypeStruct((nb, 1, rw), F32), jax.ShapeDtypeStruct((nb, 1, hw), F32),
                   jax.ShapeDtypeStruct(state_ret.shape[1:], F32), jax.ShapeDtypeStruct(state_hgrn.shape[1:], F32)],
        compiler_params=_params(("arbitrary",)),
        name="decode_recurrent",
    )(z, z, z, z, z, z, z, z, cos, sin, r3(ret_norm), r3(hg_norm), r3(lb), state_ret, state_hgrn)


def _sort_key(score):
    bits = lax.bitcast_convert_type(score + 0.0, jnp.int32)
    return bits ^ ((bits >> 31) & jnp.int32(0x7FFFFFFF))


def _count(mask):
    return jnp.sum(mask.astype(F32), axis=-1, keepdims=True)


def _nth_largest_key(count_ge, n_sel, shape):
    lo = jnp.where(count_ge(jnp.zeros(shape, jnp.int32)) >= n_sel, 0, INT_MIN).astype(jnp.int32)

    def body(i, lo):
        cand = lo | jnp.left_shift(jnp.int32(1), 30 - i)
        return jnp.where(count_ge(cand) >= n_sel, cand, lo)

    return lax.fori_loop(0, 31, body, lo)


def _tie_bound(count_eq_below, need, nbits, shape):
    def body(i, j):
        cand = j | jnp.left_shift(jnp.int32(1), nbits - 1 - i)
        return jnp.where(count_eq_below(cand) < need, cand, j)

    return lax.fori_loop(0, nbits, body, jnp.zeros(shape, jnp.int32))


def _dsa_prompt_kernel(aq_ref, iq_ref, iw_ref, k_ref, v_ref, ik_ref, o_ref, key_scr, sel_scr, *, n_sel, nbits):
    j = pl.program_id(1)
    tq = aq_ref.shape[0]
    t = k_ref.shape[0]
    ik = ik_ref[...]
    score = jnp.zeros((tq, t), F32)
    for h in range(IDX_HEADS):
        s = lax.dot_general(iq_ref[:, h * HD:(h + 1) * HD], ik, _NT, preferred_element_type=F32)
        score = score + jnp.maximum(s, 0.0) * iw_ref[:, h:h + 1]
    q_pos = j * tq + lax.broadcasted_iota(jnp.int32, (tq, 1), 0)
    col = lax.broadcasted_iota(jnp.int32, (tq, t), 1)
    visible = col <= q_pos
    key_scr[...] = _sort_key(jnp.where(visible, score, NEG_BIG))

    tau = _nth_largest_key(lambda c: _count(key_scr[...] >= c), n_sel, (tq, 1))
    key = key_scr[...]
    gt = key > tau
    eq = key == tau
    need = n_sel - _count(gt)
    bound = _tie_bound(lambda c: _count((key_scr[...] == tau) & (col < c)), need, nbits, (tq, 1))
    sel_scr[...] = ((gt | (eq & (col <= bound))) & visible).astype(F32)

    group = ATT_HEADS // ATT_KV_HEADS
    scale = ATT_HEAD_DIM ** -0.5
    for n in range(ATT_KV_HEADS):
        kn = k_ref[:, n * HD:(n + 1) * HD]
        vn = v_ref[:, n * HD:(n + 1) * HD]
        for g in range(group):
            sl = slice((n * group + g) * HD, (n * group + g + 1) * HD)
            s = lax.dot_general(aq_ref[:, sl], kn, _NT, preferred_element_type=F32) * scale
            s = jnp.where(sel_scr[...] > 0.0, s, NEG_BIG)
            m = jnp.max(s, axis=-1, keepdims=True)
            p = jnp.exp(s - m)
            o = _bdot(p, vn) / jnp.sum(p, axis=-1, keepdims=True)
            o_ref[:, sl] = o.astype(o_ref.dtype)


def _dsa_prompt(aq, iq, iw, kb, vb, ikb, b, t):
    tq = min(Q_BLOCK, t)
    nq = t // tq
    n_sel = min(TOPK_MAX, t // 4)
    nbits = max(1, (t - 1).bit_length())
    qrow = lambda w: pl.BlockSpec((tq, w), lambda bi, j: (bi * nq + j, 0))
    krow = lambda w: pl.BlockSpec((t, w), lambda bi, j: (bi, 0))
    return pl.pallas_call(
        functools.partial(_dsa_prompt_kernel, n_sel=n_sel, nbits=nbits),
        grid=(b, nq),
        in_specs=[qrow(aq.shape[1]), qrow(iq.shape[1]), qrow(LANE),
                  krow(kb.shape[1]), krow(vb.shape[1]), krow(HD)],
        out_specs=qrow(aq.shape[1]),
        out_shape=jax.ShapeDtypeStruct(aq.shape, BF16),
        scratch_shapes=[pltpu.VMEM((tq, t), jnp.int32), pltpu.VMEM((tq, t), F32)],
        compiler_params=_params(("parallel", "arbitrary")),
        name="dsa_prompt",
    )(aq, iq, iw, kb, vb, ikb)


def _dec_score_kernel(pt_ref, iq_ref, iw_ref, ikn_ref, *refs):
    pages = refs[:PAGES_PER_STEP]
    sc_ref, new_ref = refs[PAGES_PER_STEP:]
    iq = iq_ref[...].astype(BF16)
    iw = iw_ref[...]
    ik = jnp.concatenate([p[...].astype(BF16) for p in pages], axis=0)
    s = lax.dot_general(iq, ik, _NT, preferred_element_type=F32)
    sc_ref[...] = jnp.sum(jnp.maximum(s, 0.0) * iw, axis=0, keepdims=True)

    @pl.when(pl.program_id(1) == 0)
    def _():
        sn = lax.dot_general(iq, jnp.broadcast_to(ikn_ref[...], (8, HD)).astype(BF16), _NT,
                             preferred_element_type=F32)[:, 0:1]
        new_ref[...] = jnp.broadcast_to(jnp.sum(jnp.maximum(sn, 0.0) * iw, axis=0, keepdims=True), (1, LANE))


def _dec_scores(page_table, iq, iw, ik_new, cache_ik, l):
    nb, n_pages = page_table.shape
    steps = n_pages // PAGES_PER_STEP
    ih = iq.shape[1]
    page = lambda r: pl.BlockSpec((None, None, PAGE_SIZE, HD),
                                  lambda bi, p, pt: (l, pt[bi, p * PAGES_PER_STEP + r], 0, 0))
    grid_spec = pltpu.PrefetchScalarGridSpec(
        num_scalar_prefetch=1,
        grid=(nb, steps),
        in_specs=[pl.BlockSpec((None, ih, HD), lambda bi, p, pt: (bi, 0, 0)),
                  pl.BlockSpec((None, ih, 1), lambda bi, p, pt: (bi, 0, 0)),
                  pl.BlockSpec((None, 1, HD), lambda bi, p, pt: (bi, 0, 0))]
                 + [page(r) for r in range(PAGES_PER_STEP)],
        out_specs=[pl.BlockSpec((None, 1, PAGES_PER_STEP * PAGE_SIZE), lambda bi, p, pt: (bi, 0, p)),
                   pl.BlockSpec((None, 1, LANE), lambda bi, p, pt: (bi, 0, 0))],
    )
    return pl.pallas_call(
        _dec_score_kernel,
        grid_spec=grid_spec,
        out_shape=[jax.ShapeDtypeStruct((nb, 1, n_pages * PAGE_SIZE), F32),
                   jax.ShapeDtypeStruct((nb, 1, LANE), F32)],
        compiler_params=_params(("parallel", "arbitrary")),
        name="dec_scores",
    )(page_table, iq, iw, ik_new, *([cache_ik] * PAGES_PER_STEP))


def _dec_attn_kernel(pt_ref, sc_all_ref, sc_new_ref, sc_ref, q_ref, kn_ref, vn_ref, *refs,
                     n_sel, nbits, steps):
    kp = refs[:PAGES_PER_STEP]
    vp = refs[PAGES_PER_STEP:2 * PAGES_PER_STEP]
    o_ref, tau_scr, bnd_scr, m_scr, l_scr, acc_scr = refs[2 * PAGES_PER_STEP:]
    p_id = pl.program_id(1)
    chunk = PAGES_PER_STEP * PAGE_SIZE
    scale = ATT_HEAD_DIM ** -0.5
    lane0 = lax.broadcasted_iota(jnp.int32, (1, LANE), 1) == 0

    @pl.when(p_id == 0)
    def _():
        key = _sort_key(sc_all_ref[...])
        key_new = _sort_key(sc_new_ref[...])
        idx = lax.broadcasted_iota(jnp.int32, key.shape, 1)
        cnt = lambda mp, mn: _count(mp) + _count(mn & lane0)
        tau = _nth_largest_key(lambda c: cnt(key >= c, key_new >= c), n_sel, (1, 1))
        need = n_sel - cnt(key > tau, key_new > tau)
        bound = _tie_bound(lambda c: _count((key == tau) & (idx < c)), need, nbits, (1, 1))
        tau_scr[...] = jnp.broadcast_to(tau, tau_scr.shape)
        bnd_scr[...] = jnp.broadcast_to(bound, bnd_scr.shape)
        m_scr[...] = jnp.full_like(m_scr, NEG_BIG)
        l_scr[...] = jnp.zeros_like(l_scr)
        acc_scr[...] = jnp.zeros_like(acc_scr)

    tau = tau_scr[:, 0:1]
    bound = bnd_scr[:, 0:1]
    key_c = _sort_key(sc_ref[...])
    idx_c = p_id * chunk + lax.broadcasted_iota(jnp.int32, key_c.shape, 1)
    sel = (key_c > tau) | ((key_c == tau) & (idx_c <= bound))
    q = q_ref[...].astype(BF16)
    kc = jnp.concatenate([r[...].astype(BF16) for r in kp], axis=0)
    vc = jnp.concatenate([r[...].astype(BF16) for r in vp], axis=0)
    s = lax.dot_general(q, kc, _NT, preferred_element_type=F32) * scale
    s = jnp.where(sel, s, NEG_BIG)
    m_old = m_scr[:, 0:1]
    m_new = jnp.maximum(m_old, jnp.max(s, axis=-1, keepdims=True))
    alpha = jnp.exp(m_old - m_new)
    p = jnp.where(sel, jnp.exp(s - m_new), 0.0)
    l_new = alpha * l_scr[:, 0:1] + jnp.sum(p, axis=-1, keepdims=True)
    acc_new = alpha * acc_scr[...] + _bdot(p, vc)
    m_scr[...] = jnp.broadcast_to(m_new, m_scr.shape)
    l_scr[...] = jnp.broadcast_to(l_new, l_scr.shape)
    acc_scr[...] = acc_new

    @pl.when(p_id == steps - 1)
    def _():
        key_all = _sort_key(sc_all_ref[...])
        key_new = _sort_key(sc_new_ref[...])[:, 0:1]
        need = n_sel - _count(key_all > tau) - (key_new > tau).astype(F32)
        n_eq = _count(key_all == tau)
        sel_new = (key_new > tau) | ((key_new == tau) & (n_eq < need))
        qf = q_ref[...]
        s_new = jnp.sum(qf * kn_ref[...], axis=-1, keepdims=True) * scale
        s_new = jnp.where(sel_new, s_new, NEG_BIG)
        m_fin = jnp.maximum(m_new, s_new)
        a2 = jnp.exp(m_new - m_fin)
        p_new = jnp.where(sel_new, jnp.exp(s_new - m_fin), 0.0)
        l_fin = a2 * l_new + p_new
        acc_fin = (a2 * acc_new + p_new * vn_ref[...]) / l_fin
        group = ATT_HEADS // ATT_KV_HEADS
        for h in range(ATT_HEADS):
            n = h // group
            o_ref[:, h * HD:(h + 1) * HD] = acc_fin[h:h + 1, n * HD:(n + 1) * HD]


def _dec_attn(page_table, scores, score_new, q_bd, k_new, v_new, cache_k, cache_v, l):
    nb, n_pages = page_table.shape
    steps = n_pages // PAGES_PER_STEP
    s_len = n_pages * PAGE_SIZE
    chunk = PAGES_PER_STEP * PAGE_SIZE
    n_sel = min(TOPK_MAX, (s_len + 1) // 4)
    nbits = s_len.bit_length()
    kvw = cache_k.shape[-1]
    hp = q_bd.shape[1]
    page = lambda r: pl.BlockSpec((None, None, PAGE_SIZE, kvw),
                                  lambda bi, p, pt: (l, pt[bi, p * PAGES_PER_STEP + r], 0, 0))
    grid_spec = pltpu.PrefetchScalarGridSpec(
        num_scalar_prefetch=1,
        grid=(nb, steps),
        in_specs=[pl.BlockSpec((None, 1, s_len), lambda bi, p, pt: (bi, 0, 0)),
                  pl.BlockSpec((None, 1, LANE), lambda bi, p, pt: (bi, 0, 0)),
                  pl.BlockSpec((None, 1, chunk), lambda bi, p, pt: (bi, 0, p)),
                  pl.BlockSpec((None, hp, kvw), lambda bi, p, pt: (bi, 0, 0)),
                  pl.BlockSpec((None, 1, kvw), lambda bi, p, pt: (bi, 0, 0)),
                  pl.BlockSpec((None, 1, kvw), lambda bi, p, pt: (bi, 0, 0))]
                 + [page(r) for r in range(PAGES_PER_STEP)] * 2,
        out_specs=pl.BlockSpec((None, 1, ATT_HEADS * HD), lambda bi, p, pt: (bi, 0, 0)),
        scratch_shapes=[pltpu.VMEM((1, LANE), jnp.int32), pltpu.VMEM((1, LANE), jnp.int32),
                        pltpu.VMEM((hp, LANE), F32), pltpu.VMEM((hp, LANE), F32),
                        pltpu.VMEM((hp, kvw), F32)],
    )
    return pl.pallas_call(
        functools.partial(_dec_attn_kernel, n_sel=n_sel, nbits=nbits, steps=steps),
        grid_spec=grid_spec,
        out_shape=jax.ShapeDtypeStruct((nb, 1, ATT_HEADS * HD), F32),
        compiler_params=_params(("parallel", "arbitrary")),
        name="dec_attn",
    )(page_table, scores, score_new, scores, q_bd, k_new, v_new,
      *([cache_k] * PAGES_PER_STEP), *([cache_v] * PAGES_PER_STEP)).reshape(nb, ATT_HEADS * HD)


def _ffn(x, norm, w1, w3, w2, l, lowp):
    h = _rmsnorm(x, norm, l, lowp)
    g = _matmul(h, [w1, w3], l, epilogue='swiglu', out_dtype=lowp)
    half = g.shape[1] // 2
    tk = half if half % LANE == 0 else g.shape[1]
    return _matmul(g, [w2], l, epilogue='resid', res=x, scale=0.5, tk=tk)


def kernel(x_prompt, x_sample, state_ret, state_hgrn, cache_k, cache_v, cache_idx_k, page_table, ffn1_norm, ffn1_w1, ffn1_w3, ffn1_w2, mix_norm, w_in, ret_norm, q_norm, k_norm, idx_k_g, idx_k_b, hg_lb_raw, hg_norm, w_up_ret, w_up_att, w_up_hg, w_out, ffn2_norm, ffn2_w1, ffn2_w3, ffn2_w2):
    b, t, d = x_prompt.shape
    nb = x_sample.shape[0]
    depth = w_in.shape[0]
    kvw = ATT_KV_HEADS * ATT_HEAD_DIM
    group = ATT_HEADS // ATT_KV_HEADS

    lb_soft = jax.nn.softmax(hg_lb_raw.astype(F32), axis=0)
    lb_all = jnp.cumsum(lb_soft, axis=0) - lb_soft[0]
    w_in_p = _permute_w_in(w_in)

    pos_p = jnp.arange(t, dtype=jnp.int32)
    pos_s = jnp.full((nb,), PAST_LEN, jnp.int32)
    ret_f = 1.0 / (ROPE_THETA ** jnp.linspace(0.0, 1.0, RET_DK // 2, dtype=F32))
    att_f = ROPE_THETA ** (-jnp.arange(0, ATT_HEAD_DIM, 2, dtype=F32) / ATT_HEAD_DIM)
    idx_f = ROPE_THETA ** (-jnp.arange(0, IDX_ROPE_DIM, 2, dtype=F32) / IDX_ROPE_DIM)
    tabs_p = _rope_tables(pos_p, att_f, ATT_HEAD_DIM) + _rope_tables(pos_p, idx_f, IDX_ROPE_DIM)
    tabs_s = _rope_tables(pos_s, att_f, ATT_HEAD_DIM) + _rope_tables(pos_s, idx_f, IDX_ROPE_DIM)
    ret_tab_p = _rope_tables(pos_p, ret_f, RET_DK)
    ret_tab_s = _rope_tables(pos_s[:1], ret_f, RET_DK)

    ck = cache_k.reshape(cache_k.shape[:3] + (kvw,))
    cv = cache_v.reshape(cache_v.shape[:3] + (kvw,))

    xp = x_prompt.reshape(b * t, d)
    xs = x_sample.reshape(nb, d)
    outs = {n: [] for n in ('rp', 'rs', 'hp', 'hs', 'kp', 'vp', 'ip', 'ks', 'vs', 'is')}
    for l in range(depth):
        xp = _ffn(xp, ffn1_norm, ffn1_w1, ffn1_w3, ffn1_w2, l, BF16)
        h = _rmsnorm(xp, mix_norm, l, BF16)
        z = _matmul(h, [w_in_p], l)
        aq, ak, av, akb, avb, iq, ik, ikb, iw = _prelude(z, l, tabs_p, q_norm, k_norm, idx_k_g, idx_k_b, t)
        a_ret, r_p = _retention(z, l, ret_tab_p[0], ret_tab_p[1], ret_norm, b, t)
        a_hg, h_p = _gla(z, l, lb_all, hg_norm, b, t)
        a_att = _dsa_prompt(aq, iq, iw, akb, avb, ikb, b, t)
        merged = _merge(a_ret, a_att, a_hg, w_up_ret, w_up_att, w_up_hg, z, l, BF16)
        xp = _matmul(merged, [w_out], l, epilogue='resid', res=xp, scale=1.0)
        xp = _ffn(xp, ffn2_norm, ffn2_w1, ffn2_w3, ffn2_w2, l, BF16)
        outs['rp'].append(r_p); outs['hp'].append(h_p)
        outs['kp'].append(ak.reshape(b, t, ATT_KV_HEADS, ATT_HEAD_DIM))
        outs['vp'].append(av.reshape(b, t, ATT_KV_HEADS, ATT_HEAD_DIM))
        outs['ip'].append(ik.reshape(b, t, IDX_DIM))

        xs = _ffn(xs, ffn1_norm, ffn1_w1, ffn1_w3, ffn1_w2, l, F32)
        h = _rmsnorm(xs, mix_norm, l, F32)
        z = _matmul(h, [w_in_p], l)
        aq, ak, av, _, _, iq, ik, _, iw = _prelude(z, l, tabs_s, q_norm, k_norm, idx_k_g, idx_k_b, 1)
        a_ret, a_hg, r_s, h_s = _decode_rec(z, l, ret_tab_s[0], ret_tab_s[1], ret_norm, hg_norm, lb_all,
                                            state_ret, state_hgrn)
        a_ret, a_hg = a_ret.reshape(nb, -1), a_hg.reshape(nb, -1)
        scores, score_new = _dec_scores(page_table, iq.reshape(nb, IDX_HEADS, IDX_DIM),
                                        iw[:, :IDX_HEADS].reshape(nb, IDX_HEADS, 1),
                                        ik.reshape(nb, 1, IDX_DIM), cache_idx_k, l)
        qh = aq.reshape(nb, ATT_KV_HEADS, group, ATT_HEAD_DIM)
        q_bd = jnp.concatenate(
            [jnp.concatenate([qh[:, n] if m == n else jnp.zeros_like(qh[:, n]) for m in range(ATT_KV_HEADS)], axis=-1)
             for n in range(ATT_KV_HEADS)]
            + [jnp.zeros((nb, max(0, 16 - ATT_HEADS), kvw), aq.dtype)], axis=1)
        a_att = _dec_attn(page_table, scores, score_new, q_bd, ak.reshape(nb, 1, kvw), av.reshape(nb, 1, kvw),
                          ck, cv, l)
        merged = _merge(a_ret, a_att, a_hg, w_up_ret, w_up_att, w_up_hg, z, l, F32)
        xs = _matmul(merged, [w_out], l, epilogue='resid', res=xs, scale=1.0)
        xs = _ffn(xs, ffn2_norm, ffn2_w1, ffn2_w3, ffn2_w2, l, F32)
        outs['rs'].append(r_s); outs['hs'].append(h_s)
        outs['ks'].append(ak.reshape(nb, 1, ATT_KV_HEADS, ATT_HEAD_DIM))
        outs['vs'].append(av.reshape(nb, 1, ATT_KV_HEADS, ATT_HEAD_DIM))
        outs['is'].append(ik.reshape(nb, 1, IDX_DIM))

    st = lambda n: jnp.stack(outs[n])
    return (xp.reshape(b, t, d), xs.reshape(nb, 1, d),
            st('rp').astype(state_ret.dtype), st('rs').astype(state_ret.dtype),
            st('hp').astype(state_hgrn.dtype), st('hs').astype(state_hgrn.dtype),
            st('kp'), st('vp'), st('ip'), st('ks'), st('vs'), st('is'))
```

```python
import functools
import math

import numpy as np
import jax
import jax.numpy as jnp
from jax import lax
from jax.experimental import pallas as pl
from jax.experimental.pallas import tpu as pltpu

D_MODEL = 4096
BATCH = 4
SEQ = 2048
DEPTH = 2
DEC_BATCH = 8
DEC_SEQ = 1
PAST_LEN = 16384
PAGE_SIZE = 128

RET_HEADS = 8
RET_DK = 128
RET_DV = 128
ATT_HEADS = 8
ATT_KV_HEADS = 2
ATT_HEAD_DIM = 128
IDX_HEADS = 16
IDX_DIM = 128
IDX_ROPE_DIM = 64
TOPK_MAX = 256
Q_BLOCK = 128
HG_HEADS = 8
HG_DK = 128
HG_DV = 128
D_FF = 11008
ROPE_THETA = 10000.0
EPS = 1e-6
NEG_BIG = -1e30

F32 = jnp.float32
BF16 = jnp.bfloat16
LANE = 128
HD = 128
INT_MIN = -(2 ** 31)
VMEM_LIMIT = 56 * 1024 * 1024
GLA_CHUNK = 64
GLA_SUB = 16
PAGES_PER_STEP = 8

_NT = (((1,), (1,)), ((), ()))
_TN = (((0,), (0,)), ((), ()))


def _params(sem):
    return pltpu.CompilerParams(dimension_semantics=sem, vmem_limit_bytes=VMEM_LIMIT)


def _bdot(a, b, dims=None):
    a = a.astype(BF16)
    b = b.astype(BF16)
    if dims is None:
        return jnp.dot(a, b, preferred_element_type=F32)
    return lax.dot_general(a, b, dims, preferred_element_type=F32)


def _silu(x):
    return x * jax.nn.sigmoid(x)


def _segments():
    ret_qk = RET_HEADS * RET_DK
    ret_w = RET_HEADS * RET_DV
    att_w = ATT_HEADS * ATT_HEAD_DIM
    kv_w = ATT_KV_HEADS * ATT_HEAD_DIM
    hg_k = HG_HEADS * HG_DK
    hg_w = HG_HEADS * HG_DV
    names = ['r_q', 'r_k', 'r_v', 'r_g', 'a_q', 'a_k', 'a_v', 'i_q', 'i_k', 'i_w',
             'h_f', 'h_q', 'h_i', 'h_g', 'g_ret', 'g_att', 'g_hg']
    widths = [ret_qk, ret_qk, ret_w, ret_w, att_w, kv_w, kv_w, IDX_HEADS * IDX_DIM, IDX_DIM, IDX_HEADS,
              hg_k, hg_k, hg_w, hg_w, D_MODEL, D_MODEL, D_MODEL]
    src = {}
    off = 0
    for n, w in zip(names, widths):
        src[n] = (off, w)
        off += w
    padded = {n: -(-w // LANE) * LANE for n, w in zip(names, widths)}
    order = ['g_ret', 'g_att', 'g_hg'] + sorted(
        [n for n in names if not n.startswith('g_')], key=lambda n: -padded[n])
    dst = {}
    off = 0
    for n in order:
        dst[n] = off
        off += padded[n]
    total = -(-off // 256) * 256
    for n in order:
        assert dst[n] % padded[n] == 0 or n.startswith('g_'), (n, dst[n], padded[n])
    return src, dst, padded, order, total


def _repack_kernel(blk_ref, sh_ref, valid_ref, a_ref, b_ref, o_ref, *, shift):
    c = pl.program_id(1)
    a = a_ref[...]
    lane = lax.broadcasted_iota(jnp.int32, a.shape, 1)
    sh = sh_ref[c]
    shifted = jnp.where(lane < LANE - shift, pltpu.roll(a, LANE - shift, 1),
                        pltpu.roll(b_ref[...], LANE - shift, 1))
    out = jnp.where(sh != 0, shifted, a)
    o_ref[...] = jnp.where(lane < valid_ref[c], out, 0.0).astype(o_ref.dtype)


def _repack_w_in(w_in):
    src, dst, padded, order, total = _segments()
    depth, kdim, n_in = w_in.shape
    n_src_blk = -(-n_in // LANE)
    shift = IDX_HEADS % LANE
    blk = np.zeros((total // LANE,), np.int32)
    sh = np.zeros((total // LANE,), np.int32)
    valid = np.zeros((total // LANE,), np.int32)
    for n in order:
        s, w = src[n]
        for t in range(padded[n] // LANE):
            c = dst[n] // LANE + t
            col = s + t * LANE
            assert col % LANE in (0, shift)
            blk[c], sh[c], valid[c] = col // LANE, col % LANE, min(LANE, w - t * LANE)
    grid_spec = pltpu.PrefetchScalarGridSpec(
        num_scalar_prefetch=3,
        grid=(depth, total // LANE),
        in_specs=[pl.BlockSpec((None, kdim, LANE), lambda l, c, blk, sh, valid: (l, 0, blk[c])),
                  pl.BlockSpec((None, kdim, LANE),
                               lambda l, c, blk, sh, valid: (l, 0, jnp.minimum(blk[c] + 1, n_src_blk - 1)))],
        out_specs=pl.BlockSpec((None, kdim, LANE), lambda l, c, blk, sh, valid: (l, 0, c)),
    )
    return pl.pallas_call(
        functools.partial(_repack_kernel, shift=shift),
        grid_spec=grid_spec,
        out_shape=jax.ShapeDtypeStruct((depth, kdim, total), BF16),
        compiler_params=_params(("parallel", "parallel")),
        name="repack_w_in",
    )(jnp.asarray(blk), jnp.asarray(sh), jnp.asarray(valid), w_in, w_in)


def _rmsnorm_kernel(x_ref, g_ref, o_ref):
    x = x_ref[...]
    ms = jnp.mean(x * x, axis=-1, keepdims=True)
    o_ref[...] = (x * lax.rsqrt(ms + EPS) * g_ref[...]).astype(o_ref.dtype)


def _row_tile(m, cap):
    best = None
    for t in range(16, min(m, cap) + 1, 16):
        if m % t == 0:
            best = t
    return best if best is not None else m


def _rmsnorm(x, g_all, l, out_dtype):
    m, d = x.shape
    tr = _row_tile(m, 512)
    return pl.pallas_call(
        _rmsnorm_kernel,
        grid=(m // tr,),
        in_specs=[pl.BlockSpec((tr, d), lambda i: (i, 0)),
                  pl.BlockSpec((None, 1, d), lambda i: (l, 0, 0))],
        out_specs=pl.BlockSpec((tr, d), lambda i: (i, 0)),
        out_shape=jax.ShapeDtypeStruct((m, d), out_dtype),
        compiler_params=_params(("parallel",)),
        name="rmsnorm",
    )(x, g_all.reshape(g_all.shape[0], 1, d))


def _mm_kernel(*refs, n_w, nk, epilogue, scale):
    lhs_ref = refs[0]
    w_refs = refs[1:1 + n_w]
    pos = 1 + n_w
    res_ref = None
    if epilogue == 'resid':
        res_ref = refs[pos]
        pos += 1
    out_ref = refs[pos]
    acc_refs = refs[pos + 1:]

    lhs = lhs_ref[...].astype(BF16)
    parts = [jnp.dot(lhs, w[...].astype(BF16), preferred_element_type=F32) for w in w_refs]

    def finish(vals):
        if epilogue == 'swiglu':
            out = _silu(vals[0]) * vals[1]
        elif epilogue == 'resid':
            out = res_ref[...] + scale * vals[0]
        else:
            out = vals[0]
        out_ref[...] = out.astype(out_ref.dtype)

    if nk == 1:
        finish(parts)
        return

    k = pl.program_id(2)

    @pl.when(k == 0)
    def _():
        for a, p in zip(acc_refs, parts):
            a[...] = p

    @pl.when(k > 0)
    def _():
        for a, p in zip(acc_refs, parts):
            a[...] += p

    @pl.when(k == nk - 1)
    def _():
        finish([a[...] for a in acc_refs])


def _pick(n, cands):
    for c in cands:
        if n % c == 0:
            return c
    return n


def _matmul(lhs, ws, l, *, tm, epilogue='plain', res=None, scale=1.0, out_dtype=F32, tn=256, tk=None):
    m, kdim = lhs.shape
    n = ws[0].shape[-1]
    tn = _pick(n, (tn, 256, 128))
    if tk is None:
        tk = kdim
    assert m % tm == 0 and n % tn == 0 and kdim % tk == 0
    nk = kdim // tk
    lhs_mode = dict(pipeline_mode=pl.Buffered(1)) if nk == 1 and tm * kdim * lhs.dtype.itemsize > (12 << 20) else {}
    in_specs = [pl.BlockSpec((tm, tk), lambda i, j, k: (i, k), **lhs_mode)]
    in_specs += [pl.BlockSpec((None, tk, tn), lambda i, j, k: (l, k, j)) for _ in ws]
    args = [lhs] + list(ws)
    if epilogue == 'resid':
        in_specs.append(pl.BlockSpec((tm, tn), lambda i, j, k: (i, j)))
        args.append(res)
    scratch = [pltpu.VMEM((tm, tn), F32) for _ in ws] if nk > 1 else []
    return pl.pallas_call(
        functools.partial(_mm_kernel, n_w=len(ws), nk=nk, epilogue=epilogue, scale=scale),
        grid=(m // tm, n // tn, nk),
        in_specs=in_specs,
        out_specs=pl.BlockSpec((tm, tn), lambda i, j, k: (i, j)),
        out_shape=jax.ShapeDtypeStruct((m, n), out_dtype),
        scratch_shapes=scratch,
        compiler_params=_params(("parallel", "parallel", "arbitrary")),
        name="mm_" + epilogue,
    )(*args)


def _merge_kernel(ar_ref, aa_ref, ah_ref, wr_ref, wa_ref, wh_ref, gr_ref, ga_ref, gh_ref, o_ref):
    u_r = _bdot(ar_ref[...], wr_ref[...])
    u_a = _bdot(aa_ref[...], wa_ref[...])
    u_h = _bdot(ah_ref[...], wh_ref[...])
    out = (jax.nn.sigmoid(gr_ref[...]) * u_r + jax.nn.sigmoid(ga_ref[...]) * u_a
           + jax.nn.sigmoid(gh_ref[...]) * u_h)
    o_ref[...] = out.astype(o_ref.dtype)


def _merge(a_ret, a_att, a_hg, w_r, w_a, w_h, z, l, out_dtype, tm):
    m = a_ret.shape[0]
    d = w_r.shape[-1]
    tn = _pick(d, (256, 128))
    nb = d // tn
    lhs_spec = lambda a: pl.BlockSpec((tm, a.shape[1]), lambda i, j: (i, 0))
    w_spec = lambda w: pl.BlockSpec((None, w.shape[1], tn), lambda i, j: (l, 0, j))
    gate_spec = lambda g: pl.BlockSpec((tm, tn), lambda i, j: (i, g * nb + j))
    return pl.pallas_call(
        _merge_kernel,
        grid=(m // tm, nb),
        in_specs=[lhs_spec(a_ret), lhs_spec(a_att), lhs_spec(a_hg), w_spec(w_r), w_spec(w_a), w_spec(w_h),
                  gate_spec(0), gate_spec(1), gate_spec(2)],
        out_specs=pl.BlockSpec((tm, tn), lambda i, j: (i, j)),
        out_shape=jax.ShapeDtypeStruct((m, d), out_dtype),
        compiler_params=_params(("parallel", "parallel")),
        name="merge",
    )(a_ret, a_att, a_hg, w_r, w_a, w_h, z, z, z)


def _rot(x, cos, sin, half):
    up = pltpu.roll(x, LANE - half, 1)
    dn = pltpu.roll(x, half, 1)
    lane = lax.broadcasted_iota(jnp.int32, x.shape, 1)
    return x * cos + jnp.where(lane < half, up, dn) * sin


def _rope_tables(pos, freqs, width):
    ang = pos.astype(F32)[:, None] * freqs[None, :]
    cos, sin = jnp.cos(ang), jnp.sin(ang)
    t = pos.shape[0]
    pad_c = jnp.ones((t, LANE - width), F32)
    pad_s = jnp.zeros((t, LANE - width), F32)
    return (jnp.concatenate([cos, cos, pad_c], axis=1), jnp.concatenate([-sin, sin, pad_s], axis=1))


def _prelude_kernel(zq_ref, zk_ref, zv_ref, ziq_ref, zik_ref, ziw_ref, ca_ref, sa_ref, ci_ref, si_ref,
                    qn_ref, kn_ref, ig_ref, ib_ref,
                    aq_ref, ak_ref, av_ref, akb_ref, avb_ref, iq_ref, ik_ref, ikb_ref, iw_ref):
    ca, sa, ci, si = ca_ref[...], sa_ref[...], ci_ref[...], si_ref[...]

    def head_norm(x, g):
        return x * lax.rsqrt(jnp.mean(x * x, axis=-1, keepdims=True) + EPS) * g

    for h in range(ATT_HEADS):
        sl = slice(h * HD, (h + 1) * HD)
        aq_ref[:, sl] = _rot(head_norm(zq_ref[:, sl], qn_ref[...]), ca, sa, HD // 2).astype(aq_ref.dtype)
    for h in range(ATT_KV_HEADS):
        sl = slice(h * HD, (h + 1) * HD)
        k = _rot(head_norm(zk_ref[:, sl], kn_ref[...]), ca, sa, HD // 2)
        ak_ref[:, sl] = k
        akb_ref[:, sl] = k.astype(akb_ref.dtype)
    v = zv_ref[...]
    av_ref[...] = v
    avb_ref[...] = v.astype(avb_ref.dtype)
    for h in range(IDX_HEADS):
        sl = slice(h * HD, (h + 1) * HD)
        iq_ref[:, sl] = (_rot(ziq_ref[:, sl], ci, si, IDX_ROPE_DIM // 2) * (IDX_DIM ** -0.5)).astype(iq_ref.dtype)
    x = zik_ref[...]
    mu = jnp.mean(x, axis=-1, keepdims=True)
    var = jnp.mean(jnp.square(x - mu), axis=-1, keepdims=True)
    ik = _rot((x - mu) * lax.rsqrt(var + EPS) * ig_ref[...] + ib_ref[...], ci, si, IDX_ROPE_DIM // 2)
    ik_ref[...] = ik
    ikb_ref[...] = ik.astype(ikb_ref.dtype)
    iw_ref[...] = ziw_ref[...] * (IDX_HEADS ** -0.5)


def _prelude(z, m, l, tabs, q_norm, k_norm, idx_g, idx_b, t_len):
    _, dst, padded, _, _ = _segments()
    tr = min(m, 256)
    nt = t_len // tr if t_len >= tr else 1
    lowp = BF16 if tr >= 16 else F32
    aw, kvw, iw = padded['a_q'], padded['a_k'], padded['i_q']

    def zspec(name):
        w = padded[name]
        return pl.BlockSpec((tr, w), lambda i: (i, dst[name] // w))

    tab_spec = pl.BlockSpec((tr, LANE), lambda i: (i % nt, 0))
    vec_spec = pl.BlockSpec((None, 1, HD), lambda i: (l, 0, 0))
    row = lambda w: pl.BlockSpec((tr, w), lambda i: (i, 0))
    shp = lambda w, dt: jax.ShapeDtypeStruct((m, w), dt)
    vec = lambda a: a.reshape(a.shape[0], 1, HD)
    return pl.pallas_call(
        _prelude_kernel,
        grid=(m // tr,),
        in_specs=[zspec('a_q'), zspec('a_k'), zspec('a_v'), zspec('i_q'), zspec('i_k'), zspec('i_w'),
                  tab_spec, tab_spec, tab_spec, tab_spec, vec_spec, vec_spec, vec_spec, vec_spec],
        out_specs=[row(aw), row(kvw), row(kvw), row(kvw), row(kvw), row(iw), row(HD), row(HD), row(LANE)],
        out_shape=[shp(aw, lowp), shp(kvw, F32), shp(kvw, F32), shp(kvw, lowp), shp(kvw, lowp),
                   shp(iw, lowp), shp(HD, F32), shp(HD, lowp), shp(LANE, F32)],
        compiler_params=_params(("parallel",)),
        name="attn_prelude",
    )(z, z, z, z, z, z, *tabs, vec(q_norm), vec(k_norm), vec(idx_g), vec(idx_b))


def _ret_gamma_log(h):
    return math.log(1.0 - 2.0 ** (-5.0 - h))


def _retention_kernel(lg_ref, q_ref, k_ref, v_ref, g_ref, cos_ref, sin_ref, nrm_ref, o_ref, st_ref, s_scr, *, nc):
    c = pl.program_id(2)
    cr = q_ref.shape[0]
    lg = lg_ref[:, 0:1]

    @pl.when(c == 0)
    def _():
        s_scr[...] = jnp.zeros_like(s_scr)

    cos, sin = cos_ref[...], sin_ref[...]
    q = _rot(q_ref[...], cos, sin, RET_DK // 2)
    k = _rot(k_ref[...], cos, sin, RET_DK // 2) * (RET_DK ** -0.5)
    v = v_ref[...]
    ti = lax.broadcasted_iota(jnp.int32, (cr, cr), 0)
    si = lax.broadcasted_iota(jnp.int32, (cr, cr), 1)
    diff = (ti - si).astype(F32)
    intra = jnp.where(diff >= 0, jnp.exp(lg * jnp.maximum(diff, 0.0)), 0.0)
    tcol = lax.broadcasted_iota(jnp.int32, (cr, 1), 0).astype(F32)
    q_dec = jnp.exp(lg * (tcol + 1.0))
    k_dec = jnp.exp(lg * (cr - 1.0 - tcol))
    s_dec = jnp.exp(lg * cr)
    s = s_scr[...]
    a = _bdot(q, k, _NT) * intra
    o = _bdot(a, v) + _bdot(q * q_dec, s)
    s_new = s * s_dec + _bdot(k * k_dec, v, _TN)
    s_scr[...] = s_new
    y = o * lax.rsqrt(jnp.mean(o * o, axis=-1, keepdims=True) + EPS) * nrm_ref[...]
    o_ref[...] = (y * _silu(g_ref[...])).astype(o_ref.dtype)

    @pl.when(c == nc - 1)
    def _():
        st_ref[...] = s_new


def _retention(z, l, cos, sin, ret_norm, b, t):
    _, dst, _, _, _ = _segments()
    cr = min(t, 256)
    nc = t // cr
    lg = jnp.broadcast_to(
        jnp.asarray([_ret_gamma_log(h) for h in range(RET_HEADS)], F32)[:, None, None], (RET_HEADS, 1, LANE))

    def zspec(name):
        return pl.BlockSpec((cr, HD), lambda bi, h, c: (bi * nc + c, dst[name] // HD + h))

    tab = pl.BlockSpec((cr, LANE), lambda bi, h, c: (c, 0))
    return pl.pallas_call(
        functools.partial(_retention_kernel, nc=nc),
        grid=(b, RET_HEADS, nc),
        in_specs=[pl.BlockSpec((None, 1, LANE), lambda bi, h, c: (h, 0, 0)),
                  zspec('r_q'), zspec('r_k'), zspec('r_v'), zspec('r_g'), tab, tab,
                  pl.BlockSpec((None, 1, HD), lambda bi, h, c: (l, 0, h))],
        out_specs=[pl.BlockSpec((cr, HD), lambda bi, h, c: (bi * nc + c, h)),
                   pl.BlockSpec((None, None, RET_DK, RET_DV), lambda bi, h, c: (bi, h, 0, 0))],
        out_shape=[jax.ShapeDtypeStruct((z.shape[0], RET_HEADS * RET_DV), BF16),
                   jax.ShapeDtypeStruct((b, RET_HEADS, RET_DK, RET_DV), F32)],
        scratch_shapes=[pltpu.VMEM((RET_DK, RET_DV), F32)],
        compiler_params=_params(("parallel", "parallel", "arbitrary")),
        name="retention",
    )(lg, z, z, z, z, cos, sin, ret_norm.reshape(ret_norm.shape[0], 1, -1))


def _hgrn_gates(fa, lb):
    log_f = jnp.minimum(fa, 0.0) - jnp.log1p(jnp.exp(-jnp.abs(fa))) + jnp.log1p(lb * jnp.exp(-fa))
    hk = (1.0 - lb) * jax.nn.sigmoid(-fa)
    return log_f, hk


def _split3(x):
    hi = x.astype(BF16)
    r1 = x - hi.astype(F32)
    mid = r1.astype(BF16)
    lo = (r1 - mid.astype(F32)).astype(BF16)
    return hi, mid, lo


def _gla_chunk(q, k, v, g, st):
    c = GLA_CHUNK
    ti = lax.broadcasted_iota(jnp.int32, (c, c), 0)
    si = lax.broadcasted_iota(jnp.int32, (c, c), 1)
    tri = (ti >= si).astype(BF16)
    hi, mid, lo = _split3(g)
    b = (jnp.dot(tri, hi, preferred_element_type=F32) + jnp.dot(tri, mid, preferred_element_type=F32)
         + jnp.dot(tri, lo, preferred_element_type=F32))
    b_last = b[c - 1:c, :]
    o = _bdot(q * jnp.exp(b), st, _NT)
    kd = k * jnp.exp(b_last - b)
    st_new = st * jnp.exp(b_last) + _bdot(v, kd, _TN)

    nsub = c // GLA_SUB
    a_off = jnp.zeros((c, c), F32)
    for i in range(1, nsub):
        m = b[i * GLA_SUB - 1:i * GLA_SUB, :]
        qs = q * jnp.exp(jnp.minimum(b - m, 0.0))
        ks = k * jnp.exp(jnp.minimum(m - b, 0.0))
        blk = (ti // GLA_SUB == i) & (si < i * GLA_SUB)
        a_off = a_off + jnp.where(blk, _bdot(qs, ks, _NT), 0.0)
    o = o + _bdot(a_off, v)

    rows = lax.broadcasted_iota(jnp.int32, (GLA_SUB, 1), 0)
    diag = []
    for i in range(nsub):
        sl = slice(i * GLA_SUB, (i + 1) * GLA_SUB)
        qi, ki, vi, bi = q[sl], k[sl], v[sl], b[sl]
        oi = jnp.zeros((GLA_SUB, HD), F32)
        for s in range(GLA_SUB):
            d = jnp.exp(jnp.minimum(bi - bi[s:s + 1], 0.0)) * qi * ki[s:s + 1]
            w = jnp.where(rows >= s, jnp.sum(d, axis=-1, keepdims=True), 0.0)
            oi = oi + w * vi[s:s + 1]
        diag.append(oi)
    return o + jnp.concatenate(diag, axis=0), st_new


def _gla_kernel(f_ref, q_ref, i_ref, g_ref, lb_ref, nrm_ref, o_ref, st_ref, s_scr, *, nc, n_inner):
    c = pl.program_id(2)

    @pl.when(c == 0)
    def _():
        s_scr[...] = jnp.zeros_like(s_scr)

    lb = lb_ref[...]
    nrm = nrm_ref[...]

    def body(ci, carry):
        rows = pl.ds(pl.multiple_of(ci * GLA_CHUNK, GLA_CHUNK), GLA_CHUNK)
        log_f, hk = _hgrn_gates(f_ref[rows, :], lb)
        hq = _silu(q_ref[rows, :])
        o, st_new = _gla_chunk(hq, hk, i_ref[rows, :], log_f, s_scr[...])
        s_scr[...] = st_new
        y = o * lax.rsqrt(jnp.mean(o * o, axis=-1, keepdims=True) + EPS) * nrm
        o_ref[rows, :] = (y * jax.nn.sigmoid(g_ref[rows, :])).astype(o_ref.dtype)
        return carry

    lax.fori_loop(0, n_inner, body, 0)

    @pl.when(c == nc - 1)
    def _():
        st_ref[...] = s_scr[...].T


def _gla(z, l, lb, hg_norm, b, t):
    _, dst, _, _, _ = _segments()
    blk = min(t, 256)
    assert blk % GLA_CHUNK == 0
    nc = t // blk

    def zspec(name):
        return pl.BlockSpec((blk, HD), lambda bi, h, c: (bi * nc + c, dst[name] // HD + h))

    vec = lambda: pl.BlockSpec((None, 1, HD), lambda bi, h, c: (l, 0, h))
    return pl.pallas_call(
        functools.partial(_gla_kernel, nc=nc, n_inner=blk // GLA_CHUNK),
        grid=(b, HG_HEADS, nc),
        in_specs=[zspec('h_f'), zspec('h_q'), zspec('h_i'), zspec('h_g'), vec(), vec()],
        out_specs=[pl.BlockSpec((blk, HD), lambda bi, h, c: (bi * nc + c, h)),
                   pl.BlockSpec((None, None, HG_DK, HG_DV), lambda bi, h, c: (bi, h, 0, 0))],
        out_shape=[jax.ShapeDtypeStruct((z.shape[0], HG_HEADS * HG_DV), BF16),
                   jax.ShapeDtypeStruct((b, HG_HEADS, HG_DK, HG_DV), F32)],
        scratch_shapes=[pltpu.VMEM((HG_DV, HG_DK), F32)],
        compiler_params=_params(("parallel", "parallel", "arbitrary")),
        name="hgrn2",
    )(z, z, z, z, lb.reshape(lb.shape[0], 1, -1), hg_norm.reshape(hg_norm.shape[0], 1, -1))


def _to_col(row):
    n = row.shape[1]
    eye = lax.broadcasted_iota(jnp.int32, (n, n), 0) == lax.broadcasted_iota(jnp.int32, (n, n), 1)
    return jnp.sum(jnp.where(eye, row, 0.0), axis=1, keepdims=True)


def _decode_rec_kernel(rq_ref, rk_ref, rv_ref, rg_ref, hf_ref, hq_ref, hi_ref, hg_ref,
                       cos_ref, sin_ref, rn_ref, hn_ref, lb_ref, sr_ref, sh_ref,
                       or_ref, oh_ref, nr_ref, nh_ref):
    cos, sin = cos_ref[...], sin_ref[...]

    def rms(o, g):
        return o * lax.rsqrt(jnp.mean(o * o, axis=-1, keepdims=True) + EPS) * g

    for h in range(RET_HEADS):
        sl = slice(h * HD, (h + 1) * HD)
        gamma = math.exp(_ret_gamma_log(h))
        q = _rot(rq_ref[:, sl], cos, sin, RET_DK // 2)
        k = _rot(rk_ref[:, sl], cos, sin, RET_DK // 2) * (RET_DK ** -0.5)
        v = rv_ref[:, sl]
        s = sr_ref[h]
        o = jnp.sum(q * k, axis=-1, keepdims=True) * v + jnp.sum(_to_col(q * gamma) * s, axis=0, keepdims=True)
        nr_ref[h] = s * gamma + _to_col(k) * v
        or_ref[:, sl] = rms(o, rn_ref[:, sl]) * _silu(rg_ref[:, sl])

    for h in range(HG_HEADS):
        sl = slice(h * HD, (h + 1) * HD)
        log_f, k = _hgrn_gates(hf_ref[:, sl], lb_ref[:, sl])
        q = _silu(hq_ref[:, sl])
        v = hi_ref[:, sl]
        s = sh_ref[h]
        eb = jnp.exp(log_f)
        o = jnp.sum(q * k, axis=-1, keepdims=True) * v + jnp.sum(_to_col(q * eb) * s, axis=0, keepdims=True)
        nh_ref[h] = s * _to_col(eb) + _to_col(k) * v
        oh_ref[:, sl] = rms(o, hn_ref[:, sl]) * jax.nn.sigmoid(hg_ref[:, sl])


def _decode_rec(z, l, cos, sin, ret_norm, hg_norm, lb, state_ret, state_hgrn):
    _, dst, padded, _, _ = _segments()
    nb = z.shape[0]

    z = z.reshape(nb, 1, z.shape[1])

    def zspec(name):
        w = padded[name]
        return pl.BlockSpec((None, 1, w), lambda bi: (bi, 0, dst[name] // w))

    one = lambda w: pl.BlockSpec((1, w), lambda bi: (0, 0))
    vec = lambda a: pl.BlockSpec((None, 1, a.shape[-1]), lambda bi: (l, 0, 0))
    st = lambda a: pl.BlockSpec((None, None) + a.shape[2:], lambda bi: (l, bi, 0, 0, 0))
    st_out = lambda a: pl.BlockSpec((None,) + a.shape[2:], lambda bi: (bi, 0, 0, 0))
    rw, hw = RET_HEADS * RET_DV, HG_HEADS * HG_DV
    r3 = lambda a: a.reshape(a.shape[0], 1, -1)
    return pl.pallas_call(
        _decode_rec_kernel,
        grid=(nb,),
        in_specs=[zspec('r_q'), zspec('r_k'), zspec('r_v'), zspec('r_g'),
                  zspec('h_f'), zspec('h_q'), zspec('h_i'), zspec('h_g'),
                  one(LANE), one(LANE), vec(ret_norm), vec(hg_norm), vec(lb), st(state_ret), st(state_hgrn)],
        out_specs=[pl.BlockSpec((None, 1, rw), lambda bi: (bi, 0, 0)),
                   pl.BlockSpec((None, 1, hw), lambda bi: (bi, 0, 0)),
                   st_out(state_ret), st_out(state_hgrn)],
        out_shape=[jax.ShapeDtypeStruct((nb, 1, rw), F32), jax.ShapeDtypeStruct((nb, 1, hw), F32),
                   jax.ShapeDtypeStruct(state_ret.shape[1:], F32), jax.ShapeDtypeStruct(state_hgrn.shape[1:], F32)],
        compiler_params=_params(("arbitrary",)),
        name="decode_recurrent",
    )(z, z, z, z, z, z, z, z, cos, sin, r3(ret_norm), r3(hg_norm), r3(lb), state_ret, state_hgrn)


def _sort_key(score):
    bits = lax.bitcast_convert_type(score + 0.0, jnp.int32)
    return bits ^ ((bits >> 31) & jnp.int32(0x7FFFFFFF))


def _count(mask):
    return jnp.sum(mask.astype(F32), axis=-1, keepdims=True)


def _nth_largest_key(count_ge, n_sel, shape):
    lo = jnp.where(count_ge(jnp.zeros(shape, jnp.int32)) >= n_sel, 0, INT_MIN).astype(jnp.int32)

    def body(i, lo):
        cand = lo | jnp.left_shift(jnp.int32(1), 30 - i)
        return jnp.where(count_ge(cand) >= n_sel, cand, lo)

    return lax.fori_loop(0, 31, body, lo)


def _tie_bound(count_eq_below, need, nbits, shape):
    def body(i, j):
        cand = j | jnp.left_shift(jnp.int32(1), nbits - 1 - i)
        return jnp.where(count_eq_below(cand) < need, cand, j)

    return lax.fori_loop(0, nbits, body, jnp.zeros(shape, jnp.int32))


def _dsa_prompt_body(length, aq_ref, iq_ref, iw_ref, k_ref, v_ref, ik_ref, o_ref, key_scr, sel_scr, n_sel):
    j = pl.program_id(1)
    tq = aq_ref.shape[0]
    ik = ik_ref[0:length, :]
    score = jnp.zeros((tq, length), F32)
    for h in range(IDX_HEADS):
        s = lax.dot_general(iq_ref[:, h * HD:(h + 1) * HD], ik, _NT, preferred_element_type=F32)
        score = score + jnp.maximum(s, 0.0) * iw_ref[:, h:h + 1]
    q_pos = j * tq + lax.broadcasted_iota(jnp.int32, (tq, 1), 0)
    col = lax.broadcasted_iota(jnp.int32, (tq, length), 1)
    visible = col <= q_pos
    key_scr[:, 0:length] = _sort_key(jnp.where(visible, score, NEG_BIG))

    tau = _nth_largest_key(lambda c: _count(key_scr[:, 0:length] >= c), n_sel, (tq, 1))
    key = key_scr[:, 0:length]
    gt = key > tau
    eq = key == tau
    need = n_sel - _count(gt)
    spare = jnp.max(_count(eq & visible) - need) > 0.0
    bound = lax.cond(
        spare,
        lambda: _tie_bound(lambda c: _count((key_scr[:, 0:length] == tau) & (col < c)), need,
                           max(1, length.bit_length()), (tq, 1)),
        lambda: jnp.full((tq, 1), length, jnp.int32))
    sel_scr[:, 0:length] = ((gt | (eq & (col <= bound))) & visible).astype(F32)

    group = ATT_HEADS // ATT_KV_HEADS
    scale = ATT_HEAD_DIM ** -0.5
    for n in range(ATT_KV_HEADS):
        kn = k_ref[0:length, n * HD:(n + 1) * HD]
        vn = v_ref[0:length, n * HD:(n + 1) * HD]
        for g in range(group):
            sl = slice((n * group + g) * HD, (n * group + g + 1) * HD)
            s = lax.dot_general(aq_ref[:, sl], kn, _NT, preferred_element_type=F32) * scale
            s = jnp.where(sel_scr[:, 0:length] > 0.0, s, NEG_BIG)
            m = jnp.max(s, axis=-1, keepdims=True)
            p = jnp.exp(s - m)
            o = _bdot(p, vn) / jnp.sum(p, axis=-1, keepdims=True)
            o_ref[:, sl] = o.astype(o_ref.dtype)


def _dsa_prompt_kernel(*refs, n_sel, lengths):
    j = pl.program_id(1)
    tq = refs[0].shape[0]
    prev = 0
    for length in lengths:
        @pl.when((j >= prev // tq) & (j < length // tq))
        def _(length=length):
            _dsa_prompt_body(length, *refs, n_sel)
        prev = length


def _dsa_prompt(aq, iq, iw, kb, vb, ikb, b, t, mt):
    tq = min(Q_BLOCK, t)
    nq = t // tq
    n_sel = min(TOPK_MAX, t // 4)
    step = min(t, 512)
    lengths = tuple(range(step, t + 1, step))
    assert t % step == 0 and step % tq == 0 and step >= n_sel
    qrow = lambda w: pl.BlockSpec((tq, w), lambda bi, j: (bi * nq + j, 0))
    krow = lambda w: pl.BlockSpec((t, w), lambda bi, j: (bi, 0))
    return pl.pallas_call(
        functools.partial(_dsa_prompt_kernel, n_sel=n_sel, lengths=lengths),
        grid=(b, nq),
        in_specs=[qrow(aq.shape[1]), qrow(iq.shape[1]), qrow(LANE),
                  krow(kb.shape[1]), krow(vb.shape[1]), krow(HD)],
        out_specs=qrow(aq.shape[1]),
        out_shape=jax.ShapeDtypeStruct((mt, aq.shape[1]), BF16),
        scratch_shapes=[pltpu.VMEM((tq, t), jnp.int32), pltpu.VMEM((tq, t), F32)],
        compiler_params=_params(("parallel", "arbitrary")),
        name="dsa_prompt",
    )(aq, iq, iw, kb, vb, ikb)


def _dec_score_kernel(pt_ref, iq_ref, iw_ref, ikn_ref, *refs):
    pages = refs[:PAGES_PER_STEP]
    sc_ref, new_ref = refs[PAGES_PER_STEP:]
    iq = iq_ref[...].astype(BF16)
    iw = iw_ref[...]
    ik = jnp.concatenate([p[...].astype(BF16) for p in pages], axis=0)
    s = lax.dot_general(iq, ik, _NT, preferred_element_type=F32)
    sc_ref[...] = jnp.sum(jnp.maximum(s, 0.0) * iw, axis=0, keepdims=True)

    @pl.when(pl.program_id(1) == 0)
    def _():
        sn = lax.dot_general(iq, jnp.broadcast_to(ikn_ref[...], (8, HD)).astype(BF16), _NT,
                             preferred_element_type=F32)[:, 0:1]
        new_ref[...] = jnp.broadcast_to(jnp.sum(jnp.maximum(sn, 0.0) * iw, axis=0, keepdims=True), (1, LANE))


def _dec_scores(page_table, iq, iw, ik_new, cache_ik, l):
    nb, n_pages = page_table.shape
    steps = n_pages // PAGES_PER_STEP
    ih = iq.shape[1]
    page = lambda r: pl.BlockSpec((None, None, PAGE_SIZE, HD),
                                  lambda bi, p, pt: (l, pt[bi, p * PAGES_PER_STEP + r], 0, 0))
    grid_spec = pltpu.PrefetchScalarGridSpec(
        num_scalar_prefetch=1,
        grid=(nb, steps),
        in_specs=[pl.BlockSpec((None, ih, HD), lambda bi, p, pt: (bi, 0, 0)),
                  pl.BlockSpec((None, ih, 1), lambda bi, p, pt: (bi, 0, 0)),
                  pl.BlockSpec((None, 1, HD), lambda bi, p, pt: (bi, 0, 0))]
                 + [page(r) for r in range(PAGES_PER_STEP)],
        out_specs=[pl.BlockSpec((None, 1, PAGES_PER_STEP * PAGE_SIZE), lambda bi, p, pt: (bi, 0, p)),
                   pl.BlockSpec((None, 1, LANE), lambda bi, p, pt: (bi, 0, 0))],
    )
    return pl.pallas_call(
        _dec_score_kernel,
        grid_spec=grid_spec,
        out_shape=[jax.ShapeDtypeStruct((nb, 1, n_pages * PAGE_SIZE), F32),
                   jax.ShapeDtypeStruct((nb, 1, LANE), F32)],
        compiler_params=_params(("parallel", "arbitrary")),
        name="dec_scores",
    )(page_table, iq, iw, ik_new, *([cache_ik] * PAGES_PER_STEP))


def _dec_attn_kernel(pt_ref, sc_all_ref, sc_new_ref, sc_ref, q_ref, kn_ref, vn_ref, *refs,
                     n_sel, nbits, steps):
    kp = refs[:PAGES_PER_STEP]
    vp = refs[PAGES_PER_STEP:2 * PAGES_PER_STEP]
    o_ref, tau_scr, bnd_scr, m_scr, l_scr, acc_scr = refs[2 * PAGES_PER_STEP:]
    p_id = pl.program_id(1)
    chunk = PAGES_PER_STEP * PAGE_SIZE
    scale = ATT_HEAD_DIM ** -0.5
    lane0 = lax.broadcasted_iota(jnp.int32, (1, LANE), 1) == 0

    @pl.when(p_id == 0)
    def _():
        key = _sort_key(sc_all_ref[...])
        key_new = _sort_key(sc_new_ref[...])
        idx = lax.broadcasted_iota(jnp.int32, key.shape, 1)
        cnt = lambda mp, mn: _count(mp) + _count(mn & lane0)
        tau = _nth_largest_key(lambda c: cnt(key >= c, key_new >= c), n_sel, (1, 1))
        need = n_sel - cnt(key > tau, key_new > tau)
        bound = _tie_bound(lambda c: _count((key == tau) & (idx < c)), need, nbits, (1, 1))
        tau_scr[...] = jnp.broadcast_to(tau, tau_scr.shape)
        bnd_scr[...] = jnp.broadcast_to(bound, bnd_scr.shape)
        m_scr[...] = jnp.full_like(m_scr, NEG_BIG)
        l_scr[...] = jnp.zeros_like(l_scr)
        acc_scr[...] = jnp.zeros_like(acc_scr)

    tau = tau_scr[:, 0:1]
    bound = bnd_scr[:, 0:1]
    key_c = _sort_key(sc_ref[...])
    idx_c = p_id * chunk + lax.broadcasted_iota(jnp.int32, key_c.shape, 1)
    sel = (key_c > tau) | ((key_c == tau) & (idx_c <= bound))
    q = q_ref[...].astype(BF16)
    heads = lambda pages, n: jnp.concatenate(
        [r[pl.ds(n, PAGE_SIZE, stride=ATT_KV_HEADS), :].astype(BF16) for r in pages], axis=0)
    s = sum(lax.dot_general(q[:, n * HD:(n + 1) * HD], heads(kp, n), _NT, preferred_element_type=F32)
            for n in range(ATT_KV_HEADS)) * scale
    s = jnp.where(sel, s, NEG_BIG)
    m_old = m_scr[:, 0:1]
    m_new = jnp.maximum(m_old, jnp.max(s, axis=-1, keepdims=True))
    alpha = jnp.exp(m_old - m_new)
    p = jnp.where(sel, jnp.exp(s - m_new), 0.0)
    l_new = alpha * l_scr[:, 0:1] + jnp.sum(p, axis=-1, keepdims=True)
    pb = p.astype(BF16)
    acc_new = alpha * acc_scr[...] + jnp.concatenate(
        [jnp.dot(pb, heads(vp, n), preferred_element_type=F32) for n in range(ATT_KV_HEADS)], axis=1)
    m_scr[...] = jnp.broadcast_to(m_new, m_scr.shape)
    l_scr[...] = jnp.broadcast_to(l_new, l_scr.shape)
    acc_scr[...] = acc_new

    @pl.when(p_id == steps - 1)
    def _():
        key_all = _sort_key(sc_all_ref[...])
        key_new = _sort_key(sc_new_ref[...])[:, 0:1]
        need = n_sel - _count(key_all > tau) - (key_new > tau).astype(F32)
        n_eq = _count(key_all == tau)
        sel_new = (key_new > tau) | ((key_new == tau) & (n_eq < need))
        qf = q_ref[...]
        s_new = jnp.sum(qf * kn_ref[...], axis=-1, keepdims=True) * scale
        s_new = jnp.where(sel_new, s_new, NEG_BIG)
        m_fin = jnp.maximum(m_new, s_new)
        a2 = jnp.exp(m_new - m_fin)
        p_new = jnp.where(sel_new, jnp.exp(s_new - m_fin), 0.0)
        l_fin = a2 * l_new + p_new
        acc_fin = (a2 * acc_new + p_new * vn_ref[...]) / l_fin
        group = ATT_HEADS // ATT_KV_HEADS
        for h in range(ATT_HEADS):
            n = h // group
            o_ref[:, h * HD:(h + 1) * HD] = acc_fin[h:h + 1, n * HD:(n + 1) * HD]


def _dec_attn(page_table, scores, score_new, q_bd, k_new, v_new, cache_k, cache_v, l):
    nb, n_pages = page_table.shape
    steps = n_pages // PAGES_PER_STEP
    s_len = n_pages * PAGE_SIZE
    chunk = PAGES_PER_STEP * PAGE_SIZE
    n_sel = min(TOPK_MAX, (s_len + 1) // 4)
    nbits = s_len.bit_length()
    kvw = ATT_KV_HEADS * HD
    hp = q_bd.shape[1]
    page = lambda r: pl.BlockSpec((None, None, PAGE_SIZE * ATT_KV_HEADS, HD),
                                  lambda bi, p, pt: (l, pt[bi, p * PAGES_PER_STEP + r], 0, 0))
    grid_spec = pltpu.PrefetchScalarGridSpec(
        num_scalar_prefetch=1,
        grid=(nb, steps),
        in_specs=[pl.BlockSpec((None, 1, s_len), lambda bi, p, pt: (bi, 0, 0)),
                  pl.BlockSpec((None, 1, LANE), lambda bi, p, pt: (bi, 0, 0)),
                  pl.BlockSpec((None, 1, chunk), lambda bi, p, pt: (bi, 0, p)),
                  pl.BlockSpec((None, hp, kvw), lambda bi, p, pt: (bi, 0, 0)),
                  pl.BlockSpec((None, 1, kvw), lambda bi, p, pt: (bi, 0, 0)),
                  pl.BlockSpec((None, 1, kvw), lambda bi, p, pt: (bi, 0, 0))]
                 + [page(r) for r in range(PAGES_PER_STEP)] * 2,
        out_specs=pl.BlockSpec((None, 1, ATT_HEADS * HD), lambda bi, p, pt: (bi, 0, 0)),
        scratch_shapes=[pltpu.VMEM((1, LANE), jnp.int32), pltpu.VMEM((1, LANE), jnp.int32),
                        pltpu.VMEM((hp, LANE), F32), pltpu.VMEM((hp, LANE), F32),
                        pltpu.VMEM((hp, kvw), F32)],
    )
    return pl.pallas_call(
        functools.partial(_dec_attn_kernel, n_sel=n_sel, nbits=nbits, steps=steps),
        grid_spec=grid_spec,
        out_shape=jax.ShapeDtypeStruct((nb, 1, ATT_HEADS * HD), F32),
        compiler_params=_params(("parallel", "arbitrary")),
        name="dec_attn",
    )(page_table, scores, score_new, scores, q_bd, k_new, v_new,
      *([cache_k] * PAGES_PER_STEP), *([cache_v] * PAGES_PER_STEP)).reshape(nb, ATT_HEADS * HD)


def _ffn(x, norm, w1, w3, w2, l, tm):
    h = _rmsnorm(x, norm, l, BF16)
    g = _matmul(h, [w1, w3], l, tm=tm, epilogue='swiglu', out_dtype=BF16)
    return _matmul(g, [w2], l, tm=tm, epilogue='resid', res=x, scale=0.5)


def _token_tiles(n_tokens):
    n_tiles = max(1, n_tokens // 1024)
    tm = -(-n_tokens // (16 * n_tiles)) * 16
    return tm, n_tiles


def kernel(x_prompt, x_sample, state_ret, state_hgrn, cache_k, cache_v, cache_idx_k, page_table, ffn1_norm, ffn1_w1, ffn1_w3, ffn1_w2, mix_norm, w_in, ret_norm, q_norm, k_norm, idx_k_g, idx_k_b, hg_lb_raw, hg_norm, w_up_ret, w_up_att, w_up_hg, w_out, ffn2_norm, ffn2_w1, ffn2_w3, ffn2_w2):
    b, t, d = x_prompt.shape
    nb = x_sample.shape[0]
    depth = w_in.shape[0]
    kvw = ATT_KV_HEADS * ATT_HEAD_DIM
    group = ATT_HEADS // ATT_KV_HEADS

    lb_soft = jax.nn.softmax(hg_lb_raw.astype(F32), axis=0)
    lb_all = jnp.cumsum(lb_soft, axis=0) - lb_soft[0]
    w_in_p = _repack_w_in(w_in)
    cast = lambda w: w.astype(BF16)
    ffn1_w1, ffn1_w3, ffn1_w2 = cast(ffn1_w1), cast(ffn1_w3), cast(ffn1_w2)
    ffn2_w1, ffn2_w3, ffn2_w2 = cast(ffn2_w1), cast(ffn2_w3), cast(ffn2_w2)
    w_up_ret, w_up_att, w_up_hg, w_out = cast(w_up_ret), cast(w_up_att), cast(w_up_hg), cast(w_out)

    pos_p = jnp.arange(t, dtype=jnp.int32)
    pos_s = jnp.full((nb,), PAST_LEN, jnp.int32)
    ret_f = 1.0 / (ROPE_THETA ** jnp.linspace(0.0, 1.0, RET_DK // 2, dtype=F32))
    att_f = ROPE_THETA ** (-jnp.arange(0, ATT_HEAD_DIM, 2, dtype=F32) / ATT_HEAD_DIM)
    idx_f = ROPE_THETA ** (-jnp.arange(0, IDX_ROPE_DIM, 2, dtype=F32) / IDX_ROPE_DIM)
    tabs_p = _rope_tables(pos_p, att_f, ATT_HEAD_DIM) + _rope_tables(pos_p, idx_f, IDX_ROPE_DIM)
    tabs_s = _rope_tables(pos_s, att_f, ATT_HEAD_DIM) + _rope_tables(pos_s, idx_f, IDX_ROPE_DIM)
    ret_tab_p = _rope_tables(pos_p, ret_f, RET_DK)
    ret_tab_s = _rope_tables(pos_s[:1], ret_f, RET_DK)

    ck = cache_k.reshape(cache_k.shape[:2] + (PAGE_SIZE * ATT_KV_HEADS, HD))
    cv = cache_v.reshape(cache_v.shape[:2] + (PAGE_SIZE * ATT_KV_HEADS, HD))

    mp = b * t
    tm, n_tiles = _token_tiles(mp + nb)
    mt = tm * n_tiles
    x = jnp.concatenate([x_prompt.reshape(mp, d), x_sample.reshape(nb, d), jnp.zeros((mt - mp - nb, d), F32)])

    def with_tail(a, rows):
        tail = jnp.concatenate([rows.astype(a.dtype), jnp.zeros((mt - mp - nb, a.shape[1]), a.dtype)])
        return lax.dynamic_update_slice(a, tail, (mp, 0))

    outs = {n: [] for n in ('rp', 'rs', 'hp', 'hs', 'kp', 'vp', 'ip', 'ks', 'vs', 'is')}
    for l in range(depth):
        x = _ffn(x, ffn1_norm, ffn1_w1, ffn1_w3, ffn1_w2, l, tm)
        h = _rmsnorm(x, mix_norm, l, BF16)
        z = _matmul(h, [w_in_p], l, tm=tm)

        aq, ak, av, akb, avb, iq, ik, ikb, iw = _prelude(z, mp, l, tabs_p, q_norm, k_norm, idx_k_g, idx_k_b, t)
        a_ret, r_p = _retention(z, l, ret_tab_p[0], ret_tab_p[1], ret_norm, b, t)
        a_hg, h_p = _gla(z, l, lb_all, hg_norm, b, t)
        a_att = _dsa_prompt(aq, iq, iw, akb, avb, ikb, b, t, mt)
        outs['rp'].append(r_p); outs['hp'].append(h_p)
        outs['kp'].append(ak.reshape(b, t, ATT_KV_HEADS, ATT_HEAD_DIM))
        outs['vp'].append(av.reshape(b, t, ATT_KV_HEADS, ATT_HEAD_DIM))
        outs['ip'].append(ik.reshape(b, t, IDX_DIM))

        zs = z[mp:mp + nb]
        aq, ak, av, _, _, iq, ik, _, iw = _prelude(zs, nb, l, tabs_s, q_norm, k_norm, idx_k_g, idx_k_b, 1)
        s_ret, s_hg, r_s, h_s = _decode_rec(zs, l, ret_tab_s[0], ret_tab_s[1], ret_norm, hg_norm, lb_all,
                                            state_ret, state_hgrn)
        scores, score_new = _dec_scores(page_table, iq.reshape(nb, IDX_HEADS, IDX_DIM),
                                        iw[:, :IDX_HEADS].reshape(nb, IDX_HEADS, 1),
                                        ik.reshape(nb, 1, IDX_DIM), cache_idx_k, l)
        qh = aq.reshape(nb, ATT_KV_HEADS, group, ATT_HEAD_DIM)
        q_bd = jnp.concatenate(
            [jnp.concatenate([qh[:, n] if m == n else jnp.zeros_like(qh[:, n]) for m in range(ATT_KV_HEADS)], axis=-1)
             for n in range(ATT_KV_HEADS)]
            + [jnp.zeros((nb, max(0, 16 - ATT_HEADS), kvw), aq.dtype)], axis=1)
        s_att = _dec_attn(page_table, scores, score_new, q_bd, ak.reshape(nb, 1, kvw), av.reshape(nb, 1, kvw),
                          ck, cv, l)
        outs['rs'].append(r_s); outs['hs'].append(h_s)
        outs['ks'].append(ak.reshape(nb, 1, ATT_KV_HEADS, ATT_HEAD_DIM))
        outs['vs'].append(av.reshape(nb, 1, ATT_KV_HEADS, ATT_HEAD_DIM))
        outs['is'].append(ik.reshape(nb, 1, IDX_DIM))

        merged = _merge(with_tail(a_ret, s_ret.reshape(nb, -1)), with_tail(a_att, s_att),
                        with_tail(a_hg, s_hg.reshape(nb, -1)), w_up_ret, w_up_att, w_up_hg, z, l, BF16, tm)
        x = _matmul(merged, [w_out], l, tm=tm, epilogue='resid', res=x, scale=1.0)
        x = _ffn(x, ffn2_norm, ffn2_w1, ffn2_w3, ffn2_w2, l, tm)

    st = lambda n: jnp.stack(outs[n])
    return (x[:mp].reshape(b, t, d), x[mp:mp + nb].reshape(nb, 1, d),
            st('rp').astype(state_ret.dtype), st('rs').astype(state_ret.dtype),
            st('hp').astype(state_hgrn.dtype), st('hs').astype(state_hgrn.dtype),
            st('kp'), st('vp'), st('ip'), st('ks'), st('vs'), st('is'))
```

```python
import functools
import math

import numpy as np
import jax
import jax.numpy as jnp
from jax import lax
from jax.experimental import pallas as pl
from jax.experimental.pallas import tpu as pltpu

D_MODEL = 4096
BATCH = 4
SEQ = 2048
DEPTH = 2
DEC_BATCH = 8
DEC_SEQ = 1
PAST_LEN = 16384
PAGE_SIZE = 128

RET_HEADS = 8
RET_DK = 128
RET_DV = 128
ATT_HEADS = 8
ATT_KV_HEADS = 2
ATT_HEAD_DIM = 128
IDX_HEADS = 16
IDX_DIM = 128
IDX_ROPE_DIM = 64
TOPK_MAX = 256
Q_BLOCK = 128
HG_HEADS = 8
HG_DK = 128
HG_DV = 128
D_FF = 11008
ROPE_THETA = 10000.0
EPS = 1e-6
NEG_BIG = -1e30

F32 = jnp.float32
BF16 = jnp.bfloat16
LANE = 128
HD = 128
INT_MIN = -(2 ** 31)
VMEM_LIMIT = 56 * 1024 * 1024
GLA_CHUNK = 64
GLA_SUB = 16
GLA_SAFE_SPAN = 60.0
PAGES_PER_STEP = 8

_NT = (((1,), (1,)), ((), ()))
_TN = (((0,), (0,)), ((), ()))


def _params(sem):
    return pltpu.CompilerParams(dimension_semantics=sem, vmem_limit_bytes=VMEM_LIMIT)


def _bdot(a, b, dims=None):
    a = a.astype(BF16)
    b = b.astype(BF16)
    if dims is None:
        return jnp.dot(a, b, preferred_element_type=F32)
    return lax.dot_general(a, b, dims, preferred_element_type=F32)


def _silu(x):
    return x * jax.nn.sigmoid(x)


def _segments():
    ret_qk = RET_HEADS * RET_DK
    ret_w = RET_HEADS * RET_DV
    att_w = ATT_HEADS * ATT_HEAD_DIM
    kv_w = ATT_KV_HEADS * ATT_HEAD_DIM
    hg_k = HG_HEADS * HG_DK
    hg_w = HG_HEADS * HG_DV
    names = ['r_q', 'r_k', 'r_v', 'r_g', 'a_q', 'a_k', 'a_v', 'i_q', 'i_k', 'i_w',
             'h_f', 'h_q', 'h_i', 'h_g', 'g_ret', 'g_att', 'g_hg']
    widths = [ret_qk, ret_qk, ret_w, ret_w, att_w, kv_w, kv_w, IDX_HEADS * IDX_DIM, IDX_DIM, IDX_HEADS,
              hg_k, hg_k, hg_w, hg_w, D_MODEL, D_MODEL, D_MODEL]
    src = {}
    off = 0
    for n, w in zip(names, widths):
        src[n] = (off, w)
        off += w
    padded = {n: -(-w // LANE) * LANE for n, w in zip(names, widths)}
    order = ['g_ret', 'g_att', 'g_hg'] + sorted(
        [n for n in names if not n.startswith('g_')], key=lambda n: -padded[n])
    dst = {}
    off = 0
    for n in order:
        dst[n] = off
        off += padded[n]
    total = -(-off // 256) * 256
    for n in order:
        assert dst[n] % padded[n] == 0 or n.startswith('g_'), (n, dst[n], padded[n])
    return src, dst, padded, order, total


def _repack_kernel(start_ref, valid_ref, a_ref, o_ref):
    c = pl.program_id(1)
    a = a_ref[0]
    row = lax.broadcasted_iota(jnp.int32, a.shape, 0)
    o_ref[...] = jnp.where(row < valid_ref[c], a, 0.0).T.astype(o_ref.dtype)


def _repack_w_in(w_in):
    src, dst, padded, order, total = _segments()
    depth, kdim, n_in = w_in.shape
    start = np.zeros((total // LANE,), np.int32)
    valid = np.zeros((total // LANE,), np.int32)
    for n in order:
        s, w = src[n]
        for t in range(padded[n] // LANE):
            c = dst[n] // LANE + t
            start[c], valid[c] = s + t * LANE, min(LANE, w - t * LANE)
    assert (start + LANE <= n_in).all() and (start % 8 == 0).all()
    start //= 8
    grid_spec = pltpu.PrefetchScalarGridSpec(
        num_scalar_prefetch=2,
        grid=(depth, total // LANE),
        in_specs=[pl.BlockSpec((pl.Element(1), pl.Element(LANE), pl.Element(kdim)),
                               lambda l, c, start, valid: (l, start[c] * 8, 0))],
        out_specs=pl.BlockSpec((None, kdim, LANE), lambda l, c, start, valid: (l, 0, c)),
    )
    return pl.pallas_call(
        _repack_kernel,
        grid_spec=grid_spec,
        out_shape=jax.ShapeDtypeStruct((depth, kdim, total), BF16),
        compiler_params=_params(("parallel", "parallel")),
        name="repack_w_in",
    )(jnp.asarray(start), jnp.asarray(valid), jnp.swapaxes(w_in, 1, 2))


def _rmsnorm_kernel(x_ref, g_ref, o_ref):
    x = x_ref[...]
    ms = jnp.mean(x * x, axis=-1, keepdims=True)
    o_ref[...] = (x * lax.rsqrt(ms + EPS) * g_ref[...]).astype(o_ref.dtype)


def _row_tile(m, cap):
    best = None
    for t in range(16, min(m, cap) + 1, 16):
        if m % t == 0:
            best = t
    return best if best is not None else m


def _rmsnorm(x, g_all, l, out_dtype):
    m, d = x.shape
    tr = _row_tile(m, 512)
    return pl.pallas_call(
        _rmsnorm_kernel,
        grid=(m // tr,),
        in_specs=[pl.BlockSpec((tr, d), lambda i: (i, 0)),
                  pl.BlockSpec((None, 1, d), lambda i: (l, 0, 0))],
        out_specs=pl.BlockSpec((tr, d), lambda i: (i, 0)),
        out_shape=jax.ShapeDtypeStruct((m, d), out_dtype),
        compiler_params=_params(("parallel",)),
        name="rmsnorm",
    )(x, g_all.reshape(g_all.shape[0], 1, d))


def _mm_kernel(*refs, n_w, nk, epilogue, scale):
    lhs_ref = refs[0]
    w_refs = refs[1:1 + n_w]
    pos = 1 + n_w
    res_ref = None
    if epilogue == 'resid':
        res_ref = refs[pos]
        pos += 1
    out_ref = refs[pos]
    acc_refs = refs[pos + 1:]

    lhs = lhs_ref[...].astype(BF16)
    parts = [jnp.dot(lhs, w[...].astype(BF16), preferred_element_type=F32) for w in w_refs]

    def finish(vals):
        if epilogue == 'swiglu':
            out = _silu(vals[0]) * vals[1]
        elif epilogue == 'resid':
            out = res_ref[...] + scale * vals[0]
        else:
            out = vals[0]
        out_ref[...] = out.astype(out_ref.dtype)

    if nk == 1:
        finish(parts)
        return

    k = pl.program_id(2)

    @pl.when(k == 0)
    def _():
        for a, p in zip(acc_refs, parts):
            a[...] = p

    @pl.when(k > 0)
    def _():
        for a, p in zip(acc_refs, parts):
            a[...] += p

    @pl.when(k == nk - 1)
    def _():
        finish([a[...] for a in acc_refs])


def _pick(n, cands):
    for c in cands:
        if n % c == 0:
            return c
    return n


def _matmul(lhs, ws, l, *, tm, epilogue='plain', res=None, scale=1.0, out_dtype=F32, tn=256, tk=None):
    m, kdim = lhs.shape
    n = ws[0].shape[-1]
    tn = _pick(n, (tn, 256, 128))
    if tk is None:
        tk = kdim
    assert m % tm == 0 and n % tn == 0 and kdim % tk == 0
    nk = kdim // tk
    lhs_mode = dict(pipeline_mode=pl.Buffered(1)) if nk == 1 and tm * kdim * lhs.dtype.itemsize > (12 << 20) else {}
    in_specs = [pl.BlockSpec((tm, tk), lambda i, j, k: (i, k), **lhs_mode)]
    in_specs += [pl.BlockSpec((None, tk, tn), lambda i, j, k: (l, k, j)) for _ in ws]
    args = [lhs] + list(ws)
    if epilogue == 'resid':
        in_specs.append(pl.BlockSpec((tm, tn), lambda i, j, k: (i, j)))
        args.append(res)
    scratch = [pltpu.VMEM((tm, tn), F32) for _ in ws] if nk > 1 else []
    return pl.pallas_call(
        functools.partial(_mm_kernel, n_w=len(ws), nk=nk, epilogue=epilogue, scale=scale),
        grid=(m // tm, n // tn, nk),
        in_specs=in_specs,
        out_specs=pl.BlockSpec((tm, tn), lambda i, j, k: (i, j)),
        out_shape=jax.ShapeDtypeStruct((m, n), out_dtype),
        scratch_shapes=scratch,
        compiler_params=_params(("parallel", "parallel", "arbitrary")),
        name="mm_" + epilogue,
    )(*args)


def _merge_kernel(ar_ref, aa_ref, ah_ref, wr_ref, wa_ref, wh_ref, gr_ref, ga_ref, gh_ref, o_ref):
    u_r = _bdot(ar_ref[...], wr_ref[...])
    u_a = _bdot(aa_ref[...], wa_ref[...])
    u_h = _bdot(ah_ref[...], wh_ref[...])
    out = (jax.nn.sigmoid(gr_ref[...]) * u_r + jax.nn.sigmoid(ga_ref[...]) * u_a
           + jax.nn.sigmoid(gh_ref[...]) * u_h)
    o_ref[...] = out.astype(o_ref.dtype)


def _merge(a_ret, a_att, a_hg, w_r, w_a, w_h, z, l, out_dtype, tm):
    m = a_ret.shape[0]
    d = w_r.shape[-1]
    tn = _pick(d, (256, 128))
    nb = d // tn
    lhs_spec = lambda a: pl.BlockSpec((tm, a.shape[1]), lambda i, j: (i, 0))
    w_spec = lambda w: pl.BlockSpec((None, w.shape[1], tn), lambda i, j: (l, 0, j))
    gate_spec = lambda g: pl.BlockSpec((tm, tn), lambda i, j: (i, g * nb + j))
    return pl.pallas_call(
        _merge_kernel,
        grid=(m // tm, nb),
        in_specs=[lhs_spec(a_ret), lhs_spec(a_att), lhs_spec(a_hg), w_spec(w_r), w_spec(w_a), w_spec(w_h),
                  gate_spec(0), gate_spec(1), gate_spec(2)],
        out_specs=pl.BlockSpec((tm, tn), lambda i, j: (i, j)),
        out_shape=jax.ShapeDtypeStruct((m, d), out_dtype),
        compiler_params=_params(("parallel", "parallel")),
        name="merge",
    )(a_ret, a_att, a_hg, w_r, w_a, w_h, z, z, z)


def _rot(x, cos, sin, half):
    up = pltpu.roll(x, LANE - half, 1)
    dn = pltpu.roll(x, half, 1)
    lane = lax.broadcasted_iota(jnp.int32, x.shape, 1)
    return x * cos + jnp.where(lane < half, up, dn) * sin


def _rope_tables(pos, freqs, width):
    ang = pos.astype(F32)[:, None] * freqs[None, :]
    cos, sin = jnp.cos(ang), jnp.sin(ang)
    t = pos.shape[0]
    pad_c = jnp.ones((t, LANE - width), F32)
    pad_s = jnp.zeros((t, LANE - width), F32)
    return (jnp.concatenate([cos, cos, pad_c], axis=1), jnp.concatenate([-sin, sin, pad_s], axis=1))


def _prelude_kernel(zq_ref, zk_ref, zv_ref, ziq_ref, zik_ref, ziw_ref, ca_ref, sa_ref, ci_ref, si_ref,
                    qn_ref, kn_ref, ig_ref, ib_ref,
                    aq_ref, ak_ref, av_ref, akb_ref, avb_ref, iq_ref, ik_ref, ikb_ref, iw_ref):
    ca, sa, ci, si = ca_ref[...], sa_ref[...], ci_ref[...], si_ref[...]

    def head_norm(x, g):
        return x * lax.rsqrt(jnp.mean(x * x, axis=-1, keepdims=True) + EPS) * g

    for h in range(ATT_HEADS):
        sl = slice(h * HD, (h + 1) * HD)
        aq_ref[:, sl] = _rot(head_norm(zq_ref[:, sl], qn_ref[...]), ca, sa, HD // 2).astype(aq_ref.dtype)
    for h in range(ATT_KV_HEADS):
        sl = slice(h * HD, (h + 1) * HD)
        k = _rot(head_norm(zk_ref[:, sl], kn_ref[...]), ca, sa, HD // 2)
        ak_ref[:, sl] = k
        akb_ref[:, sl] = k.astype(akb_ref.dtype)
    v = zv_ref[...]
    av_ref[...] = v
    avb_ref[...] = v.astype(avb_ref.dtype)
    for h in range(IDX_HEADS):
        sl = slice(h * HD, (h + 1) * HD)
        iq_ref[:, sl] = (_rot(ziq_ref[:, sl], ci, si, IDX_ROPE_DIM // 2) * (IDX_DIM ** -0.5)).astype(iq_ref.dtype)
    x = zik_ref[...]
    mu = jnp.mean(x, axis=-1, keepdims=True)
    var = jnp.mean(jnp.square(x - mu), axis=-1, keepdims=True)
    ik = _rot((x - mu) * lax.rsqrt(var + EPS) * ig_ref[...] + ib_ref[...], ci, si, IDX_ROPE_DIM // 2)
    ik_ref[...] = ik
    ikb_ref[...] = ik.astype(ikb_ref.dtype)
    iw_ref[...] = ziw_ref[...] * (IDX_HEADS ** -0.5)


def _prelude(z, m, l, tabs, q_norm, k_norm, idx_g, idx_b, t_len):
    _, dst, padded, _, _ = _segments()
    tr = min(m, 256)
    nt = t_len // tr if t_len >= tr else 1
    lowp = BF16 if tr >= 16 else F32
    aw, kvw, iw = padded['a_q'], padded['a_k'], padded['i_q']

    def zspec(name):
        w = padded[name]
        return pl.BlockSpec((tr, w), lambda i: (i, dst[name] // w))

    tab_spec = pl.BlockSpec((tr, LANE), lambda i: (i % nt, 0))
    vec_spec = pl.BlockSpec((None, 1, HD), lambda i: (l, 0, 0))
    row = lambda w: pl.BlockSpec((tr, w), lambda i: (i, 0))
    shp = lambda w, dt: jax.ShapeDtypeStruct((m, w), dt)
    vec = lambda a: a.reshape(a.shape[0], 1, HD)
    return pl.pallas_call(
        _prelude_kernel,
        grid=(m // tr,),
        in_specs=[zspec('a_q'), zspec('a_k'), zspec('a_v'), zspec('i_q'), zspec('i_k'), zspec('i_w'),
                  tab_spec, tab_spec, tab_spec, tab_spec, vec_spec, vec_spec, vec_spec, vec_spec],
        out_specs=[row(aw), row(kvw), row(kvw), row(kvw), row(kvw), row(iw), row(HD), row(HD), row(LANE)],
        out_shape=[shp(aw, lowp), shp(kvw, F32), shp(kvw, F32), shp(kvw, lowp), shp(kvw, lowp),
                   shp(iw, lowp), shp(HD, F32), shp(HD, lowp), shp(LANE, F32)],
        compiler_params=_params(("parallel",)),
        name="attn_prelude",
    )(z, z, z, z, z, z, *tabs, vec(q_norm), vec(k_norm), vec(idx_g), vec(idx_b))


def _ret_gamma_log(h):
    return math.log(1.0 - 2.0 ** (-5.0 - h))


def _retention_kernel(lg_ref, q_ref, k_ref, v_ref, g_ref, cos_ref, sin_ref, nrm_ref, o_ref, st_ref, s_scr, *, nc):
    c = pl.program_id(2)
    cr = q_ref.shape[0]
    lg = lg_ref[:, 0:1]

    @pl.when(c == 0)
    def _():
        s_scr[...] = jnp.zeros_like(s_scr)

    cos, sin = cos_ref[...], sin_ref[...]
    q = _rot(q_ref[...], cos, sin, RET_DK // 2)
    k = _rot(k_ref[...], cos, sin, RET_DK // 2) * (RET_DK ** -0.5)
    v = v_ref[...]
    ti = lax.broadcasted_iota(jnp.int32, (cr, cr), 0)
    si = lax.broadcasted_iota(jnp.int32, (cr, cr), 1)
    diff = (ti - si).astype(F32)
    intra = jnp.where(diff >= 0, jnp.exp(lg * jnp.maximum(diff, 0.0)), 0.0)
    tcol = lax.broadcasted_iota(jnp.int32, (cr, 1), 0).astype(F32)
    q_dec = jnp.exp(lg * (tcol + 1.0))
    k_dec = jnp.exp(lg * (cr - 1.0 - tcol))
    s_dec = jnp.exp(lg * cr)
    s = s_scr[...]
    a = _bdot(q, k, _NT) * intra
    o = _bdot(a, v) + _bdot(q * q_dec, s)
    s_new = s * s_dec + _bdot(k * k_dec, v, _TN)
    s_scr[...] = s_new
    y = o * lax.rsqrt(jnp.mean(o * o, axis=-1, keepdims=True) + EPS) * nrm_ref[...]
    o_ref[...] = (y * _silu(g_ref[...])).astype(o_ref.dtype)

    @pl.when(c == nc - 1)
    def _():
        st_ref[...] = s_new


def _retention(z, l, cos, sin, ret_norm, b, t):
    _, dst, _, _, _ = _segments()
    cr = min(t, 256)
    nc = t // cr
    lg = jnp.broadcast_to(
        jnp.asarray([_ret_gamma_log(h) for h in range(RET_HEADS)], F32)[:, None, None], (RET_HEADS, 1, LANE))

    def zspec(name):
        return pl.BlockSpec((cr, HD), lambda bi, h, c: (bi * nc + c, dst[name] // HD + h))

    tab = pl.BlockSpec((cr, LANE), lambda bi, h, c: (c, 0))
    return pl.pallas_call(
        functools.partial(_retention_kernel, nc=nc),
        grid=(b, RET_HEADS, nc),
        in_specs=[pl.BlockSpec((None, 1, LANE), lambda bi, h, c: (h, 0, 0)),
                  zspec('r_q'), zspec('r_k'), zspec('r_v'), zspec('r_g'), tab, tab,
                  pl.BlockSpec((None, 1, HD), lambda bi, h, c: (l, 0, h))],
        out_specs=[pl.BlockSpec((cr, HD), lambda bi, h, c: (bi * nc + c, h)),
                   pl.BlockSpec((None, None, RET_DK, RET_DV), lambda bi, h, c: (bi, h, 0, 0))],
        out_shape=[jax.ShapeDtypeStruct((z.shape[0], RET_HEADS * RET_DV), BF16),
                   jax.ShapeDtypeStruct((b, RET_HEADS, RET_DK, RET_DV), F32)],
        scratch_shapes=[pltpu.VMEM((RET_DK, RET_DV), F32)],
        compiler_params=_params(("parallel", "parallel", "arbitrary")),
        name="retention",
    )(lg, z, z, z, z, cos, sin, ret_norm.reshape(ret_norm.shape[0], 1, -1))


def _hgrn_gates(fa, lb):
    log_f = jnp.minimum(fa, 0.0) - jnp.log1p(jnp.exp(-jnp.abs(fa))) + jnp.log1p(lb * jnp.exp(-fa))
    hk = (1.0 - lb) * jax.nn.sigmoid(-fa)
    return log_f, hk


def _split3(x):
    hi = x.astype(BF16)
    r1 = x - hi.astype(F32)
    mid = r1.astype(BF16)
    lo = (r1 - mid.astype(F32)).astype(BF16)
    return hi, mid, lo


def _gla_state_step(q, k, v, b, st):
    b_last = b[GLA_CHUNK - 1:GLA_CHUNK, :]
    o = _bdot(q * jnp.exp(b), st, _NT)
    kd = k * jnp.exp(b_last - b)
    return o, st * jnp.exp(b_last) + _bdot(v, kd, _TN)


def _gla_intra_anchored(q, k, v, b):
    c = GLA_CHUNK
    nsub = c // GLA_SUB
    ti = lax.broadcasted_iota(jnp.int32, (c, c), 0)
    si = lax.broadcasted_iota(jnp.int32, (c, c), 1)
    anchors = [jnp.zeros((1, HD), F32)] + [b[i * GLA_SUB - 1:i * GLA_SUB, :] for i in range(1, nsub)]
    m_rows = jnp.concatenate([jnp.broadcast_to(m, (GLA_SUB, HD)) for m in anchors], axis=0)
    qs = q * jnp.exp(b - m_rows)
    a = jnp.zeros((c, c), F32)
    for i in range(nsub):
        ks = k * jnp.exp(jnp.minimum(anchors[i] - b, GLA_SAFE_SPAN))
        blk = (ti // GLA_SUB == i) & (si <= ti)
        a = a + jnp.where(blk, _bdot(qs, ks, _NT), 0.0)
    return _bdot(a, v)


def _gla_intra_pairwise(q, k, v, b):
    c = GLA_CHUNK
    nsub = c // GLA_SUB
    ti = lax.broadcasted_iota(jnp.int32, (c, c), 0)
    si = lax.broadcasted_iota(jnp.int32, (c, c), 1)
    a_off = jnp.zeros((c, c), F32)
    for i in range(1, nsub):
        m = b[i * GLA_SUB - 1:i * GLA_SUB, :]
        qs = q * jnp.exp(jnp.minimum(b - m, 0.0))
        ks = k * jnp.exp(jnp.minimum(m - b, 0.0))
        blk = (ti // GLA_SUB == i) & (si < i * GLA_SUB)
        a_off = a_off + jnp.where(blk, _bdot(qs, ks, _NT), 0.0)
    o = _bdot(a_off, v)
    rows = lax.broadcasted_iota(jnp.int32, (GLA_SUB, 1), 0)
    diag = []
    for i in range(nsub):
        sl = slice(i * GLA_SUB, (i + 1) * GLA_SUB)
        qi, ki, vi, bi = q[sl], k[sl], v[sl], b[sl]
        oi = jnp.zeros((GLA_SUB, HD), F32)
        for s in range(GLA_SUB):
            d = jnp.exp(jnp.minimum(bi - bi[s:s + 1], 0.0)) * qi * ki[s:s + 1]
            w = jnp.where(rows >= s, jnp.sum(d, axis=-1, keepdims=True), 0.0)
            oi = oi + w * vi[s:s + 1]
        diag.append(oi)
    return o + jnp.concatenate(diag, axis=0)


def _gla_kernel(f_ref, q_ref, i_ref, g_ref, lb_ref, nrm_ref, o_ref, st_ref, s_scr, *, nc, n_inner):
    c = pl.program_id(2)
    blk = f_ref.shape[0]

    @pl.when(c == 0)
    def _():
        s_scr[...] = jnp.zeros_like(s_scr)

    log_f, hk = _hgrn_gates(f_ref[...], lb_ref[...])
    hq = _silu(q_ref[...])
    v = i_ref[...]
    ti = lax.broadcasted_iota(jnp.int32, (blk, blk), 0)
    si = lax.broadcasted_iota(jnp.int32, (blk, blk), 1)
    tri = ((ti >= si) & (ti // GLA_CHUNK == si // GLA_CHUNK)).astype(BF16)
    hi, mid, lo = _split3(log_f)
    b = (jnp.dot(tri, hi, preferred_element_type=F32) + jnp.dot(tri, mid, preferred_element_type=F32)
         + jnp.dot(tri, lo, preferred_element_type=F32))
    sub_sum = jnp.sum(log_f.reshape(blk // GLA_SUB, GLA_SUB, HD), axis=1)
    anchored_ok = jnp.min(sub_sum) >= -GLA_SAFE_SPAN

    def run(intra):
        st = s_scr[...]
        outs = []
        for ci in range(n_inner):
            sl = slice(ci * GLA_CHUNK, (ci + 1) * GLA_CHUNK)
            o_inter, st = _gla_state_step(hq[sl], hk[sl], v[sl], b[sl], st)
            outs.append(o_inter + intra(hq[sl], hk[sl], v[sl], b[sl]))
        s_scr[...] = st
        o = jnp.concatenate(outs, axis=0)
        y = o * lax.rsqrt(jnp.mean(o * o, axis=-1, keepdims=True) + EPS) * nrm_ref[...]
        o_ref[...] = (y * jax.nn.sigmoid(g_ref[...])).astype(o_ref.dtype)

    lax.cond(anchored_ok, lambda: run(_gla_intra_anchored), lambda: run(_gla_intra_pairwise))

    @pl.when(c == nc - 1)
    def _():
        st_ref[...] = s_scr[...].T


def _gla(z, l, lb, hg_norm, b, t):
    _, dst, _, _, _ = _segments()
    blk = min(t, 256)
    assert blk % GLA_CHUNK == 0
    nc = t // blk

    def zspec(name):
        return pl.BlockSpec((blk, HD), lambda bi, h, c: (bi * nc + c, dst[name] // HD + h))

    vec = lambda: pl.BlockSpec((None, 1, HD), lambda bi, h, c: (l, 0, h))
    return pl.pallas_call(
        functools.partial(_gla_kernel, nc=nc, n_inner=blk // GLA_CHUNK),
        grid=(b, HG_HEADS, nc),
        in_specs=[zspec('h_f'), zspec('h_q'), zspec('h_i'), zspec('h_g'), vec(), vec()],
        out_specs=[pl.BlockSpec((blk, HD), lambda bi, h, c: (bi * nc + c, h)),
                   pl.BlockSpec((None, None, HG_DK, HG_DV), lambda bi, h, c: (bi, h, 0, 0))],
        out_shape=[jax.ShapeDtypeStruct((z.shape[0], HG_HEADS * HG_DV), BF16),
                   jax.ShapeDtypeStruct((b, HG_HEADS, HG_DK, HG_DV), F32)],
        scratch_shapes=[pltpu.VMEM((HG_DV, HG_DK), F32)],
        compiler_params=_params(("parallel", "parallel", "arbitrary")),
        name="hgrn2",
    )(z, z, z, z, lb.reshape(lb.shape[0], 1, -1), hg_norm.reshape(hg_norm.shape[0], 1, -1))


def _to_col(row):
    n = row.shape[1]
    eye = lax.broadcasted_iota(jnp.int32, (n, n), 0) == lax.broadcasted_iota(jnp.int32, (n, n), 1)
    return jnp.sum(jnp.where(eye, row, 0.0), axis=1, keepdims=True)


def _decode_rec_kernel(rq_ref, rk_ref, rv_ref, rg_ref, hf_ref, hq_ref, hi_ref, hg_ref,
                       cos_ref, sin_ref, rn_ref, hn_ref, lb_ref, sr_ref, sh_ref,
                       or_ref, oh_ref, nr_ref, nh_ref):
    cos, sin = cos_ref[...], sin_ref[...]

    def rms(o, g):
        return o * lax.rsqrt(jnp.mean(o * o, axis=-1, keepdims=True) + EPS) * g

    for h in range(RET_HEADS):
        sl = slice(h * HD, (h + 1) * HD)
        gamma = math.exp(_ret_gamma_log(h))
        q = _rot(rq_ref[:, sl], cos, sin, RET_DK // 2)
        k = _rot(rk_ref[:, sl], cos, sin, RET_DK // 2) * (RET_DK ** -0.5)
        v = rv_ref[:, sl]
        s = sr_ref[h]
        o = jnp.sum(q * k, axis=-1, keepdims=True) * v + jnp.sum(_to_col(q * gamma) * s, axis=0, keepdims=True)
        nr_ref[h] = s * gamma + _to_col(k) * v
        or_ref[:, sl] = rms(o, rn_ref[:, sl]) * _silu(rg_ref[:, sl])

    for h in range(HG_HEADS):
        sl = slice(h * HD, (h + 1) * HD)
        log_f, k = _hgrn_gates(hf_ref[:, sl], lb_ref[:, sl])
        q = _silu(hq_ref[:, sl])
        v = hi_ref[:, sl]
        s = sh_ref[h]
        eb = jnp.exp(log_f)
        o = jnp.sum(q * k, axis=-1, keepdims=True) * v + jnp.sum(_to_col(q * eb) * s, axis=0, keepdims=True)
        nh_ref[h] = s * _to_col(eb) + _to_col(k) * v
        oh_ref[:, sl] = rms(o, hn_ref[:, sl]) * jax.nn.sigmoid(hg_ref[:, sl])


def _decode_rec(z, l, cos, sin, ret_norm, hg_norm, lb, state_ret, state_hgrn):
    _, dst, padded, _, _ = _segments()
    nb = z.shape[0]

    z = z.reshape(nb, 1, z.shape[1])

    def zspec(name):
        w = padded[name]
        return pl.BlockSpec((None, 1, w), lambda bi: (bi, 0, dst[name] // w))

    one = lambda w: pl.BlockSpec((1, w), lambda bi: (0, 0))
    vec = lambda a: pl.BlockSpec((None, 1, a.shape[-1]), lambda bi: (l, 0, 0))
    st = lambda a: pl.BlockSpec((None, None) + a.shape[2:], lambda bi: (l, bi, 0, 0, 0))
    st_out = lambda a: pl.BlockSpec((None,) + a.shape[2:], lambda bi: (bi, 0, 0, 0))
    rw, hw = RET_HEADS * RET_DV, HG_HEADS * HG_DV
    r3 = lambda a: a.reshape(a.shape[0], 1, -1)
    return pl.pallas_call(
        _decode_rec_kernel,
        grid=(nb,),
        in_specs=[zspec('r_q'), zspec('r_k'), zspec('r_v'), zspec('r_g'),
                  zspec('h_f'), zspec('h_q'), zspec('h_i'), zspec('h_g'),
                  one(LANE), one(LANE), vec(ret_norm), vec(hg_norm), vec(lb), st(state_ret), st(state_hgrn)],
        out_specs=[pl.BlockSpec((None, 1, rw), lambda bi: (bi, 0, 0)),
                   pl.BlockSpec((None, 1, hw), lambda bi: (bi, 0, 0)),
                   st_out(state_ret), st_out(state_hgrn)],
        out_shape=[jax.ShapeDtypeStruct((nb, 1, rw), F32), jax.ShapeDtypeStruct((nb, 1, hw), F32),
                   jax.ShapeDtypeStruct(state_ret.shape[1:], F32), jax.ShapeDtypeStruct(state_hgrn.shape[1:], F32)],
        compiler_params=_params(("arbitrary",)),
        name="decode_recurrent",
    )(z, z, z, z, z, z, z, z, cos, sin, r3(ret_norm), r3(hg_norm), r3(lb), state_ret, state_hgrn)


def _sort_key(score):
    bits = lax.bitcast_convert_type(score + 0.0, jnp.int32)
    return bits ^ ((bits >> 31) & jnp.int32(0x7FFFFFFF))


def _count(mask):
    return jnp.sum(mask.astype(F32), axis=-1, keepdims=True)


def _nth_largest_key(count_ge, n_sel, shape):
    lo = jnp.where(count_ge(jnp.zeros(shape, jnp.int32)) >= n_sel, 0, INT_MIN).astype(jnp.int32)

    def body(i, lo):
        cand = lo | jnp.left_shift(jnp.int32(1), 30 - i)
        return jnp.where(count_ge(cand) >= n_sel, cand, lo)

    return lax.fori_loop(0, 31, body, lo)


def _tie_bound(count_eq_below, need, nbits, shape):
    def body(i, j):
        cand = j | jnp.left_shift(jnp.int32(1), nbits - 1 - i)
        return jnp.where(count_eq_below(cand) < need, cand, j)

    return lax.fori_loop(0, nbits, body, jnp.zeros(shape, jnp.int32))


def _dsa_prompt_body(length, aq_ref, iq_ref, iw_ref, k_ref, v_ref, ik_ref, o_ref, key_scr, sel_scr, n_sel):
    j = pl.program_id(1)
    tq = aq_ref.shape[0]
    ik = ik_ref[0:length, :]
    score = jnp.zeros((tq, length), F32)
    for h in range(IDX_HEADS):
        s = lax.dot_general(iq_ref[:, h * HD:(h + 1) * HD], ik, _NT, preferred_element_type=F32)
        score = score + jnp.maximum(s, 0.0) * iw_ref[:, h:h + 1]
    q_pos = j * tq + lax.broadcasted_iota(jnp.int32, (tq, 1), 0)
    col = lax.broadcasted_iota(jnp.int32, (tq, length), 1)
    visible = col <= q_pos
    key_scr[:, 0:length] = _sort_key(jnp.where(visible, score, NEG_BIG))

    tau = _nth_largest_key(lambda c: _count(key_scr[:, 0:length] >= c), n_sel, (tq, 1))
    key = key_scr[:, 0:length]
    gt = key > tau
    eq = key == tau
    need = n_sel - _count(gt)
    spare = jnp.max(_count(eq & visible) - need) > 0.0
    bound = lax.cond(
        spare,
        lambda: _tie_bound(lambda c: _count((key_scr[:, 0:length] == tau) & (col < c)), need,
                           max(1, length.bit_length()), (tq, 1)),
        lambda: jnp.full((tq, 1), length, jnp.int32))
    sel_scr[:, 0:length] = ((gt | (eq & (col <= bound))) & visible).astype(F32)

    group = ATT_HEADS // ATT_KV_HEADS
    scale = ATT_HEAD_DIM ** -0.5
    for n in range(ATT_KV_HEADS):
        kn = k_ref[0:length, n * HD:(n + 1) * HD]
        vn = v_ref[0:length, n * HD:(n + 1) * HD]
        for g in range(group):
            sl = slice((n * group + g) * HD, (n * group + g + 1) * HD)
            s = lax.dot_general(aq_ref[:, sl], kn, _NT, preferred_element_type=F32) * scale
            s = jnp.where(sel_scr[:, 0:length] > 0.0, s, NEG_BIG)
            m = jnp.max(s, axis=-1, keepdims=True)
            p = jnp.exp(s - m)
            o = _bdot(p, vn) / jnp.sum(p, axis=-1, keepdims=True)
            o_ref[:, sl] = o.astype(o_ref.dtype)


def _dsa_prompt_kernel(*refs, n_sel, lengths):
    j = pl.program_id(1)
    tq = refs[0].shape[0]
    prev = 0
    for length in lengths:
        @pl.when((j >= prev // tq) & (j < length // tq))
        def _(length=length):
            _dsa_prompt_body(length, *refs, n_sel)
        prev = length


def _dsa_prompt(aq, iq, iw, kb, vb, ikb, b, t, mt):
    tq = min(Q_BLOCK, t)
    nq = t // tq
    n_sel = min(TOPK_MAX, t // 4)
    step = min(t, 512)
    lengths = tuple(range(step, t + 1, step))
    assert t % step == 0 and step % tq == 0 and step >= n_sel
    qrow = lambda w: pl.BlockSpec((tq, w), lambda bi, j: (bi * nq + j, 0))
    krow = lambda w: pl.BlockSpec((t, w), lambda bi, j: (bi, 0))
    return pl.pallas_call(
        functools.partial(_dsa_prompt_kernel, n_sel=n_sel, lengths=lengths),
        grid=(b, nq),
        in_specs=[qrow(aq.shape[1]), qrow(iq.shape[1]), qrow(LANE),
                  krow(kb.shape[1]), krow(vb.shape[1]), krow(HD)],
        out_specs=qrow(aq.shape[1]),
        out_shape=jax.ShapeDtypeStruct((mt, aq.shape[1]), BF16),
        scratch_shapes=[pltpu.VMEM((tq, t), jnp.int32), pltpu.VMEM((tq, t), F32)],
        compiler_params=_params(("parallel", "arbitrary")),
        name="dsa_prompt",
    )(aq, iq, iw, kb, vb, ikb)


def _dec_score_kernel(pt_ref, iq_ref, iw_ref, ikn_ref, *refs):
    pages = refs[:PAGES_PER_STEP]
    sc_ref, new_ref = refs[PAGES_PER_STEP:]
    iq = iq_ref[...].astype(BF16)
    iw = iw_ref[...]
    ik = jnp.concatenate([p[...].astype(BF16) for p in pages], axis=0)
    s = lax.dot_general(iq, ik, _NT, preferred_element_type=F32)
    sc_ref[...] = jnp.sum(jnp.maximum(s, 0.0) * iw, axis=0, keepdims=True)

    @pl.when(pl.program_id(1) == 0)
    def _():
        sn = lax.dot_general(iq, jnp.broadcast_to(ikn_ref[...], (8, HD)).astype(BF16), _NT,
                             preferred_element_type=F32)[:, 0:1]
        new_ref[...] = jnp.broadcast_to(jnp.sum(jnp.maximum(sn, 0.0) * iw, axis=0, keepdims=True), (1, LANE))


def _dec_scores(page_table, iq, iw, ik_new, cache_ik, l):
    nb, n_pages = page_table.shape
    steps = n_pages // PAGES_PER_STEP
    ih = iq.shape[1]
    page = lambda r: pl.BlockSpec((None, None, PAGE_SIZE, HD),
                                  lambda bi, p, pt: (l, pt[bi, p * PAGES_PER_STEP + r], 0, 0))
    grid_spec = pltpu.PrefetchScalarGridSpec(
        num_scalar_prefetch=1,
        grid=(nb, steps),
        in_specs=[pl.BlockSpec((None, ih, HD), lambda bi, p, pt: (bi, 0, 0)),
                  pl.BlockSpec((None, ih, 1), lambda bi, p, pt: (bi, 0, 0)),
                  pl.BlockSpec((None, 1, HD), lambda bi, p, pt: (bi, 0, 0))]
                 + [page(r) for r in range(PAGES_PER_STEP)],
        out_specs=[pl.BlockSpec((None, 1, PAGES_PER_STEP * PAGE_SIZE), lambda bi, p, pt: (bi, 0, p)),
                   pl.BlockSpec((None, 1, LANE), lambda bi, p, pt: (bi, 0, 0))],
    )
    return pl.pallas_call(
        _dec_score_kernel,
        grid_spec=grid_spec,
        out_shape=[jax.ShapeDtypeStruct((nb, 1, n_pages * PAGE_SIZE), F32),
                   jax.ShapeDtypeStruct((nb, 1, LANE), F32)],
        compiler_params=_params(("parallel", "arbitrary")),
        name="dec_scores",
    )(page_table, iq, iw, ik_new, *([cache_ik] * PAGES_PER_STEP))


def _dec_attn_kernel(pt_ref, sc_all_ref, sc_new_ref, sc_ref, q_ref, kn_ref, vn_ref, *refs,
                     n_sel, nbits, steps):
    kp = refs[:PAGES_PER_STEP]
    vp = refs[PAGES_PER_STEP:2 * PAGES_PER_STEP]
    o_ref, tau_scr, bnd_scr, m_scr, l_scr, acc_scr = refs[2 * PAGES_PER_STEP:]
    p_id = pl.program_id(1)
    chunk = PAGES_PER_STEP * PAGE_SIZE
    scale = ATT_HEAD_DIM ** -0.5
    lane0 = lax.broadcasted_iota(jnp.int32, (1, LANE), 1) == 0

    @pl.when(p_id == 0)
    def _():
        key = _sort_key(sc_all_ref[...])
        key_new = _sort_key(sc_new_ref[...])
        idx = lax.broadcasted_iota(jnp.int32, key.shape, 1)
        cnt = lambda mp, mn: _count(mp) + _count(mn & lane0)
        tau = _nth_largest_key(lambda c: cnt(key >= c, key_new >= c), n_sel, (1, 1))
        need = n_sel - cnt(key > tau, key_new > tau)
        bound = _tie_bound(lambda c: _count((key == tau) & (idx < c)), need, nbits, (1, 1))
        tau_scr[...] = jnp.broadcast_to(tau, tau_scr.shape)
        bnd_scr[...] = jnp.broadcast_to(bound, bnd_scr.shape)
        m_scr[...] = jnp.full_like(m_scr, NEG_BIG)
        l_scr[...] = jnp.zeros_like(l_scr)
        acc_scr[...] = jnp.zeros_like(acc_scr)

    tau = tau_scr[:, 0:1]
    bound = bnd_scr[:, 0:1]
    key_c = _sort_key(sc_ref[...])
    idx_c = p_id * chunk + lax.broadcasted_iota(jnp.int32, key_c.shape, 1)
    sel = (key_c > tau) | ((key_c == tau) & (idx_c <= bound))
    q = q_ref[...].astype(BF16)
    heads = lambda pages, n: jnp.concatenate(
        [r[pl.ds(n, PAGE_SIZE, stride=ATT_KV_HEADS), :].astype(BF16) for r in pages], axis=0)
    s = sum(lax.dot_general(q[:, n * HD:(n + 1) * HD], heads(kp, n), _NT, preferred_element_type=F32)
            for n in range(ATT_KV_HEADS)) * scale
    s = jnp.where(sel, s, NEG_BIG)
    m_old = m_scr[:, 0:1]
    m_new = jnp.maximum(m_old, jnp.max(s, axis=-1, keepdims=True))
    alpha = jnp.exp(m_old - m_new)
    p = jnp.where(sel, jnp.exp(s - m_new), 0.0)
    l_new = alpha * l_scr[:, 0:1] + jnp.sum(p, axis=-1, keepdims=True)
    pb = p.astype(BF16)
    acc_new = alpha * acc_scr[...] + jnp.concatenate(
        [jnp.dot(pb, heads(vp, n), preferred_element_type=F32) for n in range(ATT_KV_HEADS)], axis=1)
    m_scr[...] = jnp.broadcast_to(m_new, m_scr.shape)
    l_scr[...] = jnp.broadcast_to(l_new, l_scr.shape)
    acc_scr[...] = acc_new

    @pl.when(p_id == steps - 1)
    def _():
        key_all = _sort_key(sc_all_ref[...])
        key_new = _sort_key(sc_new_ref[...])[:, 0:1]
        need = n_sel - _count(key_all > tau) - (key_new > tau).astype(F32)
        n_eq = _count(key_all == tau)
        sel_new = (key_new > tau) | ((key_new == tau) & (n_eq < need))
        qf = q_ref[...]
        s_new = jnp.sum(qf * kn_ref[...], axis=-1, keepdims=True) * scale
        s_new = jnp.where(sel_new, s_new, NEG_BIG)
        m_fin = jnp.maximum(m_new, s_new)
        a2 = jnp.exp(m_new - m_fin)
        p_new = jnp.where(sel_new, jnp.exp(s_new - m_fin), 0.0)
        l_fin = a2 * l_new + p_new
        acc_fin = (a2 * acc_new + p_new * vn_ref[...]) / l_fin
        group = ATT_HEADS // ATT_KV_HEADS
        for h in range(ATT_HEADS):
            n = h // group
            o_ref[:, h * HD:(h + 1) * HD] = acc_fin[h:h + 1, n * HD:(n + 1) * HD]


def _dec_attn(page_table, scores, score_new, q_bd, k_new, v_new, cache_k, cache_v, l):
    nb, n_pages = page_table.shape
    steps = n_pages // PAGES_PER_STEP
    s_len = n_pages * PAGE_SIZE
    chunk = PAGES_PER_STEP * PAGE_SIZE
    n_sel = min(TOPK_MAX, (s_len + 1) // 4)
    nbits = s_len.bit_length()
    kvw = ATT_KV_HEADS * HD
    hp = q_bd.shape[1]
    page = lambda r: pl.BlockSpec((None, None, PAGE_SIZE * ATT_KV_HEADS, HD),
                                  lambda bi, p, pt: (l, pt[bi, p * PAGES_PER_STEP + r], 0, 0))
    grid_spec = pltpu.PrefetchScalarGridSpec(
        num_scalar_prefetch=1,
        grid=(nb, steps),
        in_specs=[pl.BlockSpec((None, 1, s_len), lambda bi, p, pt: (bi, 0, 0)),
                  pl.BlockSpec((None, 1, LANE), lambda bi, p, pt: (bi, 0, 0)),
                  pl.BlockSpec((None, 1, chunk), lambda bi, p, pt: (bi, 0, p)),
                  pl.BlockSpec((None, hp, kvw), lambda bi, p, pt: (bi, 0, 0)),
                  pl.BlockSpec((None, 1, kvw), lambda bi, p, pt: (bi, 0, 0)),
                  pl.BlockSpec((None, 1, kvw), lambda bi, p, pt: (bi, 0, 0))]
                 + [page(r) for r in range(PAGES_PER_STEP)] * 2,
        out_specs=pl.BlockSpec((None, 1, ATT_HEADS * HD), lambda bi, p, pt: (bi, 0, 0)),
        scratch_shapes=[pltpu.VMEM((1, LANE), jnp.int32), pltpu.VMEM((1, LANE), jnp.int32),
                        pltpu.VMEM((hp, LANE), F32), pltpu.VMEM((hp, LANE), F32),
                        pltpu.VMEM((hp, kvw), F32)],
    )
    return pl.pallas_call(
        functools.partial(_dec_attn_kernel, n_sel=n_sel, nbits=nbits, steps=steps),
        grid_spec=grid_spec,
        out_shape=jax.ShapeDtypeStruct((nb, 1, ATT_HEADS * HD), F32),
        compiler_params=_params(("parallel", "arbitrary")),
        name="dec_attn",
    )(page_table, scores, score_new, scores, q_bd, k_new, v_new,
      *([cache_k] * PAGES_PER_STEP), *([cache_v] * PAGES_PER_STEP)).reshape(nb, ATT_HEADS * HD)


def _ffn(x, norm, w1, w3, w2, l, tm):
    h = _rmsnorm(x, norm, l, BF16)
    tm_up = 2 * tm if (x.shape[0] // tm) % 2 == 0 else tm
    g = _matmul(h, [w1, w3], l, tm=tm_up, epilogue='swiglu', out_dtype=BF16)
    return _matmul(g, [w2], l, tm=tm, epilogue='resid', res=x, scale=0.5)


def _token_tiles(n_tokens):
    n_tiles = max(1, n_tokens // 1024)
    tm = -(-n_tokens // (16 * n_tiles)) * 16
    return tm, n_tiles


def kernel(x_prompt, x_sample, state_ret, state_hgrn, cache_k, cache_v, cache_idx_k, page_table, ffn1_norm, ffn1_w1, ffn1_w3, ffn1_w2, mix_norm, w_in, ret_norm, q_norm, k_norm, idx_k_g, idx_k_b, hg_lb_raw, hg_norm, w_up_ret, w_up_att, w_up_hg, w_out, ffn2_norm, ffn2_w1, ffn2_w3, ffn2_w2):
    b, t, d = x_prompt.shape
    nb = x_sample.shape[0]
    depth = w_in.shape[0]
    kvw = ATT_KV_HEADS * ATT_HEAD_DIM
    group = ATT_HEADS // ATT_KV_HEADS

    lb_soft = jax.nn.softmax(hg_lb_raw.astype(F32), axis=0)
    lb_all = jnp.cumsum(lb_soft, axis=0) - lb_soft[0]
    w_in_p = _repack_w_in(w_in)
    cast = lambda w: w.astype(BF16)
    ffn1_w2, ffn2_w2 = cast(ffn1_w2), cast(ffn2_w2)
    w_up_ret, w_up_att, w_up_hg, w_out = cast(w_up_ret), cast(w_up_att), cast(w_up_hg), cast(w_out)

    pos_p = jnp.arange(t, dtype=jnp.int32)
    pos_s = jnp.full((nb,), PAST_LEN, jnp.int32)
    ret_f = 1.0 / (ROPE_THETA ** jnp.linspace(0.0, 1.0, RET_DK // 2, dtype=F32))
    att_f = ROPE_THETA ** (-jnp.arange(0, ATT_HEAD_DIM, 2, dtype=F32) / ATT_HEAD_DIM)
    idx_f = ROPE_THETA ** (-jnp.arange(0, IDX_ROPE_DIM, 2, dtype=F32) / IDX_ROPE_DIM)
    tabs_p = _rope_tables(pos_p, att_f, ATT_HEAD_DIM) + _rope_tables(pos_p, idx_f, IDX_ROPE_DIM)
    tabs_s = _rope_tables(pos_s, att_f, ATT_HEAD_DIM) + _rope_tables(pos_s, idx_f, IDX_ROPE_DIM)
    ret_tab_p = _rope_tables(pos_p, ret_f, RET_DK)
    ret_tab_s = _rope_tables(pos_s[:1], ret_f, RET_DK)

    ck = cache_k.reshape(cache_k.shape[:2] + (PAGE_SIZE * ATT_KV_HEADS, HD))
    cv = cache_v.reshape(cache_v.shape[:2] + (PAGE_SIZE * ATT_KV_HEADS, HD))

    mp = b * t
    tm, n_tiles = _token_tiles(mp + nb)
    mt = tm * n_tiles
    x = jnp.concatenate([x_prompt.reshape(mp, d), x_sample.reshape(nb, d), jnp.zeros((mt - mp - nb, d), F32)])

    def with_tail(a, rows):
        tail = jnp.concatenate([rows.astype(a.dtype), jnp.zeros((mt - mp - nb, a.shape[1]), a.dtype)])
        return lax.dynamic_update_slice(a, tail, (mp, 0))

    outs = {n: [] for n in ('rp', 'rs', 'hp', 'hs', 'kp', 'vp', 'ip', 'ks', 'vs', 'is')}
    for l in range(depth):
        x = _ffn(x, ffn1_norm, ffn1_w1, ffn1_w3, ffn1_w2, l, tm)
        h = _rmsnorm(x, mix_norm, l, BF16)
        z = _matmul(h, [w_in_p], l, tm=tm)

        aq, ak, av, akb, avb, iq, ik, ikb, iw = _prelude(z, mp, l, tabs_p, q_norm, k_norm, idx_k_g, idx_k_b, t)
        a_ret, r_p = _retention(z, l, ret_tab_p[0], ret_tab_p[1], ret_norm, b, t)
        a_hg, h_p = _gla(z, l, lb_all, hg_norm, b, t)
        a_att = _dsa_prompt(aq, iq, iw, akb, avb, ikb, b, t, mt)
        outs['rp'].append(r_p); outs['hp'].append(h_p)
        outs['kp'].append(ak.reshape(b, t, ATT_KV_HEADS, ATT_HEAD_DIM))
        outs['vp'].append(av.reshape(b, t, ATT_KV_HEADS, ATT_HEAD_DIM))
        outs['ip'].append(ik.reshape(b, t, IDX_DIM))

        zs = z[mp:mp + nb]
        aq, ak, av, _, _, iq, ik, _, iw = _prelude(zs, nb, l, tabs_s, q_norm, k_norm, idx_k_g, idx_k_b, 1)
        s_ret, s_hg, r_s, h_s = _decode_rec(zs, l, ret_tab_s[0], ret_tab_s[1], ret_norm, hg_norm, lb_all,
                                            state_ret, state_hgrn)
        scores, score_new = _dec_scores(page_table, iq.reshape(nb, IDX_HEADS, IDX_DIM),
                                        iw[:, :IDX_HEADS].reshape(nb, IDX_HEADS, 1),
                                        ik.reshape(nb, 1, IDX_DIM), cache_idx_k, l)
        qh = aq.reshape(nb, ATT_KV_HEADS, group, ATT_HEAD_DIM)
        q_bd = jnp.concatenate(
            [jnp.concatenate([qh[:, n] if m == n else jnp.zeros_like(qh[:, n]) for m in range(ATT_KV_HEADS)], axis=-1)
             for n in range(ATT_KV_HEADS)]
            + [jnp.zeros((nb, max(0, 16 - ATT_HEADS), kvw), aq.dtype)], axis=1)
        s_att = _dec_attn(page_table, scores, score_new, q_bd, ak.reshape(nb, 1, kvw), av.reshape(nb, 1, kvw),
                          ck, cv, l)
        outs['rs'].append(r_s); outs['hs'].append(h_s)
        outs['ks'].append(ak.reshape(nb, 1, ATT_KV_HEADS, ATT_HEAD_DIM))
        outs['vs'].append(av.reshape(nb, 1, ATT_KV_HEADS, ATT_HEAD_DIM))
        outs['is'].append(ik.reshape(nb, 1, IDX_DIM))

        merged = _merge(with_tail(a_ret, s_ret.reshape(nb, -1)), with_tail(a_att, s_att),
                        with_tail(a_hg, s_hg.reshape(nb, -1)), w_up_ret, w_up_att, w_up_hg, z, l, BF16, tm)
        x = _matmul(merged, [w_out], l, tm=tm, epilogue='resid', res=x, scale=1.0)
        x = _ffn(x, ffn2_norm, ffn2_w1, ffn2_w3, ffn2_w2, l, tm)

    st = lambda n: jnp.stack(outs[n])
    return (x[:mp].reshape(b, t, d), x[mp:mp + nb].reshape(nb, 1, d),
            st('rp').astype(state_ret.dtype), st('rs').astype(state_ret.dtype),
            st('hp').astype(state_hgrn.dtype), st('hs').astype(state_hgrn.dtype),
            st('kp'), st('vp'), st('ip'), st('ks'), st('vs'), st('is'))
```

```python
import functools
import math

import numpy as np
import jax
import jax.numpy as jnp
from jax import lax
from jax.experimental import pallas as pl
from jax.experimental.pallas import tpu as pltpu

D_MODEL = 4096
BATCH = 4
SEQ = 2048
DEPTH = 2
DEC_BATCH = 8
DEC_SEQ = 1
PAST_LEN = 16384
PAGE_SIZE = 128

RET_HEADS = 8
RET_DK = 128
RET_DV = 128
ATT_HEADS = 8
ATT_KV_HEADS = 2
ATT_HEAD_DIM = 128
IDX_HEADS = 16
IDX_DIM = 128
IDX_ROPE_DIM = 64
TOPK_MAX = 256
Q_BLOCK = 128
HG_HEADS = 8
HG_DK = 128
HG_DV = 128
D_FF = 11008
ROPE_THETA = 10000.0
EPS = 1e-6
NEG_BIG = -1e30

F32 = jnp.float32
BF16 = jnp.bfloat16
LANE = 128
HD = 128
INT_MIN = -(2 ** 31)
VMEM_LIMIT = 56 * 1024 * 1024
GLA_CHUNK = 64
GLA_SUB = 16
GLA_SAFE_SPAN = 60.0
PAGES_PER_STEP = 16

_NT = (((1,), (1,)), ((), ()))
_TN = (((0,), (0,)), ((), ()))


def _params(sem):
    return pltpu.CompilerParams(dimension_semantics=sem, vmem_limit_bytes=VMEM_LIMIT)


def _bdot(a, b, dims=None):
    a = a.astype(BF16)
    b = b.astype(BF16)
    if dims is None:
        return jnp.dot(a, b, preferred_element_type=F32)
    return lax.dot_general(a, b, dims, preferred_element_type=F32)


def _silu(x):
    return x * jax.nn.sigmoid(x)


def _segments():
    ret_qk = RET_HEADS * RET_DK
    ret_w = RET_HEADS * RET_DV
    att_w = ATT_HEADS * ATT_HEAD_DIM
    kv_w = ATT_KV_HEADS * ATT_HEAD_DIM
    hg_k = HG_HEADS * HG_DK
    hg_w = HG_HEADS * HG_DV
    names = ['r_q', 'r_k', 'r_v', 'r_g', 'a_q', 'a_k', 'a_v', 'i_q', 'i_k', 'i_w',
             'h_f', 'h_q', 'h_i', 'h_g', 'g_ret', 'g_att', 'g_hg']
    widths = [ret_qk, ret_qk, ret_w, ret_w, att_w, kv_w, kv_w, IDX_HEADS * IDX_DIM, IDX_DIM, IDX_HEADS,
              hg_k, hg_k, hg_w, hg_w, D_MODEL, D_MODEL, D_MODEL]
    src = {}
    off = 0
    for n, w in zip(names, widths):
        src[n] = (off, w)
        off += w
    padded = {n: -(-w // LANE) * LANE for n, w in zip(names, widths)}
    order = ['g_ret', 'g_att', 'g_hg'] + sorted(
        [n for n in names if not n.startswith('g_')], key=lambda n: -padded[n])
    dst = {}
    off = 0
    for n in order:
        dst[n] = off
        off += padded[n]
    total = -(-off // 256) * 256
    for n in order:
        assert dst[n] % padded[n] == 0 or n.startswith('g_'), (n, dst[n], padded[n])
    return src, dst, padded, order, total


def _repack_kernel(start_ref, valid_ref, a_ref, o_ref):
    c = pl.program_id(1)
    a = a_ref[0]
    row = lax.broadcasted_iota(jnp.int32, a.shape, 0)
    o_ref[...] = jnp.where(row < valid_ref[c], a, 0.0).T.astype(o_ref.dtype)


def _repack_w_in(w_in):
    src, dst, padded, order, total = _segments()
    depth, kdim, n_in = w_in.shape
    start = np.zeros((total // LANE,), np.int32)
    valid = np.zeros((total // LANE,), np.int32)
    for n in order:
        s, w = src[n]
        for t in range(padded[n] // LANE):
            c = dst[n] // LANE + t
            start[c], valid[c] = s + t * LANE, min(LANE, w - t * LANE)
    assert (start + LANE <= n_in).all() and (start % 8 == 0).all()
    start //= 8
    grid_spec = pltpu.PrefetchScalarGridSpec(
        num_scalar_prefetch=2,
        grid=(depth, total // LANE),
        in_specs=[pl.BlockSpec((pl.Element(1), pl.Element(LANE), pl.Element(kdim)),
                               lambda l, c, start, valid: (l, start[c] * 8, 0))],
        out_specs=pl.BlockSpec((None, kdim, LANE), lambda l, c, start, valid: (l, 0, c)),
    )
    return pl.pallas_call(
        _repack_kernel,
        grid_spec=grid_spec,
        out_shape=jax.ShapeDtypeStruct((depth, kdim, total), BF16),
        compiler_params=_params(("parallel", "parallel")),
        name="repack_w_in",
    )(jnp.asarray(start), jnp.asarray(valid), jnp.swapaxes(w_in, 1, 2))


def _rmsnorm_kernel(x_ref, g_ref, o_ref):
    x = x_ref[...]
    ms = jnp.mean(x * x, axis=-1, keepdims=True)
    o_ref[...] = (x * lax.rsqrt(ms + EPS) * g_ref[...]).astype(o_ref.dtype)


def _row_tile(m, cap):
    best = None
    for t in range(16, min(m, cap) + 1, 16):
        if m % t == 0:
            best = t
    return best if best is not None else m


def _rmsnorm(x, g_all, l, out_dtype):
    m, d = x.shape
    tr = _row_tile(m, 512)
    return pl.pallas_call(
        _rmsnorm_kernel,
        grid=(m // tr,),
        in_specs=[pl.BlockSpec((tr, d), lambda i: (i, 0)),
                  pl.BlockSpec((None, 1, d), lambda i: (l, 0, 0))],
        out_specs=pl.BlockSpec((tr, d), lambda i: (i, 0)),
        out_shape=jax.ShapeDtypeStruct((m, d), out_dtype),
        compiler_params=_params(("parallel",)),
        name="rmsnorm",
    )(x, g_all.reshape(g_all.shape[0], 1, d))


def _mm_kernel(*refs, n_w, nk, epilogue, scale):
    lhs_ref = refs[0]
    w_refs = refs[1:1 + n_w]
    pos = 1 + n_w
    res_ref = None
    if epilogue == 'resid':
        res_ref = refs[pos]
        pos += 1
    out_ref = refs[pos]
    acc_refs = refs[pos + 1:]

    lhs = lhs_ref[...].astype(BF16)
    parts = [jnp.dot(lhs, w[...].astype(BF16), preferred_element_type=F32) for w in w_refs]

    def finish(vals):
        if epilogue == 'swiglu':
            out = _silu(vals[0]) * vals[1]
        elif epilogue == 'resid':
            out = res_ref[...] + scale * vals[0]
        else:
            out = vals[0]
        out_ref[...] = out.astype(out_ref.dtype)

    if nk == 1:
        finish(parts)
        return

    k = pl.program_id(2)

    @pl.when(k == 0)
    def _():
        for a, p in zip(acc_refs, parts):
            a[...] = p

    @pl.when(k > 0)
    def _():
        for a, p in zip(acc_refs, parts):
            a[...] += p

    @pl.when(k == nk - 1)
    def _():
        finish([a[...] for a in acc_refs])


def _pick(n, cands):
    for c in cands:
        if n % c == 0:
            return c
    return n


def _matmul(lhs, ws, l, *, tm, epilogue='plain', res=None, scale=1.0, out_dtype=F32, tn=256, tk=None,
            single_lhs=None):
    m, kdim = lhs.shape
    n = ws[0].shape[-1]
    tn = _pick(n, (tn, 256, 128))
    if tk is None:
        tk = kdim
    assert m % tm == 0 and n % tn == 0 and kdim % tk == 0
    nk = kdim // tk
    if single_lhs is None:
        single_lhs = nk == 1 and tm * kdim * lhs.dtype.itemsize > (12 << 20)
    lhs_mode = dict(pipeline_mode=pl.Buffered(1)) if single_lhs else {}
    in_specs = [pl.BlockSpec((tm, tk), lambda i, j, k: (i, k), **lhs_mode)]
    in_specs += [pl.BlockSpec((None, tk, tn), lambda i, j, k: (l, k, j)) for _ in ws]
    args = [lhs] + list(ws)
    if epilogue == 'resid':
        in_specs.append(pl.BlockSpec((tm, tn), lambda i, j, k: (i, j)))
        args.append(res)
    scratch = [pltpu.VMEM((tm, tn), F32) for _ in ws] if nk > 1 else []
    return pl.pallas_call(
        functools.partial(_mm_kernel, n_w=len(ws), nk=nk, epilogue=epilogue, scale=scale),
        grid=(m // tm, n // tn, nk),
        in_specs=in_specs,
        out_specs=pl.BlockSpec((tm, tn), lambda i, j, k: (i, j)),
        out_shape=jax.ShapeDtypeStruct((m, n), out_dtype),
        scratch_shapes=scratch,
        compiler_params=_params(("parallel", "parallel", "arbitrary")),
        name="mm_" + epilogue,
    )(*args)


def _merge_kernel(ar_ref, aa_ref, ah_ref, wr_ref, wa_ref, wh_ref, gr_ref, ga_ref, gh_ref, o_ref):
    u_r = _bdot(ar_ref[...], wr_ref[...])
    u_a = _bdot(aa_ref[...], wa_ref[...])
    u_h = _bdot(ah_ref[...], wh_ref[...])
    out = (jax.nn.sigmoid(gr_ref[...]) * u_r + jax.nn.sigmoid(ga_ref[...]) * u_a
           + jax.nn.sigmoid(gh_ref[...]) * u_h)
    o_ref[...] = out.astype(o_ref.dtype)


def _merge(a_ret, a_att, a_hg, w_r, w_a, w_h, z, l, out_dtype, tm):
    m = a_ret.shape[0]
    d = w_r.shape[-1]
    tn = _pick(d, (256, 128))
    nb = d // tn
    lhs_spec = lambda a: pl.BlockSpec((tm, a.shape[1]), lambda i, j: (i, 0))
    w_spec = lambda w: pl.BlockSpec((None, w.shape[1], tn), lambda i, j: (l, 0, j))
    gate_spec = lambda g: pl.BlockSpec((tm, tn), lambda i, j: (i, g * nb + j))
    return pl.pallas_call(
        _merge_kernel,
        grid=(m // tm, nb),
        in_specs=[lhs_spec(a_ret), lhs_spec(a_att), lhs_spec(a_hg), w_spec(w_r), w_spec(w_a), w_spec(w_h),
                  gate_spec(0), gate_spec(1), gate_spec(2)],
        out_specs=pl.BlockSpec((tm, tn), lambda i, j: (i, j)),
        out_shape=jax.ShapeDtypeStruct((m, d), out_dtype),
        compiler_params=_params(("parallel", "parallel")),
        name="merge",
    )(a_ret, a_att, a_hg, w_r, w_a, w_h, z, z, z)


def _rot(x, cos, sin, half):
    up = pltpu.roll(x, LANE - half, 1)
    dn = pltpu.roll(x, half, 1)
    lane = lax.broadcasted_iota(jnp.int32, x.shape, 1)
    return x * cos + jnp.where(lane < half, up, dn) * sin


def _rope_tables(pos, freqs, width):
    ang = pos.astype(F32)[:, None] * freqs[None, :]
    cos, sin = jnp.cos(ang), jnp.sin(ang)
    t = pos.shape[0]
    pad_c = jnp.ones((t, LANE - width), F32)
    pad_s = jnp.zeros((t, LANE - width), F32)
    return (jnp.concatenate([cos, cos, pad_c], axis=1), jnp.concatenate([-sin, sin, pad_s], axis=1))


def _prelude_kernel(zq_ref, zk_ref, zv_ref, ziq_ref, zik_ref, ziw_ref, ca_ref, sa_ref, ci_ref, si_ref,
                    qn_ref, kn_ref, ig_ref, ib_ref,
                    aq_ref, ak_ref, av_ref, akb_ref, avb_ref, iq_ref, ik_ref, ikb_ref, iw_ref):
    ca, sa, ci, si = ca_ref[...], sa_ref[...], ci_ref[...], si_ref[...]

    def head_norm(x, g):
        return x * lax.rsqrt(jnp.mean(x * x, axis=-1, keepdims=True) + EPS) * g

    for h in range(ATT_HEADS):
        sl = slice(h * HD, (h + 1) * HD)
        aq_ref[:, sl] = _rot(head_norm(zq_ref[:, sl], qn_ref[...]), ca, sa, HD // 2).astype(aq_ref.dtype)
    for h in range(ATT_KV_HEADS):
        sl = slice(h * HD, (h + 1) * HD)
        k = _rot(head_norm(zk_ref[:, sl], kn_ref[...]), ca, sa, HD // 2)
        ak_ref[:, sl] = k
        akb_ref[:, sl] = k.astype(akb_ref.dtype)
    v = zv_ref[...]
    av_ref[...] = v
    avb_ref[...] = v.astype(avb_ref.dtype)
    for h in range(IDX_HEADS):
        sl = slice(h * HD, (h + 1) * HD)
        iq_ref[:, sl] = (_rot(ziq_ref[:, sl], ci, si, IDX_ROPE_DIM // 2) * (IDX_DIM ** -0.5)).astype(iq_ref.dtype)
    x = zik_ref[...]
    mu = jnp.mean(x, axis=-1, keepdims=True)
    var = jnp.mean(jnp.square(x - mu), axis=-1, keepdims=True)
    ik = _rot((x - mu) * lax.rsqrt(var + EPS) * ig_ref[...] + ib_ref[...], ci, si, IDX_ROPE_DIM // 2)
    ik_ref[...] = ik
    ikb_ref[...] = ik.astype(ikb_ref.dtype)
    iw_ref[...] = ziw_ref[...] * (IDX_HEADS ** -0.5)


def _prelude(z, m, l, tabs, q_norm, k_norm, idx_g, idx_b, t_len):
    _, dst, padded, _, _ = _segments()
    tr = min(m, 256)
    nt = t_len // tr if t_len >= tr else 1
    lowp = BF16 if tr >= 16 else F32
    aw, kvw, iw = padded['a_q'], padded['a_k'], padded['i_q']

    def zspec(name):
        w = padded[name]
        return pl.BlockSpec((tr, w), lambda i: (i, dst[name] // w))

    tab_spec = pl.BlockSpec((tr, LANE), lambda i: (i % nt, 0))
    vec_spec = pl.BlockSpec((None, 1, HD), lambda i: (l, 0, 0))
    row = lambda w: pl.BlockSpec((tr, w), lambda i: (i, 0))
    shp = lambda w, dt: jax.ShapeDtypeStruct((m, w), dt)
    vec = lambda a: a.reshape(a.shape[0], 1, HD)
    return pl.pallas_call(
        _prelude_kernel,
        grid=(m // tr,),
        in_specs=[zspec('a_q'), zspec('a_k'), zspec('a_v'), zspec('i_q'), zspec('i_k'), zspec('i_w'),
                  tab_spec, tab_spec, tab_spec, tab_spec, vec_spec, vec_spec, vec_spec, vec_spec],
        out_specs=[row(aw), row(kvw), row(kvw), row(kvw), row(kvw), row(iw), row(HD), row(HD), row(LANE)],
        out_shape=[shp(aw, lowp), shp(kvw, F32), shp(kvw, F32), shp(kvw, lowp), shp(kvw, lowp),
                   shp(iw, lowp), shp(HD, F32), shp(HD, lowp), shp(LANE, F32)],
        compiler_params=_params(("parallel",)),
        name="attn_prelude",
    )(z, z, z, z, z, z, *tabs, vec(q_norm), vec(k_norm), vec(idx_g), vec(idx_b))


def _ret_gamma_log(h):
    return math.log(1.0 - 2.0 ** (-5.0 - h))


def _retention_kernel(lg_ref, q_ref, k_ref, v_ref, g_ref, cos_ref, sin_ref, nrm_ref, o_ref, st_ref, s_scr, *, nc):
    c = pl.program_id(2)
    cr = q_ref.shape[0]
    lg = lg_ref[:, 0:1]

    @pl.when(c == 0)
    def _():
        s_scr[...] = jnp.zeros_like(s_scr)

    cos, sin = cos_ref[...], sin_ref[...]
    q = _rot(q_ref[...], cos, sin, RET_DK // 2)
    k = _rot(k_ref[...], cos, sin, RET_DK // 2) * (RET_DK ** -0.5)
    v = v_ref[...]
    ti = lax.broadcasted_iota(jnp.int32, (cr, cr), 0)
    si = lax.broadcasted_iota(jnp.int32, (cr, cr), 1)
    diff = (ti - si).astype(F32)
    intra = jnp.where(diff >= 0, jnp.exp(lg * jnp.maximum(diff, 0.0)), 0.0)
    tcol = lax.broadcasted_iota(jnp.int32, (cr, 1), 0).astype(F32)
    q_dec = jnp.exp(lg * (tcol + 1.0))
    k_dec = jnp.exp(lg * (cr - 1.0 - tcol))
    s_dec = jnp.exp(lg * cr)
    s = s_scr[...]
    a = _bdot(q, k, _NT) * intra
    o = _bdot(a, v) + _bdot(q * q_dec, s)
    s_new = s * s_dec + _bdot(k * k_dec, v, _TN)
    s_scr[...] = s_new
    y = o * lax.rsqrt(jnp.mean(o * o, axis=-1, keepdims=True) + EPS) * nrm_ref[...]
    o_ref[...] = (y * _silu(g_ref[...])).astype(o_ref.dtype)

    @pl.when(c == nc - 1)
    def _():
        st_ref[...] = s_new


def _retention(z, l, cos, sin, ret_norm, b, t):
    _, dst, _, _, _ = _segments()
    cr = min(t, 256)
    nc = t // cr
    lg = jnp.broadcast_to(
        jnp.asarray([_ret_gamma_log(h) for h in range(RET_HEADS)], F32)[:, None, None], (RET_HEADS, 1, LANE))

    def zspec(name):
        return pl.BlockSpec((cr, HD), lambda bi, h, c: (bi * nc + c, dst[name] // HD + h))

    tab = pl.BlockSpec((cr, LANE), lambda bi, h, c: (c, 0))
    return pl.pallas_call(
        functools.partial(_retention_kernel, nc=nc),
        grid=(b, RET_HEADS, nc),
        in_specs=[pl.BlockSpec((None, 1, LANE), lambda bi, h, c: (h, 0, 0)),
                  zspec('r_q'), zspec('r_k'), zspec('r_v'), zspec('r_g'), tab, tab,
                  pl.BlockSpec((None, 1, HD), lambda bi, h, c: (l, 0, h))],
        out_specs=[pl.BlockSpec((cr, HD), lambda bi, h, c: (bi * nc + c, h)),
                   pl.BlockSpec((None, None, RET_DK, RET_DV), lambda bi, h, c: (bi, h, 0, 0))],
        out_shape=[jax.ShapeDtypeStruct((z.shape[0], RET_HEADS * RET_DV), BF16),
                   jax.ShapeDtypeStruct((b, RET_HEADS, RET_DK, RET_DV), F32)],
        scratch_shapes=[pltpu.VMEM((RET_DK, RET_DV), F32)],
        compiler_params=_params(("parallel", "parallel", "arbitrary")),
        name="retention",
    )(lg, z, z, z, z, cos, sin, ret_norm.reshape(ret_norm.shape[0], 1, -1))


def _hgrn_gates(fa, lb):
    log_f = jnp.minimum(fa, 0.0) - jnp.log1p(jnp.exp(-jnp.abs(fa))) + jnp.log1p(lb * jnp.exp(-fa))
    hk = (1.0 - lb) * jax.nn.sigmoid(-fa)
    return log_f, hk


def _split3(x):
    hi = x.astype(BF16)
    r1 = x - hi.astype(F32)
    mid = r1.astype(BF16)
    lo = (r1 - mid.astype(F32)).astype(BF16)
    return hi, mid, lo


def _gla_state_step(q, k, v, b, st):
    b_last = b[GLA_CHUNK - 1:GLA_CHUNK, :]
    o = _bdot(q * jnp.exp(b), st, _NT)
    kd = k * jnp.exp(b_last - b)
    return o, st * jnp.exp(b_last) + _bdot(v, kd, _TN)


def _gla_intra_anchored(q, k, v, b):
    c = GLA_CHUNK
    nsub = c // GLA_SUB
    ti = lax.broadcasted_iota(jnp.int32, (c, c), 0)
    si = lax.broadcasted_iota(jnp.int32, (c, c), 1)
    anchors = [jnp.zeros((1, HD), F32)] + [b[i * GLA_SUB - 1:i * GLA_SUB, :] for i in range(1, nsub)]
    m_rows = jnp.concatenate([jnp.broadcast_to(m, (GLA_SUB, HD)) for m in anchors], axis=0)
    qs = q * jnp.exp(b - m_rows)
    a = jnp.zeros((c, c), F32)
    for i in range(nsub):
        ks = k * jnp.exp(jnp.minimum(anchors[i] - b, GLA_SAFE_SPAN))
        blk = (ti // GLA_SUB == i) & (si <= ti)
        a = a + jnp.where(blk, _bdot(qs, ks, _NT), 0.0)
    return _bdot(a, v)


def _gla_intra_pairwise(q, k, v, b):
    c = GLA_CHUNK
    nsub = c // GLA_SUB
    ti = lax.broadcasted_iota(jnp.int32, (c, c), 0)
    si = lax.broadcasted_iota(jnp.int32, (c, c), 1)
    a_off = jnp.zeros((c, c), F32)
    for i in range(1, nsub):
        m = b[i * GLA_SUB - 1:i * GLA_SUB, :]
        qs = q * jnp.exp(jnp.minimum(b - m, 0.0))
        ks = k * jnp.exp(jnp.minimum(m - b, 0.0))
        blk = (ti // GLA_SUB == i) & (si < i * GLA_SUB)
        a_off = a_off + jnp.where(blk, _bdot(qs, ks, _NT), 0.0)
    o = _bdot(a_off, v)
    rows = lax.broadcasted_iota(jnp.int32, (GLA_SUB, 1), 0)
    diag = []
    for i in range(nsub):
        sl = slice(i * GLA_SUB, (i + 1) * GLA_SUB)
        qi, ki, vi, bi = q[sl], k[sl], v[sl], b[sl]
        oi = jnp.zeros((GLA_SUB, HD), F32)
        for s in range(GLA_SUB):
            d = jnp.exp(jnp.minimum(bi - bi[s:s + 1], 0.0)) * qi * ki[s:s + 1]
            w = jnp.where(rows >= s, jnp.sum(d, axis=-1, keepdims=True), 0.0)
            oi = oi + w * vi[s:s + 1]
        diag.append(oi)
    return o + jnp.concatenate(diag, axis=0)


def _gla_kernel(f_ref, q_ref, i_ref, g_ref, lb_ref, nrm_ref, o_ref, st_ref, s_scr, *, nc, n_inner):
    c = pl.program_id(2)
    blk = f_ref.shape[0]

    @pl.when(c == 0)
    def _():
        s_scr[...] = jnp.zeros_like(s_scr)

    log_f, hk = _hgrn_gates(f_ref[...], lb_ref[...])
    hq = _silu(q_ref[...])
    v = i_ref[...]
    ti = lax.broadcasted_iota(jnp.int32, (blk, blk), 0)
    si = lax.broadcasted_iota(jnp.int32, (blk, blk), 1)
    tri = ((ti >= si) & (ti // GLA_CHUNK == si // GLA_CHUNK)).astype(BF16)
    hi, mid, lo = _split3(log_f)
    b = (jnp.dot(tri, hi, preferred_element_type=F32) + jnp.dot(tri, mid, preferred_element_type=F32)
         + jnp.dot(tri, lo, preferred_element_type=F32))
    sub_sum = jnp.sum(log_f.reshape(blk // GLA_SUB, GLA_SUB, HD), axis=1)
    anchored_ok = jnp.min(sub_sum) >= -GLA_SAFE_SPAN

    def run(intra):
        st = s_scr[...]
        outs = []
        for ci in range(n_inner):
            sl = slice(ci * GLA_CHUNK, (ci + 1) * GLA_CHUNK)
            o_inter, st = _gla_state_step(hq[sl], hk[sl], v[sl], b[sl], st)
            outs.append(o_inter + intra(hq[sl], hk[sl], v[sl], b[sl]))
        s_scr[...] = st
        o = jnp.concatenate(outs, axis=0)
        y = o * lax.rsqrt(jnp.mean(o * o, axis=-1, keepdims=True) + EPS) * nrm_ref[...]
        o_ref[...] = (y * jax.nn.sigmoid(g_ref[...])).astype(o_ref.dtype)

    lax.cond(anchored_ok, lambda: run(_gla_intra_anchored), lambda: run(_gla_intra_pairwise))

    @pl.when(c == nc - 1)
    def _():
        st_ref[...] = s_scr[...].T


def _gla(z, l, lb, hg_norm, b, t):
    _, dst, _, _, _ = _segments()
    blk = min(t, 256)
    assert blk % GLA_CHUNK == 0
    nc = t // blk

    def zspec(name):
        return pl.BlockSpec((blk, HD), lambda bi, h, c: (bi * nc + c, dst[name] // HD + h))

    vec = lambda: pl.BlockSpec((None, 1, HD), lambda bi, h, c: (l, 0, h))
    return pl.pallas_call(
        functools.partial(_gla_kernel, nc=nc, n_inner=blk // GLA_CHUNK),
        grid=(b, HG_HEADS, nc),
        in_specs=[zspec('h_f'), zspec('h_q'), zspec('h_i'), zspec('h_g'), vec(), vec()],
        out_specs=[pl.BlockSpec((blk, HD), lambda bi, h, c: (bi * nc + c, h)),
                   pl.BlockSpec((None, None, HG_DK, HG_DV), lambda bi, h, c: (bi, h, 0, 0))],
        out_shape=[jax.ShapeDtypeStruct((z.shape[0], HG_HEADS * HG_DV), BF16),
                   jax.ShapeDtypeStruct((b, HG_HEADS, HG_DK, HG_DV), F32)],
        scratch_shapes=[pltpu.VMEM((HG_DV, HG_DK), F32)],
        compiler_params=_params(("parallel", "parallel", "arbitrary")),
        name="hgrn2",
    )(z, z, z, z, lb.reshape(lb.shape[0], 1, -1), hg_norm.reshape(hg_norm.shape[0], 1, -1))


def _to_col(row):
    n = row.shape[1]
    eye = lax.broadcasted_iota(jnp.int32, (n, n), 0) == lax.broadcasted_iota(jnp.int32, (n, n), 1)
    return jnp.sum(jnp.where(eye, row, 0.0), axis=1, keepdims=True)


def _decode_rec_kernel(rq_ref, rk_ref, rv_ref, rg_ref, hf_ref, hq_ref, hi_ref, hg_ref,
                       cos_ref, sin_ref, rn_ref, hn_ref, lb_ref, sr_ref, sh_ref,
                       or_ref, oh_ref, nr_ref, nh_ref):
    cos, sin = cos_ref[...], sin_ref[...]

    def rms(o, g):
        return o * lax.rsqrt(jnp.mean(o * o, axis=-1, keepdims=True) + EPS) * g

    for h in range(RET_HEADS):
        sl = slice(h * HD, (h + 1) * HD)
        gamma = math.exp(_ret_gamma_log(h))
        q = _rot(rq_ref[:, sl], cos, sin, RET_DK // 2)
        k = _rot(rk_ref[:, sl], cos, sin, RET_DK // 2) * (RET_DK ** -0.5)
        v = rv_ref[:, sl]
        s = sr_ref[h]
        o = jnp.sum(q * k, axis=-1, keepdims=True) * v + jnp.sum(_to_col(q * gamma) * s, axis=0, keepdims=True)
        nr_ref[h] = s * gamma + _to_col(k) * v
        or_ref[:, sl] = rms(o, rn_ref[:, sl]) * _silu(rg_ref[:, sl])

    for h in range(HG_HEADS):
        sl = slice(h * HD, (h + 1) * HD)
        log_f, k = _hgrn_gates(hf_ref[:, sl], lb_ref[:, sl])
        q = _silu(hq_ref[:, sl])
        v = hi_ref[:, sl]
        s = sh_ref[h]
        eb = jnp.exp(log_f)
        o = jnp.sum(q * k, axis=-1, keepdims=True) * v + jnp.sum(_to_col(q * eb) * s, axis=0, keepdims=True)
        nh_ref[h] = s * _to_col(eb) + _to_col(k) * v
        oh_ref[:, sl] = rms(o, hn_ref[:, sl]) * jax.nn.sigmoid(hg_ref[:, sl])


def _decode_rec(z, l, cos, sin, ret_norm, hg_norm, lb, state_ret, state_hgrn):
    _, dst, padded, _, _ = _segments()
    nb = z.shape[0]

    z = z.reshape(nb, 1, z.shape[1])

    def zspec(name):
        w = padded[name]
        return pl.BlockSpec((None, 1, w), lambda bi: (bi, 0, dst[name] // w))

    one = lambda w: pl.BlockSpec((1, w), lambda bi: (0, 0))
    vec = lambda a: pl.BlockSpec((None, 1, a.shape[-1]), lambda bi: (l, 0, 0))
    st = lambda a: pl.BlockSpec((None, None) + a.shape[2:], lambda bi: (l, bi, 0, 0, 0))
    st_out = lambda a: pl.BlockSpec((None,) + a.shape[2:], lambda bi: (bi, 0, 0, 0))
    rw, hw = RET_HEADS * RET_DV, HG_HEADS * HG_DV
    r3 = lambda a: a.reshape(a.shape[0], 1, -1)
    return pl.pallas_call(
        _decode_rec_kernel,
        grid=(nb,),
        in_specs=[zspec('r_q'), zspec('r_k'), zspec('r_v'), zspec('r_g'),
                  zspec('h_f'), zspec('h_q'), zspec('h_i'), zspec('h_g'),
                  one(LANE), one(LANE), vec(ret_norm), vec(hg_norm), vec(lb), st(state_ret), st(state_hgrn)],
        out_specs=[pl.BlockSpec((None, 1, rw), lambda bi: (bi, 0, 0)),
                   pl.BlockSpec((None, 1, hw), lambda bi: (bi, 0, 0)),
                   st_out(state_ret), st_out(state_hgrn)],
        out_shape=[jax.ShapeDtypeStruct((nb, 1, rw), F32), jax.ShapeDtypeStruct((nb, 1, hw), F32),
                   jax.ShapeDtypeStruct(state_ret.shape[1:], F32), jax.ShapeDtypeStruct(state_hgrn.shape[1:], F32)],
        compiler_params=_params(("arbitrary",)),
        name="decode_recurrent",
    )(z, z, z, z, z, z, z, z, cos, sin, r3(ret_norm), r3(hg_norm), r3(lb), state_ret, state_hgrn)


def _sort_key(score):
    bits = lax.bitcast_convert_type(score + 0.0, jnp.int32)
    return bits ^ ((bits >> 31) & jnp.int32(0x7FFFFFFF))


def _count(mask):
    return jnp.sum(mask.astype(F32), axis=-1, keepdims=True)


def _nth_largest_key(count_ge, n_sel, shape):
    lo = jnp.where(count_ge(jnp.zeros(shape, jnp.int32)) >= n_sel, 0, INT_MIN).astype(jnp.int32)

    def body(i, lo):
        cand = lo | jnp.left_shift(jnp.int32(1), 30 - i)
        return jnp.where(count_ge(cand) >= n_sel, cand, lo)

    return lax.fori_loop(0, 31, body, lo)


def _tie_bound(count_eq_below, need, nbits, shape):
    def body(i, j):
        cand = j | jnp.left_shift(jnp.int32(1), nbits - 1 - i)
        return jnp.where(count_eq_below(cand) < need, cand, j)

    return lax.fori_loop(0, nbits, body, jnp.zeros(shape, jnp.int32))


def _dsa_prompt_body(length, aq_ref, iq_ref, iw_ref, k_ref, v_ref, ik_ref, o_ref, key_scr, sel_scr, n_sel):
    j = pl.program_id(1)
    tq = aq_ref.shape[0]
    ik = ik_ref[0:length, :]
    score = jnp.zeros((tq, length), F32)
    for h in range(IDX_HEADS):
        s = lax.dot_general(iq_ref[:, h * HD:(h + 1) * HD], ik, _NT, preferred_element_type=F32)
        score = score + jnp.maximum(s, 0.0) * iw_ref[:, h:h + 1]
    q_pos = j * tq + lax.broadcasted_iota(jnp.int32, (tq, 1), 0)
    col = lax.broadcasted_iota(jnp.int32, (tq, length), 1)
    visible = col <= q_pos
    key_scr[:, 0:length] = _sort_key(jnp.where(visible, score, NEG_BIG))

    tau = _nth_largest_key(lambda c: _count(key_scr[:, 0:length] >= c), n_sel, (tq, 1))
    key = key_scr[:, 0:length]
    gt = key > tau
    eq = key == tau
    need = n_sel - _count(gt)
    spare = jnp.max(_count(eq & visible) - need) > 0.0
    bound = lax.cond(
        spare,
        lambda: _tie_bound(lambda c: _count((key_scr[:, 0:length] == tau) & (col < c)), need,
                           max(1, length.bit_length()), (tq, 1)),
        lambda: jnp.full((tq, 1), length, jnp.int32))
    sel_scr[:, 0:length] = ((gt | (eq & (col <= bound))) & visible).astype(F32)

    group = ATT_HEADS // ATT_KV_HEADS
    scale = ATT_HEAD_DIM ** -0.5
    for n in range(ATT_KV_HEADS):
        kn = k_ref[0:length, n * HD:(n + 1) * HD]
        vn = v_ref[0:length, n * HD:(n + 1) * HD]
        for g in range(group):
            sl = slice((n * group + g) * HD, (n * group + g + 1) * HD)
            s = lax.dot_general(aq_ref[:, sl], kn, _NT, preferred_element_type=F32) * scale
            s = jnp.where(sel_scr[:, 0:length] > 0.0, s, NEG_BIG)
            m = jnp.max(s, axis=-1, keepdims=True)
            p = jnp.exp(s - m)
            o = _bdot(p, vn) / jnp.sum(p, axis=-1, keepdims=True)
            o_ref[:, sl] = o.astype(o_ref.dtype)


def _dsa_prompt_kernel(*refs, n_sel, lengths):
    j = pl.program_id(1)
    tq = refs[0].shape[0]
    prev = 0
    for length in lengths:
        @pl.when((j >= prev // tq) & (j < length // tq))
        def _(length=length):
            _dsa_prompt_body(length, *refs, n_sel)
        prev = length


def _dsa_prompt(aq, iq, iw, kb, vb, ikb, b, t, mt):
    tq = min(Q_BLOCK, t)
    nq = t // tq
    n_sel = min(TOPK_MAX, t // 4)
    step = min(t, 256)
    lengths = tuple(range(step, t + 1, step))
    assert t % step == 0 and step % tq == 0 and step >= n_sel
    qrow = lambda w: pl.BlockSpec((tq, w), lambda bi, j: (bi * nq + j, 0))
    krow = lambda w: pl.BlockSpec((t, w), lambda bi, j: (bi, 0))
    return pl.pallas_call(
        functools.partial(_dsa_prompt_kernel, n_sel=n_sel, lengths=lengths),
        grid=(b, nq),
        in_specs=[qrow(aq.shape[1]), qrow(iq.shape[1]), qrow(LANE),
                  krow(kb.shape[1]), krow(vb.shape[1]), krow(HD)],
        out_specs=qrow(aq.shape[1]),
        out_shape=jax.ShapeDtypeStruct((mt, aq.shape[1]), BF16),
        scratch_shapes=[pltpu.VMEM((tq, t), jnp.int32), pltpu.VMEM((tq, t), F32)],
        compiler_params=_params(("parallel", "arbitrary")),
        name="dsa_prompt",
    )(aq, iq, iw, kb, vb, ikb)


def _dec_score_kernel(pt_ref, iq_ref, iw_ref, ikn_ref, *refs):
    pages = refs[:PAGES_PER_STEP]
    sc_ref, new_ref = refs[PAGES_PER_STEP:]
    iq = iq_ref[...].astype(BF16)
    iw = iw_ref[...]
    ik = jnp.concatenate([p[...].astype(BF16) for p in pages], axis=0)
    s = lax.dot_general(iq, ik, _NT, preferred_element_type=F32)
    sc_ref[...] = jnp.sum(jnp.maximum(s, 0.0) * iw, axis=0, keepdims=True)

    @pl.when(pl.program_id(1) == 0)
    def _():
        sn = lax.dot_general(iq, jnp.broadcast_to(ikn_ref[...], (8, HD)).astype(BF16), _NT,
                             preferred_element_type=F32)[:, 0:1]
        new_ref[...] = jnp.broadcast_to(jnp.sum(jnp.maximum(sn, 0.0) * iw, axis=0, keepdims=True), (1, LANE))


def _dec_scores(page_table, iq, iw, ik_new, cache_ik, l):
    nb, n_pages = page_table.shape
    steps = n_pages // PAGES_PER_STEP
    ih = iq.shape[1]
    page = lambda r: pl.BlockSpec((None, None, PAGE_SIZE, HD),
                                  lambda bi, p, pt: (l, pt[bi, p * PAGES_PER_STEP + r], 0, 0))
    grid_spec = pltpu.PrefetchScalarGridSpec(
        num_scalar_prefetch=1,
        grid=(nb, steps),
        in_specs=[pl.BlockSpec((None, ih, HD), lambda bi, p, pt: (bi, 0, 0)),
                  pl.BlockSpec((None, ih, 1), lambda bi, p, pt: (bi, 0, 0)),
                  pl.BlockSpec((None, 1, HD), lambda bi, p, pt: (bi, 0, 0))]
                 + [page(r) for r in range(PAGES_PER_STEP)],
        out_specs=[pl.BlockSpec((None, 1, PAGES_PER_STEP * PAGE_SIZE), lambda bi, p, pt: (bi, 0, p)),
                   pl.BlockSpec((None, 1, LANE), lambda bi, p, pt: (bi, 0, 0))],
    )
    return pl.pallas_call(
        _dec_score_kernel,
        grid_spec=grid_spec,
        out_shape=[jax.ShapeDtypeStruct((nb, 1, n_pages * PAGE_SIZE), F32),
                   jax.ShapeDtypeStruct((nb, 1, LANE), F32)],
        compiler_params=_params(("parallel", "arbitrary")),
        name="dec_scores",
    )(page_table, iq, iw, ik_new, *([cache_ik] * PAGES_PER_STEP))


def _dec_select_kernel(sc_ref, new_ref, tau_ref, bnd_ref, seln_ref, *, n_sel, nbits):
    key = _sort_key(sc_ref[...])
    key_new = _sort_key(new_ref[...])[:, 0:1]
    nb, s_len = key.shape
    idx = lax.broadcasted_iota(jnp.int32, key.shape, 1)
    tau = _nth_largest_key(lambda c: _count(key >= c) + (key_new >= c).astype(F32), n_sel, (nb, 1))
    eq = key == tau
    need = n_sel - _count(key > tau) - (key_new > tau).astype(F32)
    n_eq = _count(eq)
    bound = lax.cond(
        jnp.max(n_eq - need) > 0.0,
        lambda: _tie_bound(lambda c: _count(eq & (idx < c)), need, nbits, (nb, 1)),
        lambda: jnp.full((nb, 1), s_len, jnp.int32))
    sel_new = (key_new > tau) | ((key_new == tau) & (n_eq < need))
    tau_ref[...] = jnp.broadcast_to(tau, tau_ref.shape)
    bnd_ref[...] = jnp.broadcast_to(bound, bnd_ref.shape)
    seln_ref[...] = jnp.broadcast_to(sel_new.astype(F32), seln_ref.shape)


def _dec_select(scores, score_new, n_sel):
    nb, s_len = scores.shape
    out = lambda dt: jax.ShapeDtypeStruct((nb, LANE), dt)
    return pl.pallas_call(
        functools.partial(_dec_select_kernel, n_sel=n_sel, nbits=s_len.bit_length()),
        out_shape=[out(jnp.int32), out(jnp.int32), out(F32)],
        compiler_params=pltpu.CompilerParams(vmem_limit_bytes=VMEM_LIMIT),
        name="dec_select",
    )(scores, score_new)


def _dec_attn_kernel(pt_ref, tau_ref, bnd_ref, seln_ref, sc_ref, q_ref, kn_ref, vn_ref, *refs, steps):
    kp = refs[:PAGES_PER_STEP]
    vp = refs[PAGES_PER_STEP:2 * PAGES_PER_STEP]
    o_ref, m_scr, l_scr, acc_scr = refs[2 * PAGES_PER_STEP:]
    p_id = pl.program_id(1)
    chunk = PAGES_PER_STEP * PAGE_SIZE
    scale = ATT_HEAD_DIM ** -0.5

    @pl.when(p_id == 0)
    def _():
        m_scr[...] = jnp.full_like(m_scr, NEG_BIG)
        l_scr[...] = jnp.zeros_like(l_scr)
        acc_scr[...] = jnp.zeros_like(acc_scr)

    tau = tau_ref[:, 0:1]
    bound = bnd_ref[:, 0:1]
    key_c = _sort_key(sc_ref[...])
    idx_c = p_id * chunk + lax.broadcasted_iota(jnp.int32, key_c.shape, 1)
    sel = (key_c > tau) | ((key_c == tau) & (idx_c <= bound))
    q = q_ref[...].astype(BF16)
    heads = lambda pages, n: jnp.concatenate(
        [r[pl.ds(n, PAGE_SIZE, stride=ATT_KV_HEADS), :].astype(BF16) for r in pages], axis=0)
    s = sum(lax.dot_general(q[:, n * HD:(n + 1) * HD], heads(kp, n), _NT, preferred_element_type=F32)
            for n in range(ATT_KV_HEADS)) * scale
    s = jnp.where(sel, s, NEG_BIG)
    m_old = m_scr[:, 0:1]
    m_new = jnp.maximum(m_old, jnp.max(s, axis=-1, keepdims=True))
    alpha = jnp.exp(m_old - m_new)
    p = jnp.where(sel, jnp.exp(s - m_new), 0.0)
    l_new = alpha * l_scr[:, 0:1] + jnp.sum(p, axis=-1, keepdims=True)
    pb = p.astype(BF16)
    acc_new = alpha * acc_scr[...] + jnp.concatenate(
        [jnp.dot(pb, heads(vp, n), preferred_element_type=F32) for n in range(ATT_KV_HEADS)], axis=1)
    m_scr[...] = jnp.broadcast_to(m_new, m_scr.shape)
    l_scr[...] = jnp.broadcast_to(l_new, l_scr.shape)
    acc_scr[...] = acc_new

    @pl.when(p_id == steps - 1)
    def _():
        sel_new = seln_ref[:, 0:1] > 0.0
        qf = q_ref[...]
        s_new = jnp.sum(qf * kn_ref[...], axis=-1, keepdims=True) * scale
        s_new = jnp.where(sel_new, s_new, NEG_BIG)
        m_fin = jnp.maximum(m_new, s_new)
        a2 = jnp.exp(m_new - m_fin)
        p_new = jnp.where(sel_new, jnp.exp(s_new - m_fin), 0.0)
        l_fin = a2 * l_new + p_new
        acc_fin = (a2 * acc_new + p_new * vn_ref[...]) / l_fin
        group = ATT_HEADS // ATT_KV_HEADS
        for h in range(ATT_HEADS):
            n = h // group
            o_ref[:, h * HD:(h + 1) * HD] = acc_fin[h:h + 1, n * HD:(n + 1) * HD]


def _dec_attn(page_table, scores, score_new, q_bd, k_new, v_new, cache_k, cache_v, l):
    nb, n_pages = page_table.shape
    steps = n_pages // PAGES_PER_STEP
    s_len = n_pages * PAGE_SIZE
    chunk = PAGES_PER_STEP * PAGE_SIZE
    n_sel = min(TOPK_MAX, (s_len + 1) // 4)
    tau, bound, sel_new = _dec_select(scores.reshape(nb, s_len), score_new.reshape(nb, LANE), n_sel)
    row = lambda a: a.reshape(nb, 1, LANE)
    row_spec = pl.BlockSpec((None, 1, LANE), lambda bi, p, pt: (bi, 0, 0))
    kvw = ATT_KV_HEADS * HD
    hp = q_bd.shape[1]
    page = lambda r: pl.BlockSpec((None, None, PAGE_SIZE * ATT_KV_HEADS, HD),
                                  lambda bi, p, pt: (l, pt[bi, p * PAGES_PER_STEP + r], 0, 0))
    grid_spec = pltpu.PrefetchScalarGridSpec(
        num_scalar_prefetch=1,
        grid=(nb, steps),
        in_specs=[row_spec, row_spec, row_spec,
                  pl.BlockSpec((None, 1, chunk), lambda bi, p, pt: (bi, 0, p)),
                  pl.BlockSpec((None, hp, kvw), lambda bi, p, pt: (bi, 0, 0)),
                  pl.BlockSpec((None, 1, kvw), lambda bi, p, pt: (bi, 0, 0)),
                  pl.BlockSpec((None, 1, kvw), lambda bi, p, pt: (bi, 0, 0))]
                 + [page(r) for r in range(PAGES_PER_STEP)] * 2,
        out_specs=pl.BlockSpec((None, 1, ATT_HEADS * HD), lambda bi, p, pt: (bi, 0, 0)),
        scratch_shapes=[pltpu.VMEM((hp, LANE), F32), pltpu.VMEM((hp, LANE), F32),
                        pltpu.VMEM((hp, kvw), F32)],
    )
    return pl.pallas_call(
        functools.partial(_dec_attn_kernel, steps=steps),
        grid_spec=grid_spec,
        out_shape=jax.ShapeDtypeStruct((nb, 1, ATT_HEADS * HD), F32),
        compiler_params=_params(("parallel", "arbitrary")),
        name="dec_attn",
    )(page_table, row(tau), row(bound), row(sel_new), scores, q_bd, k_new, v_new,
      *([cache_k] * PAGES_PER_STEP), *([cache_v] * PAGES_PER_STEP)).reshape(nb, ATT_HEADS * HD)


def _ffn(x, norm, w1, w3, w2, l, tm):
    h = _rmsnorm(x, norm, l, BF16)
    tm_up = 2 * tm if (x.shape[0] // tm) % 2 == 0 else tm
    g = _matmul(h, [w1, w3], l, tm=tm_up, epilogue='swiglu', out_dtype=BF16)
    return _matmul(g, [w2], l, tm=tm, epilogue='resid', res=x, scale=0.5)


def _token_tiles(n_tokens):
    n_tiles = max(1, n_tokens // 1024)
    tm = -(-n_tokens // (16 * n_tiles)) * 16
    return tm, n_tiles


def kernel(x_prompt, x_sample, state_ret, state_hgrn, cache_k, cache_v, cache_idx_k, page_table, ffn1_norm, ffn1_w1, ffn1_w3, ffn1_w2, mix_norm, w_in, ret_norm, q_norm, k_norm, idx_k_g, idx_k_b, hg_lb_raw, hg_norm, w_up_ret, w_up_att, w_up_hg, w_out, ffn2_norm, ffn2_w1, ffn2_w3, ffn2_w2):
    b, t, d = x_prompt.shape
    nb = x_sample.shape[0]
    depth = w_in.shape[0]
    kvw = ATT_KV_HEADS * ATT_HEAD_DIM
    group = ATT_HEADS // ATT_KV_HEADS

    lb_soft = jax.nn.softmax(hg_lb_raw.astype(F32), axis=0)
    lb_all = jnp.cumsum(lb_soft, axis=0) - lb_soft[0]
    w_in_p = _repack_w_in(w_in)
    cast = lambda w: w.astype(BF16)
    ffn1_w2, ffn2_w2 = cast(ffn1_w2), cast(ffn2_w2)
    w_up_ret, w_up_att, w_up_hg, w_out = cast(w_up_ret), cast(w_up_att), cast(w_up_hg), cast(w_out)

    pos_p = jnp.arange(t, dtype=jnp.int32)
    pos_s = jnp.full((nb,), PAST_LEN, jnp.int32)
    ret_f = 1.0 / (ROPE_THETA ** jnp.linspace(0.0, 1.0, RET_DK // 2, dtype=F32))
    att_f = ROPE_THETA ** (-jnp.arange(0, ATT_HEAD_DIM, 2, dtype=F32) / ATT_HEAD_DIM)
    idx_f = ROPE_THETA ** (-jnp.arange(0, IDX_ROPE_DIM, 2, dtype=F32) / IDX_ROPE_DIM)
    tabs_p = _rope_tables(pos_p, att_f, ATT_HEAD_DIM) + _rope_tables(pos_p, idx_f, IDX_ROPE_DIM)
    tabs_s = _rope_tables(pos_s, att_f, ATT_HEAD_DIM) + _rope_tables(pos_s, idx_f, IDX_ROPE_DIM)
    ret_tab_p = _rope_tables(pos_p, ret_f, RET_DK)
    ret_tab_s = _rope_tables(pos_s[:1], ret_f, RET_DK)

    ck = cache_k.reshape(cache_k.shape[:2] + (PAGE_SIZE * ATT_KV_HEADS, HD))
    cv = cache_v.reshape(cache_v.shape[:2] + (PAGE_SIZE * ATT_KV_HEADS, HD))

    mp = b * t
    tm, n_tiles = _token_tiles(mp + nb)
    mt = tm * n_tiles
    x = jnp.concatenate([x_prompt.reshape(mp, d), x_sample.reshape(nb, d), jnp.zeros((mt - mp - nb, d), F32)])

    def with_tail(a, rows):
        tail = jnp.concatenate([rows.astype(a.dtype), jnp.zeros((mt - mp - nb, a.shape[1]), a.dtype)])
        return lax.dynamic_update_slice(a, tail, (mp, 0))

    outs = {n: [] for n in ('rp', 'rs', 'hp', 'hs', 'kp', 'vp', 'ip', 'ks', 'vs', 'is')}
    for l in range(depth):
        x = _ffn(x, ffn1_norm, ffn1_w1, ffn1_w3, ffn1_w2, l, tm)
        h = _rmsnorm(x, mix_norm, l, BF16)
        z = _matmul(h, [w_in_p], l, tm=2 * tm if n_tiles % 2 == 0 else tm, single_lhs=False)

        aq, ak, av, akb, avb, iq, ik, ikb, iw = _prelude(z, mp, l, tabs_p, q_norm, k_norm, idx_k_g, idx_k_b, t)
        a_ret, r_p = _retention(z, l, ret_tab_p[0], ret_tab_p[1], ret_norm, b, t)
        a_hg, h_p = _gla(z, l, lb_all, hg_norm, b, t)
        a_att = _dsa_prompt(aq, iq, iw, akb, avb, ikb, b, t, mt)
        outs['rp'].append(r_p); outs['hp'].append(h_p)
        outs['kp'].append(ak.reshape(b, t, ATT_KV_HEADS, ATT_HEAD_DIM))
        outs['vp'].append(av.reshape(b, t, ATT_KV_HEADS, ATT_HEAD_DIM))
        outs['ip'].append(ik.reshape(b, t, IDX_DIM))

        zs = z[mp:mp + nb]
        aq, ak, av, _, _, iq, ik, _, iw = _prelude(zs, nb, l, tabs_s, q_norm, k_norm, idx_k_g, idx_k_b, 1)
        s_ret, s_hg, r_s, h_s = _decode_rec(zs, l, ret_tab_s[0], ret_tab_s[1], ret_norm, hg_norm, lb_all,
                                            state_ret, state_hgrn)
        scores, score_new = _dec_scores(page_table, iq.reshape(nb, IDX_HEADS, IDX_DIM),
                                        iw[:, :IDX_HEADS].reshape(nb, IDX_HEADS, 1),
                                        ik.reshape(nb, 1, IDX_DIM), cache_idx_k, l)
        qh = aq.reshape(nb, ATT_KV_HEADS, group, ATT_HEAD_DIM)
        q_bd = jnp.concatenate(
            [jnp.concatenate([qh[:, n] if m == n else jnp.zeros_like(qh[:, n]) for m in range(ATT_KV_HEADS)], axis=-1)
             for n in range(ATT_KV_HEADS)]
            + [jnp.zeros((nb, max(0, 16 - ATT_HEADS), kvw), aq.dtype)], axis=1)
        s_att = _dec_attn(page_table, scores, score_new, q_bd, ak.reshape(nb, 1, kvw), av.reshape(nb, 1, kvw),
                          ck, cv, l)
        outs['rs'].append(r_s); outs['hs'].append(h_s)
        outs['ks'].append(ak.reshape(nb, 1, ATT_KV_HEADS, ATT_HEAD_DIM))
        outs['vs'].append(av.reshape(nb, 1, ATT_KV_HEADS, ATT_HEAD_DIM))
        outs['is'].append(ik.reshape(nb, 1, IDX_DIM))

        merged = _merge(with_tail(a_ret, s_ret.reshape(nb, -1)), with_tail(a_att, s_att),
                        with_tail(a_hg, s_hg.reshape(nb, -1)), w_up_ret, w_up_att, w_up_hg, z, l, BF16, tm)
        x = _matmul(merged, [w_out], l, tm=tm, epilogue='resid', res=x, scale=1.0)
        x = _ffn(x, ffn2_norm, ffn2_w1, ffn2_w3, ffn2_w2, l, tm)

    st = lambda n: jnp.stack(outs[n])
    return (x[:mp].reshape(b, t, d), x[mp:mp + nb].reshape(nb, 1, d),
            st('rp').astype(state_ret.dtype), st('rs').astype(state_ret.dtype),
            st('hp').astype(state_hgrn.dtype), st('hs').astype(state_hgrn.dtype),
            st('kp'), st('vp'), st('ip'), st('ks'), st('vs'), st('is'))
```

```python
import functools
import math

import numpy as np
import jax
import jax.numpy as jnp
from jax import lax
from jax.experimental import pallas as pl
from jax.experimental.pallas import tpu as pltpu

D_MODEL = 4096
BATCH = 4
SEQ = 2048
DEPTH = 2
DEC_BATCH = 8
DEC_SEQ = 1
PAST_LEN = 16384
PAGE_SIZE = 128

RET_HEADS = 8
RET_DK = 128
RET_DV = 128
ATT_HEADS = 8
ATT_KV_HEADS = 2
ATT_HEAD_DIM = 128
IDX_HEADS = 16
IDX_DIM = 128
IDX_ROPE_DIM = 64
TOPK_MAX = 256
Q_BLOCK = 128
HG_HEADS = 8
HG_DK = 128
HG_DV = 128
D_FF = 11008
ROPE_THETA = 10000.0
EPS = 1e-6
NEG_BIG = -1e30

F32 = jnp.float32
BF16 = jnp.bfloat16
LANE = 128
HD = 128
INT_MIN = -(2 ** 31)
VMEM_LIMIT = 56 * 1024 * 1024
GLA_CHUNK = 64
GLA_SUB = 16
SEARCH_GROUPS = 4
GLA_SAFE_SPAN = 60.0
PAGES_PER_STEP = 16

_NT = (((1,), (1,)), ((), ()))
_TN = (((0,), (0,)), ((), ()))


def _params(sem):
    return pltpu.CompilerParams(dimension_semantics=sem, vmem_limit_bytes=VMEM_LIMIT)


def _bdot(a, b, dims=None):
    a = a.astype(BF16)
    b = b.astype(BF16)
    if dims is None:
        return jnp.dot(a, b, preferred_element_type=F32)
    return lax.dot_general(a, b, dims, preferred_element_type=F32)


def _silu(x):
    return x * jax.nn.sigmoid(x)


def _segments():
    ret_qk = RET_HEADS * RET_DK
    ret_w = RET_HEADS * RET_DV
    att_w = ATT_HEADS * ATT_HEAD_DIM
    kv_w = ATT_KV_HEADS * ATT_HEAD_DIM
    hg_k = HG_HEADS * HG_DK
    hg_w = HG_HEADS * HG_DV
    names = ['r_q', 'r_k', 'r_v', 'r_g', 'a_q', 'a_k', 'a_v', 'i_q', 'i_k', 'i_w',
             'h_f', 'h_q', 'h_i', 'h_g', 'g_ret', 'g_att', 'g_hg']
    widths = [ret_qk, ret_qk, ret_w, ret_w, att_w, kv_w, kv_w, IDX_HEADS * IDX_DIM, IDX_DIM, IDX_HEADS,
              hg_k, hg_k, hg_w, hg_w, D_MODEL, D_MODEL, D_MODEL]
    src = {}
    off = 0
    for n, w in zip(names, widths):
        src[n] = (off, w)
        off += w
    padded = {n: -(-w // LANE) * LANE for n, w in zip(names, widths)}
    order = ['g_ret', 'g_att', 'g_hg'] + sorted(
        [n for n in names if not n.startswith('g_')], key=lambda n: -padded[n])
    dst = {}
    off = 0
    for n in order:
        dst[n] = off
        off += padded[n]
    total = -(-off // 256) * 256
    for n in order:
        assert dst[n] % padded[n] == 0 or n.startswith('g_'), (n, dst[n], padded[n])
    return src, dst, padded, order, total


def _repack_kernel(start_ref, valid_ref, a_ref, o_ref):
    c = pl.program_id(1)
    a = a_ref[0]
    row = lax.broadcasted_iota(jnp.int32, a.shape, 0)
    o_ref[...] = jnp.where(row < valid_ref[c], a, 0.0).T.astype(o_ref.dtype)


def _repack_w_in(w_in):
    src, dst, padded, order, total = _segments()
    depth, kdim, n_in = w_in.shape
    start = np.zeros((total // LANE,), np.int32)
    valid = np.zeros((total // LANE,), np.int32)
    for n in order:
        s, w = src[n]
        for t in range(padded[n] // LANE):
            c = dst[n] // LANE + t
            start[c], valid[c] = s + t * LANE, min(LANE, w - t * LANE)
    assert (start + LANE <= n_in).all() and (start % 8 == 0).all()
    start //= 8
    grid_spec = pltpu.PrefetchScalarGridSpec(
        num_scalar_prefetch=2,
        grid=(depth, total // LANE),
        in_specs=[pl.BlockSpec((pl.Element(1), pl.Element(LANE), pl.Element(kdim)),
                               lambda l, c, start, valid: (l, start[c] * 8, 0))],
        out_specs=pl.BlockSpec((None, kdim, LANE), lambda l, c, start, valid: (l, 0, c)),
    )
    return pl.pallas_call(
        _repack_kernel,
        grid_spec=grid_spec,
        out_shape=jax.ShapeDtypeStruct((depth, kdim, total), BF16),
        compiler_params=_params(("parallel", "parallel")),
        name="repack_w_in",
    )(jnp.asarray(start), jnp.asarray(valid), jnp.swapaxes(w_in, 1, 2))


def _row_prep_kernel(x_ref, g_ref, xb_ref, ssq_ref):
    x = x_ref[...]
    xb_ref[...] = (x * g_ref[...]).astype(xb_ref.dtype)
    ssq_ref[...] = jnp.broadcast_to(jnp.sum(x * x, axis=-1, keepdims=True), ssq_ref.shape)


def _row_tile(m, cap):
    best = None
    for t in range(16, min(m, cap) + 1, 16):
        if m % t == 0:
            best = t
    return best if best is not None else m


def _row_prep(x, g_all, l):
    m, d = x.shape
    tr = _row_tile(m, 512)
    return pl.pallas_call(
        _row_prep_kernel,
        grid=(m // tr,),
        in_specs=[pl.BlockSpec((tr, d), lambda i: (i, 0)),
                  pl.BlockSpec((None, 1, d), lambda i: (l, 0, 0))],
        out_specs=[pl.BlockSpec((tr, d), lambda i: (i, 0)), pl.BlockSpec((tr, LANE), lambda i: (i, 0))],
        out_shape=[jax.ShapeDtypeStruct((m, d), BF16), jax.ShapeDtypeStruct((m, LANE), F32)],
        compiler_params=_params(("parallel",)),
        name="row_prep",
    )(x, g_all.reshape(g_all.shape[0], 1, d))


def _mm_kernel(*refs, n_w, epilogue, scale, row_norm, feed_norm):
    it = iter(refs)
    lhs_ref = next(it)
    w_refs = [next(it) for _ in range(n_w)]
    res_ref = next(it) if epilogue == 'resid' else None
    ssq_ref = next(it) if row_norm else None
    gain_ref = next(it) if feed_norm else None
    out_ref = next(it)
    xb_ref, ssq_out_ref = (next(it), next(it)) if feed_norm else (None, None)

    lhs = lhs_ref[...].astype(BF16)
    vals = [jnp.dot(lhs, w[...].astype(BF16), preferred_element_type=F32) for w in w_refs]
    if row_norm:
        r = lax.rsqrt(ssq_ref[:, 0:1] * (1.0 / lhs_ref.shape[1]) + EPS)
        vals = [v * r for v in vals]
    if epilogue == 'swiglu':
        out = _silu(vals[0]) * vals[1]
    elif epilogue == 'resid':
        out = res_ref[...] + scale * vals[0]
    else:
        out = vals[0]
    out_ref[...] = out.astype(out_ref.dtype)
    if feed_norm:
        xb_ref[...] = (out * gain_ref[...]).astype(xb_ref.dtype)
        part = jnp.broadcast_to(jnp.sum(out * out, axis=-1, keepdims=True), ssq_out_ref.shape)
        j = pl.program_id(1)

        @pl.when(j == 0)
        def _():
            ssq_out_ref[...] = part

        @pl.when(j > 0)
        def _():
            ssq_out_ref[...] += part


def _pick(n, cands):
    for c in cands:
        if n % c == 0:
            return c
    return n


def _matmul(lhs, ws, l, *, tm, epilogue='plain', res=None, scale=1.0, out_dtype=F32, tn=256, single_lhs=None,
            ssq=None, next_gain=None):
    m, kdim = lhs.shape
    n = ws[0].shape[-1]
    tn = _pick(n, (tn, 256, 128))
    assert m % tm == 0 and n % tn == 0
    if single_lhs is None:
        single_lhs = tm * kdim * lhs.dtype.itemsize > (12 << 20)
    lhs_mode = dict(pipeline_mode=pl.Buffered(1)) if single_lhs else {}
    in_specs = [pl.BlockSpec((tm, kdim), lambda i, j: (i, 0), **lhs_mode)]
    in_specs += [pl.BlockSpec((None, kdim, tn), lambda i, j: (l, 0, j)) for _ in ws]
    args = [lhs] + list(ws)
    tile = pl.BlockSpec((tm, tn), lambda i, j: (i, j))
    rows = pl.BlockSpec((tm, LANE), lambda i, j: (i, 0))
    if epilogue == 'resid':
        in_specs.append(tile)
        args.append(res)
    if ssq is not None:
        in_specs.append(rows)
        args.append(ssq)
    out_specs, out_shape = tile, jax.ShapeDtypeStruct((m, n), out_dtype)
    if next_gain is not None:
        gains, gl = next_gain
        in_specs.append(pl.BlockSpec((None, 1, tn), lambda i, j: (gl, 0, j)))
        args.append(gains.reshape(gains.shape[0], 1, n))
        out_specs = [tile, tile, rows]
        out_shape = [out_shape, jax.ShapeDtypeStruct((m, n), BF16), jax.ShapeDtypeStruct((m, LANE), F32)]
    return pl.pallas_call(
        functools.partial(_mm_kernel, n_w=len(ws), epilogue=epilogue, scale=scale,
                          row_norm=ssq is not None, feed_norm=next_gain is not None),
        grid=(m // tm, n // tn),
        in_specs=in_specs,
        out_specs=out_specs,
        out_shape=out_shape,
        compiler_params=_params(("parallel", "arbitrary")),
        name="mm_" + epilogue,
    )(*args)


def _merge_kernel(ar_ref, aa_ref, ah_ref, wr_ref, wa_ref, wh_ref, gr_ref, ga_ref, gh_ref, o_ref):
    u_r = _bdot(ar_ref[...], wr_ref[...])
    u_a = _bdot(aa_ref[...], wa_ref[...])
    u_h = _bdot(ah_ref[...], wh_ref[...])
    out = (jax.nn.sigmoid(gr_ref[...]) * u_r + jax.nn.sigmoid(ga_ref[...]) * u_a
           + jax.nn.sigmoid(gh_ref[...]) * u_h)
    o_ref[...] = out.astype(o_ref.dtype)


def _merge(a_ret, a_att, a_hg, w_r, w_a, w_h, z, l, out_dtype, tm):
    m = a_ret.shape[0]
    d = w_r.shape[-1]
    tn = _pick(d, (256, 128))
    nb = d // tn
    lhs_spec = lambda a: pl.BlockSpec((tm, a.shape[1]), lambda i, j: (i, 0))
    w_spec = lambda w: pl.BlockSpec((None, w.shape[1], tn), lambda i, j: (l, 0, j))
    gate_spec = lambda g: pl.BlockSpec((tm, tn), lambda i, j: (i, g * nb + j))
    return pl.pallas_call(
        _merge_kernel,
        grid=(m // tm, nb),
        in_specs=[lhs_spec(a_ret), lhs_spec(a_att), lhs_spec(a_hg), w_spec(w_r), w_spec(w_a), w_spec(w_h),
                  gate_spec(0), gate_spec(1), gate_spec(2)],
        out_specs=pl.BlockSpec((tm, tn), lambda i, j: (i, j)),
        out_shape=jax.ShapeDtypeStruct((m, d), out_dtype),
        compiler_params=_params(("parallel", "parallel")),
        name="merge",
    )(a_ret, a_att, a_hg, w_r, w_a, w_h, z, z, z)


def _rot(x, cos, sin, half):
    up = pltpu.roll(x, LANE - half, 1)
    dn = pltpu.roll(x, half, 1)
    lane = lax.broadcasted_iota(jnp.int32, x.shape, 1)
    return x * cos + jnp.where(lane < half, up, dn) * sin


def _rope_tables(pos, freqs, width):
    ang = pos.astype(F32)[:, None] * freqs[None, :]
    cos, sin = jnp.cos(ang), jnp.sin(ang)
    t = pos.shape[0]
    pad_c = jnp.ones((t, LANE - width), F32)
    pad_s = jnp.zeros((t, LANE - width), F32)
    return (jnp.concatenate([cos, cos, pad_c], axis=1), jnp.concatenate([-sin, sin, pad_s], axis=1))


def _prelude_kernel(zq_ref, zk_ref, zv_ref, ziq_ref, zik_ref, ziw_ref, ca_ref, sa_ref, ci_ref, si_ref,
                    qn_ref, kn_ref, ig_ref, ib_ref,
                    aq_ref, ak_ref, av_ref, akb_ref, avb_ref, iq_ref, ik_ref, ikb_ref, iw_ref):
    ca, sa, ci, si = ca_ref[...], sa_ref[...], ci_ref[...], si_ref[...]

    def head_norm(x, g):
        return x * lax.rsqrt(jnp.mean(x * x, axis=-1, keepdims=True) + EPS) * g

    for h in range(ATT_HEADS):
        sl = slice(h * HD, (h + 1) * HD)
        aq_ref[:, sl] = _rot(head_norm(zq_ref[:, sl], qn_ref[...]), ca, sa, HD // 2).astype(aq_ref.dtype)
    for h in range(ATT_KV_HEADS):
        sl = slice(h * HD, (h + 1) * HD)
        k = _rot(head_norm(zk_ref[:, sl], kn_ref[...]), ca, sa, HD // 2)
        ak_ref[:, sl] = k
        akb_ref[:, sl] = k.astype(akb_ref.dtype)
    v = zv_ref[...]
    av_ref[...] = v
    avb_ref[...] = v.astype(avb_ref.dtype)
    for h in range(IDX_HEADS):
        sl = slice(h * HD, (h + 1) * HD)
        iq_ref[:, sl] = (_rot(ziq_ref[:, sl], ci, si, IDX_ROPE_DIM // 2) * (IDX_DIM ** -0.5)).astype(iq_ref.dtype)
    x = zik_ref[...]
    mu = jnp.mean(x, axis=-1, keepdims=True)
    var = jnp.mean(jnp.square(x - mu), axis=-1, keepdims=True)
    ik = _rot((x - mu) * lax.rsqrt(var + EPS) * ig_ref[...] + ib_ref[...], ci, si, IDX_ROPE_DIM // 2)
    ik_ref[...] = ik
    ikb_ref[...] = ik.astype(ikb_ref.dtype)
    iw_ref[...] = ziw_ref[...] * (IDX_HEADS ** -0.5)


def _prelude(z, m, l, tabs, q_norm, k_norm, idx_g, idx_b, t_len):
    _, dst, padded, _, _ = _segments()
    tr = min(m, 256)
    nt = t_len // tr if t_len >= tr else 1
    lowp = BF16 if tr >= 16 else F32
    aw, kvw, iw = padded['a_q'], padded['a_k'], padded['i_q']

    def zspec(name):
        w = padded[name]
        return pl.BlockSpec((tr, w), lambda i: (i, dst[name] // w))

    tab_spec = pl.BlockSpec((tr, LANE), lambda i: (i % nt, 0))
    vec_spec = pl.BlockSpec((None, 1, HD), lambda i: (l, 0, 0))
    row = lambda w: pl.BlockSpec((tr, w), lambda i: (i, 0))
    shp = lambda w, dt: jax.ShapeDtypeStruct((m, w), dt)
    vec = lambda a: a.reshape(a.shape[0], 1, HD)
    return pl.pallas_call(
        _prelude_kernel,
        grid=(m // tr,),
        in_specs=[zspec('a_q'), zspec('a_k'), zspec('a_v'), zspec('i_q'), zspec('i_k'), zspec('i_w'),
                  tab_spec, tab_spec, tab_spec, tab_spec, vec_spec, vec_spec, vec_spec, vec_spec],
        out_specs=[row(aw), row(kvw), row(kvw), row(kvw), row(kvw), row(iw), row(HD), row(HD), row(LANE)],
        out_shape=[shp(aw, lowp), shp(kvw, F32), shp(kvw, F32), shp(kvw, lowp), shp(kvw, lowp),
                   shp(iw, lowp), shp(HD, F32), shp(HD, lowp), shp(LANE, F32)],
        compiler_params=_params(("parallel",)),
        name="attn_prelude",
    )(z, z, z, z, z, z, *tabs, vec(q_norm), vec(k_norm), vec(idx_g), vec(idx_b))


def _ret_gamma_log(h):
    return math.log(1.0 - 2.0 ** (-5.0 - h))


def _retention_kernel(lg_ref, q_ref, k_ref, v_ref, g_ref, cos_ref, sin_ref, nrm_ref, o_ref, st_ref, s_scr, *, nc):
    c = pl.program_id(2)
    cr = q_ref.shape[0]
    lg = lg_ref[:, 0:1]

    @pl.when(c == 0)
    def _():
        s_scr[...] = jnp.zeros_like(s_scr)

    cos, sin = cos_ref[...], sin_ref[...]
    q = _rot(q_ref[...], cos, sin, RET_DK // 2)
    k = _rot(k_ref[...], cos, sin, RET_DK // 2) * (RET_DK ** -0.5)
    v = v_ref[...]
    ti = lax.broadcasted_iota(jnp.int32, (cr, cr), 0)
    si = lax.broadcasted_iota(jnp.int32, (cr, cr), 1)
    diff = (ti - si).astype(F32)
    intra = jnp.where(diff >= 0, jnp.exp(lg * jnp.maximum(diff, 0.0)), 0.0)
    tcol = lax.broadcasted_iota(jnp.int32, (cr, 1), 0).astype(F32)
    q_dec = jnp.exp(lg * (tcol + 1.0))
    k_dec = jnp.exp(lg * (cr - 1.0 - tcol))
    s_dec = jnp.exp(lg * cr)
    s = s_scr[...]
    a = _bdot(q, k, _NT) * intra
    o = _bdot(a, v) + _bdot(q * q_dec, s)
    s_new = s * s_dec + _bdot(k * k_dec, v, _TN)
    s_scr[...] = s_new
    y = o * lax.rsqrt(jnp.mean(o * o, axis=-1, keepdims=True) + EPS) * nrm_ref[...]
    o_ref[...] = (y * _silu(g_ref[...])).astype(o_ref.dtype)

    @pl.when(c == nc - 1)
    def _():
        st_ref[...] = s_new


def _retention(z, l, cos, sin, ret_norm, b, t):
    _, dst, _, _, _ = _segments()
    cr = min(t, 256)
    nc = t // cr
    lg = jnp.broadcast_to(
        jnp.asarray([_ret_gamma_log(h) for h in range(RET_HEADS)], F32)[:, None, None], (RET_HEADS, 1, LANE))

    def zspec(name):
        return pl.BlockSpec((cr, HD), lambda bi, h, c: (bi * nc + c, dst[name] // HD + h))

    tab = pl.BlockSpec((cr, LANE), lambda bi, h, c: (c, 0))
    return pl.pallas_call(
        functools.partial(_retention_kernel, nc=nc),
        grid=(b, RET_HEADS, nc),
        in_specs=[pl.BlockSpec((None, 1, LANE), lambda bi, h, c: (h, 0, 0)),
                  zspec('r_q'), zspec('r_k'), zspec('r_v'), zspec('r_g'), tab, tab,
                  pl.BlockSpec((None, 1, HD), lambda bi, h, c: (l, 0, h))],
        out_specs=[pl.BlockSpec((cr, HD), lambda bi, h, c: (bi * nc + c, h)),
                   pl.BlockSpec((None, None, RET_DK, RET_DV), lambda bi, h, c: (bi, h, 0, 0))],
        out_shape=[jax.ShapeDtypeStruct((z.shape[0], RET_HEADS * RET_DV), BF16),
                   jax.ShapeDtypeStruct((b, RET_HEADS, RET_DK, RET_DV), F32)],
        scratch_shapes=[pltpu.VMEM((RET_DK, RET_DV), F32)],
        compiler_params=_params(("parallel", "parallel", "arbitrary")),
        name="retention",
    )(lg, z, z, z, z, cos, sin, ret_norm.reshape(ret_norm.shape[0], 1, -1))


def _hgrn_gates(fa, lb):
    log_f = jnp.minimum(fa, 0.0) - jnp.log1p(jnp.exp(-jnp.abs(fa))) + jnp.log1p(lb * jnp.exp(-fa))
    hk = (1.0 - lb) * jax.nn.sigmoid(-fa)
    return log_f, hk


def _split3(x):
    hi = x.astype(BF16)
    r1 = x - hi.astype(F32)
    mid = r1.astype(BF16)
    lo = (r1 - mid.astype(F32)).astype(BF16)
    return hi, mid, lo


def _gla_state_step(q, k, v, b, st):
    b_last = b[GLA_CHUNK - 1:GLA_CHUNK, :]
    o = _bdot(q * jnp.exp(b), st, _NT)
    kd = k * jnp.exp(b_last - b)
    return o, st * jnp.exp(b_last) + _bdot(v, kd, _TN)


def _gla_intra_anchored(q, k, v, b):
    c = GLA_CHUNK
    nsub = c // GLA_SUB
    ti = lax.broadcasted_iota(jnp.int32, (c, c), 0)
    si = lax.broadcasted_iota(jnp.int32, (c, c), 1)
    anchors = [jnp.zeros((1, HD), F32)] + [b[i * GLA_SUB - 1:i * GLA_SUB, :] for i in range(1, nsub)]
    m_rows = jnp.concatenate([jnp.broadcast_to(m, (GLA_SUB, HD)) for m in anchors], axis=0)
    qs = q * jnp.exp(b - m_rows)
    a = jnp.zeros((c, c), F32)
    for i in range(nsub):
        ks = k * jnp.exp(jnp.minimum(anchors[i] - b, GLA_SAFE_SPAN))
        blk = (ti // GLA_SUB == i) & (si <= ti)
        a = a + jnp.where(blk, _bdot(qs, ks, _NT), 0.0)
    return _bdot(a, v)


def _gla_intra_pairwise(q, k, v, b):
    c = GLA_CHUNK
    nsub = c // GLA_SUB
    ti = lax.broadcasted_iota(jnp.int32, (c, c), 0)
    si = lax.broadcasted_iota(jnp.int32, (c, c), 1)
    a_off = jnp.zeros((c, c), F32)
    for i in range(1, nsub):
        m = b[i * GLA_SUB - 1:i * GLA_SUB, :]
        qs = q * jnp.exp(jnp.minimum(b - m, 0.0))
        ks = k * jnp.exp(jnp.minimum(m - b, 0.0))
        blk = (ti // GLA_SUB == i) & (si < i * GLA_SUB)
        a_off = a_off + jnp.where(blk, _bdot(qs, ks, _NT), 0.0)
    o = _bdot(a_off, v)
    rows = lax.broadcasted_iota(jnp.int32, (GLA_SUB, 1), 0)
    diag = []
    for i in range(nsub):
        sl = slice(i * GLA_SUB, (i + 1) * GLA_SUB)
        qi, ki, vi, bi = q[sl], k[sl], v[sl], b[sl]
        oi = jnp.zeros((GLA_SUB, HD), F32)
        for s in range(GLA_SUB):
            d = jnp.exp(jnp.minimum(bi - bi[s:s + 1], 0.0)) * qi * ki[s:s + 1]
            w = jnp.where(rows >= s, jnp.sum(d, axis=-1, keepdims=True), 0.0)
            oi = oi + w * vi[s:s + 1]
        diag.append(oi)
    return o + jnp.concatenate(diag, axis=0)


def _gla_kernel(f_ref, q_ref, i_ref, g_ref, lb_ref, nrm_ref, o_ref, st_ref, s_scr, *, nc, n_inner):
    c = pl.program_id(2)
    blk = f_ref.shape[0]

    @pl.when(c == 0)
    def _():
        s_scr[...] = jnp.zeros_like(s_scr)

    log_f, hk = _hgrn_gates(f_ref[...], lb_ref[...])
    hq = _silu(q_ref[...])
    v = i_ref[...]
    ti = lax.broadcasted_iota(jnp.int32, (blk, blk), 0)
    si = lax.broadcasted_iota(jnp.int32, (blk, blk), 1)
    tri = ((ti >= si) & (ti // GLA_CHUNK == si // GLA_CHUNK)).astype(BF16)
    hi, mid, lo = _split3(log_f)
    b = (jnp.dot(tri, hi, preferred_element_type=F32) + jnp.dot(tri, mid, preferred_element_type=F32)
         + jnp.dot(tri, lo, preferred_element_type=F32))
    sub_sum = jnp.sum(log_f.reshape(blk // GLA_SUB, GLA_SUB, HD), axis=1)
    anchored_ok = jnp.min(sub_sum) >= -GLA_SAFE_SPAN

    def run(intra):
        st = s_scr[...]
        outs = []
        for ci in range(n_inner):
            sl = slice(ci * GLA_CHUNK, (ci + 1) * GLA_CHUNK)
            o_inter, st = _gla_state_step(hq[sl], hk[sl], v[sl], b[sl], st)
            outs.append(o_inter + intra(hq[sl], hk[sl], v[sl], b[sl]))
        s_scr[...] = st
        o = jnp.concatenate(outs, axis=0)
        y = o * lax.rsqrt(jnp.mean(o * o, axis=-1, keepdims=True) + EPS) * nrm_ref[...]
        o_ref[...] = (y * jax.nn.sigmoid(g_ref[...])).astype(o_ref.dtype)

    lax.cond(anchored_ok, lambda: run(_gla_intra_anchored), lambda: run(_gla_intra_pairwise))

    @pl.when(c == nc - 1)
    def _():
        st_ref[...] = s_scr[...].T


def _gla(z, l, lb, hg_norm, b, t):
    _, dst, _, _, _ = _segments()
    blk = min(t, 256)
    assert blk % GLA_CHUNK == 0
    nc = t // blk

    def zspec(name):
        return pl.BlockSpec((blk, HD), lambda bi, h, c: (bi * nc + c, dst[name] // HD + h))

    vec = lambda: pl.BlockSpec((None, 1, HD), lambda bi, h, c: (l, 0, h))
    return pl.pallas_call(
        functools.partial(_gla_kernel, nc=nc, n_inner=blk // GLA_CHUNK),
        grid=(b, HG_HEADS, nc),
        in_specs=[zspec('h_f'), zspec('h_q'), zspec('h_i'), zspec('h_g'), vec(), vec()],
        out_specs=[pl.BlockSpec((blk, HD), lambda bi, h, c: (bi * nc + c, h)),
                   pl.BlockSpec((None, None, HG_DK, HG_DV), lambda bi, h, c: (bi, h, 0, 0))],
        out_shape=[jax.ShapeDtypeStruct((z.shape[0], HG_HEADS * HG_DV), BF16),
                   jax.ShapeDtypeStruct((b, HG_HEADS, HG_DK, HG_DV), F32)],
        scratch_shapes=[pltpu.VMEM((HG_DV, HG_DK), F32)],
        compiler_params=_params(("parallel", "parallel", "arbitrary")),
        name="hgrn2",
    )(z, z, z, z, lb.reshape(lb.shape[0], 1, -1), hg_norm.reshape(hg_norm.shape[0], 1, -1))


def _to_col(row):
    n = row.shape[1]
    eye = lax.broadcasted_iota(jnp.int32, (n, n), 0) == lax.broadcasted_iota(jnp.int32, (n, n), 1)
    return jnp.sum(jnp.where(eye, row, 0.0), axis=1, keepdims=True)


def _decode_rec_kernel(rq_ref, rk_ref, rv_ref, rg_ref, hf_ref, hq_ref, hi_ref, hg_ref,
                       cos_ref, sin_ref, rn_ref, hn_ref, lb_ref, sr_ref, sh_ref,
                       or_ref, oh_ref, nr_ref, nh_ref):
    cos, sin = cos_ref[...], sin_ref[...]

    def rms(o, g):
        return o * lax.rsqrt(jnp.mean(o * o, axis=-1, keepdims=True) + EPS) * g

    for h in range(RET_HEADS):
        sl = slice(h * HD, (h + 1) * HD)
        gamma = math.exp(_ret_gamma_log(h))
        q = _rot(rq_ref[:, sl], cos, sin, RET_DK // 2)
        k = _rot(rk_ref[:, sl], cos, sin, RET_DK // 2) * (RET_DK ** -0.5)
        v = rv_ref[:, sl]
        s = sr_ref[h]
        o = jnp.sum(q * k, axis=-1, keepdims=True) * v + jnp.sum(_to_col(q * gamma) * s, axis=0, keepdims=True)
        nr_ref[h] = s * gamma + _to_col(k) * v
        or_ref[:, sl] = rms(o, rn_ref[:, sl]) * _silu(rg_ref[:, sl])

    for h in range(HG_HEADS):
        sl = slice(h * HD, (h + 1) * HD)
        log_f, k = _hgrn_gates(hf_ref[:, sl], lb_ref[:, sl])
        q = _silu(hq_ref[:, sl])
        v = hi_ref[:, sl]
        s = sh_ref[h]
        eb = jnp.exp(log_f)
        o = jnp.sum(q * k, axis=-1, keepdims=True) * v + jnp.sum(_to_col(q * eb) * s, axis=0, keepdims=True)
        nh_ref[h] = s * _to_col(eb) + _to_col(k) * v
        oh_ref[:, sl] = rms(o, hn_ref[:, sl]) * jax.nn.sigmoid(hg_ref[:, sl])


def _decode_rec(z, l, cos, sin, ret_norm, hg_norm, lb, state_ret, state_hgrn):
    _, dst, padded, _, _ = _segments()
    nb = z.shape[0]

    z = z.reshape(nb, 1, z.shape[1])

    def zspec(name):
        w = padded[name]
        return pl.BlockSpec((None, 1, w), lambda bi: (bi, 0, dst[name] // w))

    one = lambda w: pl.BlockSpec((1, w), lambda bi: (0, 0))
    vec = lambda a: pl.BlockSpec((None, 1, a.shape[-1]), lambda bi: (l, 0, 0))
    st = lambda a: pl.BlockSpec((None, None) + a.shape[2:], lambda bi: (l, bi, 0, 0, 0))
    st_out = lambda a: pl.BlockSpec((None,) + a.shape[2:], lambda bi: (bi, 0, 0, 0))
    rw, hw = RET_HEADS * RET_DV, HG_HEADS * HG_DV
    r3 = lambda a: a.reshape(a.shape[0], 1, -1)
    return pl.pallas_call(
        _decode_rec_kernel,
        grid=(nb,),
        in_specs=[zspec('r_q'), zspec('r_k'), zspec('r_v'), zspec('r_g'),
                  zspec('h_f'), zspec('h_q'), zspec('h_i'), zspec('h_g'),
                  one(LANE), one(LANE), vec(ret_norm), vec(hg_norm), vec(lb), st(state_ret), st(state_hgrn)],
        out_specs=[pl.BlockSpec((None, 1, rw), lambda bi: (bi, 0, 0)),
                   pl.BlockSpec((None, 1, hw), lambda bi: (bi, 0, 0)),
                   st_out(state_ret), st_out(state_hgrn)],
        out_shape=[jax.ShapeDtypeStruct((nb, 1, rw), F32), jax.ShapeDtypeStruct((nb, 1, hw), F32),
                   jax.ShapeDtypeStruct(state_ret.shape[1:], F32), jax.ShapeDtypeStruct(state_hgrn.shape[1:], F32)],
        compiler_params=_params(("arbitrary",)),
        name="decode_recurrent",
    )(z, z, z, z, z, z, z, z, cos, sin, r3(ret_norm), r3(hg_norm), r3(lb), state_ret, state_hgrn)


def _sort_key(score):
    bits = lax.bitcast_convert_type(score + 0.0, jnp.int32)
    return bits ^ ((bits >> 31) & jnp.int32(0x7FFFFFFF))


def _count(mask):
    return jnp.sum(mask.astype(F32), axis=-1, keepdims=True)


def _nth_largest_key(count_ge, n_sel, shape):
    return _nth_largest_keys([count_ge], n_sel, shape)[0]


def _nth_largest_keys(count_fns, n_sel, shape):
    zero = jnp.zeros(shape, jnp.int32)
    los = tuple(jnp.where(f(zero) >= n_sel, 0, INT_MIN).astype(jnp.int32) for f in count_fns)

    def body(i, los):
        bit = jnp.left_shift(jnp.int32(1), 30 - i)
        return tuple(jnp.where(f(lo | bit) >= n_sel, lo | bit, lo) for f, lo in zip(count_fns, los))

    return lax.fori_loop(0, 31, body, los)


def _tie_bound(count_eq_below, need, nbits, shape):
    def body(i, j):
        cand = j | jnp.left_shift(jnp.int32(1), nbits - 1 - i)
        return jnp.where(count_eq_below(cand) < need, cand, j)

    return lax.fori_loop(0, nbits, body, jnp.zeros(shape, jnp.int32))


def _dsa_prompt_body(length, aq_ref, iq_ref, iw_ref, k_ref, v_ref, ik_ref, o_ref, key_scr, sel_scr, n_sel):
    j = pl.program_id(1)
    tq = aq_ref.shape[0]
    ik = ik_ref[0:length, :]
    score = jnp.zeros((tq, length), F32)
    for h in range(IDX_HEADS):
        s = lax.dot_general(iq_ref[:, h * HD:(h + 1) * HD], ik, _NT, preferred_element_type=F32)
        score = score + jnp.maximum(s, 0.0) * iw_ref[:, h:h + 1]
    q_pos = j * tq + lax.broadcasted_iota(jnp.int32, (tq, 1), 0)
    col = lax.broadcasted_iota(jnp.int32, (tq, length), 1)
    visible = col <= q_pos
    key_scr[:, 0:length] = _sort_key(jnp.where(visible, score, NEG_BIG))

    rows = tq // SEARCH_GROUPS
    group_count = lambda r: (lambda c: _count(key_scr[r * rows:(r + 1) * rows, 0:length] >= c))
    tau = jnp.concatenate(
        _nth_largest_keys([group_count(r) for r in range(SEARCH_GROUPS)], n_sel, (rows, 1)), axis=0)
    key = key_scr[:, 0:length]
    gt = key > tau
    eq = key == tau
    need = n_sel - _count(gt)
    spare = jnp.max(_count(eq & visible) - need) > 0.0
    bound = lax.cond(
        spare,
        lambda: _tie_bound(lambda c: _count((key_scr[:, 0:length] == tau) & (col < c)), need,
                           max(1, length.bit_length()), (tq, 1)),
        lambda: jnp.full((tq, 1), length, jnp.int32))
    sel_scr[:, 0:length] = ((gt | (eq & (col <= bound))) & visible).astype(F32)

    group = ATT_HEADS // ATT_KV_HEADS
    scale = ATT_HEAD_DIM ** -0.5
    for n in range(ATT_KV_HEADS):
        kn = k_ref[0:length, n * HD:(n + 1) * HD]
        vn = v_ref[0:length, n * HD:(n + 1) * HD]
        for g in range(group):
            sl = slice((n * group + g) * HD, (n * group + g + 1) * HD)
            s = lax.dot_general(aq_ref[:, sl], kn, _NT, preferred_element_type=F32) * scale
            s = jnp.where(sel_scr[:, 0:length] > 0.0, s, NEG_BIG)
            m = jnp.max(s, axis=-1, keepdims=True)
            p = jnp.exp(s - m)
            o = _bdot(p, vn) / jnp.sum(p, axis=-1, keepdims=True)
            o_ref[:, sl] = o.astype(o_ref.dtype)


def _dsa_prompt_kernel(*refs, n_sel, lengths):
    j = pl.program_id(1)
    tq = refs[0].shape[0]
    prev = 0
    for length in lengths:
        @pl.when((j >= prev // tq) & (j < length // tq))
        def _(length=length):
            _dsa_prompt_body(length, *refs, n_sel)
        prev = length


def _dsa_prompt(aq, iq, iw, kb, vb, ikb, b, t, mt):
    tq = min(Q_BLOCK, t)
    nq = t // tq
    n_sel = min(TOPK_MAX, t // 4)
    step = min(t, 512)
    lengths = tuple(range(step, t + 1, step))
    assert t % step == 0 and step % tq == 0 and step >= n_sel
    qrow = lambda w: pl.BlockSpec((tq, w), lambda bi, j: (bi * nq + j, 0))
    krow = lambda w: pl.BlockSpec((t, w), lambda bi, j: (bi, 0))
    return pl.pallas_call(
        functools.partial(_dsa_prompt_kernel, n_sel=n_sel, lengths=lengths),
        grid=(b, nq),
        in_specs=[qrow(aq.shape[1]), qrow(iq.shape[1]), qrow(LANE),
                  krow(kb.shape[1]), krow(vb.shape[1]), krow(HD)],
        out_specs=qrow(aq.shape[1]),
        out_shape=jax.ShapeDtypeStruct((mt, aq.shape[1]), BF16),
        scratch_shapes=[pltpu.VMEM((tq, t), jnp.int32), pltpu.VMEM((tq, t), F32)],
        compiler_params=_params(("parallel", "arbitrary")),
        name="dsa_prompt",
    )(aq, iq, iw, kb, vb, ikb)


def _dec_score_kernel(pt_ref, iq_ref, iw_ref, ikn_ref, *refs):
    pages = refs[:PAGES_PER_STEP]
    sc_ref, new_ref = refs[PAGES_PER_STEP:]
    iq = iq_ref[...].astype(BF16)
    iw = iw_ref[...]
    ik = jnp.concatenate([p[...].astype(BF16) for p in pages], axis=0)
    s = lax.dot_general(iq, ik, _NT, preferred_element_type=F32)
    sc_ref[...] = jnp.sum(jnp.maximum(s, 0.0) * iw, axis=0, keepdims=True)

    @pl.when(pl.program_id(1) == 0)
    def _():
        sn = lax.dot_general(iq, jnp.broadcast_to(ikn_ref[...], (8, HD)).astype(BF16), _NT,
                             preferred_element_type=F32)[:, 0:1]
        new_ref[...] = jnp.broadcast_to(jnp.sum(jnp.maximum(sn, 0.0) * iw, axis=0, keepdims=True), (1, LANE))


def _dec_scores(page_table, iq, iw, ik_new, cache_ik, l):
    nb, n_pages = page_table.shape
    steps = n_pages // PAGES_PER_STEP
    ih = iq.shape[1]
    page = lambda r: pl.BlockSpec((None, None, PAGE_SIZE, HD),
                                  lambda bi, p, pt: (l, pt[bi, p * PAGES_PER_STEP + r], 0, 0))
    grid_spec = pltpu.PrefetchScalarGridSpec(
        num_scalar_prefetch=1,
        grid=(nb, steps),
        in_specs=[pl.BlockSpec((None, ih, HD), lambda bi, p, pt: (bi, 0, 0)),
                  pl.BlockSpec((None, ih, 1), lambda bi, p, pt: (bi, 0, 0)),
                  pl.BlockSpec((None, 1, HD), lambda bi, p, pt: (bi, 0, 0))]
                 + [page(r) for r in range(PAGES_PER_STEP)],
        out_specs=[pl.BlockSpec((None, 1, PAGES_PER_STEP * PAGE_SIZE), lambda bi, p, pt: (bi, 0, p)),
                   pl.BlockSpec((None, 1, LANE), lambda bi, p, pt: (bi, 0, 0))],
    )
    return pl.pallas_call(
        _dec_score_kernel,
        grid_spec=grid_spec,
        out_shape=[jax.ShapeDtypeStruct((nb, 1, n_pages * PAGE_SIZE), F32),
                   jax.ShapeDtypeStruct((nb, 1, LANE), F32)],
        compiler_params=_params(("parallel", "arbitrary")),
        name="dec_scores",
    )(page_table, iq, iw, ik_new, *([cache_ik] * PAGES_PER_STEP))


def _dec_select_kernel(sc_ref, new_ref, tau_ref, bnd_ref, seln_ref, *, n_sel, nbits):
    key = _sort_key(sc_ref[...])
    key_new = _sort_key(new_ref[...])[:, 0:1]
    nb, s_len = key.shape
    idx = lax.broadcasted_iota(jnp.int32, key.shape, 1)
    tau = _nth_largest_key(lambda c: _count(key >= c) + (key_new >= c).astype(F32), n_sel, (nb, 1))
    eq = key == tau
    need = n_sel - _count(key > tau) - (key_new > tau).astype(F32)
    n_eq = _count(eq)
    bound = lax.cond(
        jnp.max(n_eq - need) > 0.0,
        lambda: _tie_bound(lambda c: _count(eq & (idx < c)), need, nbits, (nb, 1)),
        lambda: jnp.full((nb, 1), s_len, jnp.int32))
    sel_new = (key_new > tau) | ((key_new == tau) & (n_eq < need))
    tau_ref[...] = jnp.broadcast_to(tau, tau_ref.shape)
    bnd_ref[...] = jnp.broadcast_to(bound, bnd_ref.shape)
    seln_ref[...] = jnp.broadcast_to(sel_new.astype(F32), seln_ref.shape)


def _dec_select(scores, score_new, n_sel):
    nb, s_len = scores.shape
    out = lambda dt: jax.ShapeDtypeStruct((nb, LANE), dt)
    return pl.pallas_call(
        functools.partial(_dec_select_kernel, n_sel=n_sel, nbits=s_len.bit_length()),
        out_shape=[out(jnp.int32), out(jnp.int32), out(F32)],
        compiler_params=pltpu.CompilerParams(vmem_limit_bytes=VMEM_LIMIT),
        name="dec_select",
    )(scores, score_new)


def _dec_attn_kernel(pt_ref, tau_ref, bnd_ref, seln_ref, sc_ref, q_ref, kn_ref, vn_ref, *refs, steps):
    kp = refs[:PAGES_PER_STEP]
    vp = refs[PAGES_PER_STEP:2 * PAGES_PER_STEP]
    o_ref, m_scr, l_scr, acc_scr = refs[2 * PAGES_PER_STEP:]
    p_id = pl.program_id(1)
    chunk = PAGES_PER_STEP * PAGE_SIZE
    scale = ATT_HEAD_DIM ** -0.5

    @pl.when(p_id == 0)
    def _():
        m_scr[...] = jnp.full_like(m_scr, NEG_BIG)
        l_scr[...] = jnp.zeros_like(l_scr)
        acc_scr[...] = jnp.zeros_like(acc_scr)

    tau = tau_ref[:, 0:1]
    bound = bnd_ref[:, 0:1]
    key_c = _sort_key(sc_ref[...])
    idx_c = p_id * chunk + lax.broadcasted_iota(jnp.int32, key_c.shape, 1)
    sel = (key_c > tau) | ((key_c == tau) & (idx_c <= bound))
    q = q_ref[...].astype(BF16)
    heads = lambda pages, n: jnp.concatenate(
        [r[pl.ds(n, PAGE_SIZE, stride=ATT_KV_HEADS), :].astype(BF16) for r in pages], axis=0)
    s = sum(lax.dot_general(q[:, n * HD:(n + 1) * HD], heads(kp, n), _NT, preferred_element_type=F32)
            for n in range(ATT_KV_HEADS)) * scale
    s = jnp.where(sel, s, NEG_BIG)
    m_old = m_scr[:, 0:1]
    m_new = jnp.maximum(m_old, jnp.max(s, axis=-1, keepdims=True))
    alpha = jnp.exp(m_old - m_new)
    p = jnp.where(sel, jnp.exp(s - m_new), 0.0)
    l_new = alpha * l_scr[:, 0:1] + jnp.sum(p, axis=-1, keepdims=True)
    pb = p.astype(BF16)
    acc_new = alpha * acc_scr[...] + jnp.concatenate(
        [jnp.dot(pb, heads(vp, n), preferred_element_type=F32) for n in range(ATT_KV_HEADS)], axis=1)
    m_scr[...] = jnp.broadcast_to(m_new, m_scr.shape)
    l_scr[...] = jnp.broadcast_to(l_new, l_scr.shape)
    acc_scr[...] = acc_new

    @pl.when(p_id == steps - 1)
    def _():
        sel_new = seln_ref[:, 0:1] > 0.0
        qf = q_ref[...]
        s_new = jnp.sum(qf * kn_ref[...], axis=-1, keepdims=True) * scale
        s_new = jnp.where(sel_new, s_new, NEG_BIG)
        m_fin = jnp.maximum(m_new, s_new)
        a2 = jnp.exp(m_new - m_fin)
        p_new = jnp.where(sel_new, jnp.exp(s_new - m_fin), 0.0)
        l_fin = a2 * l_new + p_new
        acc_fin = (a2 * acc_new + p_new * vn_ref[...]) / l_fin
        group = ATT_HEADS // ATT_KV_HEADS
        for h in range(ATT_HEADS):
            n = h // group
            o_ref[:, h * HD:(h + 1) * HD] = acc_fin[h:h + 1, n * HD:(n + 1) * HD]


def _dec_attn(page_table, scores, score_new, q_bd, k_new, v_new, cache_k, cache_v, l):
    nb, n_pages = page_table.shape
    steps = n_pages // PAGES_PER_STEP
    s_len = n_pages * PAGE_SIZE
    chunk = PAGES_PER_STEP * PAGE_SIZE
    n_sel = min(TOPK_MAX, (s_len + 1) // 4)
    tau, bound, sel_new = _dec_select(scores.reshape(nb, s_len), score_new.reshape(nb, LANE), n_sel)
    row = lambda a: a.reshape(nb, 1, LANE)
    row_spec = pl.BlockSpec((None, 1, LANE), lambda bi, p, pt: (bi, 0, 0))
    kvw = ATT_KV_HEADS * HD
    hp = q_bd.shape[1]
    page = lambda r: pl.BlockSpec((None, None, PAGE_SIZE * ATT_KV_HEADS, HD),
                                  lambda bi, p, pt: (l, pt[bi, p * PAGES_PER_STEP + r], 0, 0))
    grid_spec = pltpu.PrefetchScalarGridSpec(
        num_scalar_prefetch=1,
        grid=(nb, steps),
        in_specs=[row_spec, row_spec, row_spec,
                  pl.BlockSpec((None, 1, chunk), lambda bi, p, pt: (bi, 0, p)),
                  pl.BlockSpec((None, hp, kvw), lambda bi, p, pt: (bi, 0, 0)),
                  pl.BlockSpec((None, 1, kvw), lambda bi, p, pt: (bi, 0, 0)),
                  pl.BlockSpec((None, 1, kvw), lambda bi, p, pt: (bi, 0, 0))]
                 + [page(r) for r in range(PAGES_PER_STEP)] * 2,
        out_specs=pl.BlockSpec((None, 1, ATT_HEADS * HD), lambda bi, p, pt: (bi, 0, 0)),
        scratch_shapes=[pltpu.VMEM((hp, LANE), F32), pltpu.VMEM((hp, LANE), F32),
                        pltpu.VMEM((hp, kvw), F32)],
    )
    return pl.pallas_call(
        functools.partial(_dec_attn_kernel, steps=steps),
        grid_spec=grid_spec,
        out_shape=jax.ShapeDtypeStruct((nb, 1, ATT_HEADS * HD), F32),
        compiler_params=_params(("parallel", "arbitrary")),
        name="dec_attn",
    )(page_table, row(tau), row(bound), row(sel_new), scores, q_bd, k_new, v_new,
      *([cache_k] * PAGES_PER_STEP), *([cache_v] * PAGES_PER_STEP)).reshape(nb, ATT_HEADS * HD)


def _ffn(x, xb, ssq, w1, w3, w2, l, tm, next_gain):
    tm_up = 2 * tm if (x.shape[0] // tm) % 2 == 0 else tm
    g = _matmul(xb, [w1, w3], l, tm=tm_up, epilogue='swiglu', out_dtype=BF16, ssq=ssq)
    return _matmul(g, [w2], l, tm=tm, epilogue='resid', res=x, scale=0.5, next_gain=next_gain)


def _token_tiles(n_tokens):
    n_tiles = max(1, n_tokens // 1024)
    tm = -(-n_tokens // (16 * n_tiles)) * 16
    return tm, n_tiles


def kernel(x_prompt, x_sample, state_ret, state_hgrn, cache_k, cache_v, cache_idx_k, page_table, ffn1_norm, ffn1_w1, ffn1_w3, ffn1_w2, mix_norm, w_in, ret_norm, q_norm, k_norm, idx_k_g, idx_k_b, hg_lb_raw, hg_norm, w_up_ret, w_up_att, w_up_hg, w_out, ffn2_norm, ffn2_w1, ffn2_w3, ffn2_w2):
    b, t, d = x_prompt.shape
    nb = x_sample.shape[0]
    depth = w_in.shape[0]
    kvw = ATT_KV_HEADS * ATT_HEAD_DIM
    group = ATT_HEADS // ATT_KV_HEADS

    lb_soft = jax.nn.softmax(hg_lb_raw.astype(F32), axis=0)
    lb_all = jnp.cumsum(lb_soft, axis=0) - lb_soft[0]
    w_in_p = _repack_w_in(w_in)
    cast = lambda w: w.astype(BF16)
    ffn1_w2, ffn2_w2 = cast(ffn1_w2), cast(ffn2_w2)
    w_up_ret, w_up_att, w_up_hg, w_out = cast(w_up_ret), cast(w_up_att), cast(w_up_hg), cast(w_out)

    pos_p = jnp.arange(t, dtype=jnp.int32)
    pos_s = jnp.full((nb,), PAST_LEN, jnp.int32)
    ret_f = 1.0 / (ROPE_THETA ** jnp.linspace(0.0, 1.0, RET_DK // 2, dtype=F32))
    att_f = ROPE_THETA ** (-jnp.arange(0, ATT_HEAD_DIM, 2, dtype=F32) / ATT_HEAD_DIM)
    idx_f = ROPE_THETA ** (-jnp.arange(0, IDX_ROPE_DIM, 2, dtype=F32) / IDX_ROPE_DIM)
    tabs_p = _rope_tables(pos_p, att_f, ATT_HEAD_DIM) + _rope_tables(pos_p, idx_f, IDX_ROPE_DIM)
    tabs_s = _rope_tables(pos_s, att_f, ATT_HEAD_DIM) + _rope_tables(pos_s, idx_f, IDX_ROPE_DIM)
    ret_tab_p = _rope_tables(pos_p, ret_f, RET_DK)
    ret_tab_s = _rope_tables(pos_s[:1], ret_f, RET_DK)

    ck = cache_k.reshape(cache_k.shape[:2] + (PAGE_SIZE * ATT_KV_HEADS, HD))
    cv = cache_v.reshape(cache_v.shape[:2] + (PAGE_SIZE * ATT_KV_HEADS, HD))

    mp = b * t
    tm, n_tiles = _token_tiles(mp + nb)
    mt = tm * n_tiles
    x = jnp.concatenate([x_prompt.reshape(mp, d), x_sample.reshape(nb, d), jnp.zeros((mt - mp - nb, d), F32)])

    def with_tail(a, rows):
        tail = jnp.concatenate([rows.astype(a.dtype), jnp.zeros((mt - mp - nb, a.shape[1]), a.dtype)])
        return lax.dynamic_update_slice(a, tail, (mp, 0))

    outs = {n: [] for n in ('rp', 'rs', 'hp', 'hs', 'kp', 'vp', 'ip', 'ks', 'vs', 'is')}
    xb, ssq = _row_prep(x, ffn1_norm, 0)
    for l in range(depth):
        x, xb, ssq = _ffn(x, xb, ssq, ffn1_w1, ffn1_w3, ffn1_w2, l, tm, (mix_norm, l))
        z = _matmul(xb, [w_in_p], l, tm=2 * tm if n_tiles % 2 == 0 else tm, single_lhs=False, ssq=ssq)

        aq, ak, av, akb, avb, iq, ik, ikb, iw = _prelude(z, mp, l, tabs_p, q_norm, k_norm, idx_k_g, idx_k_b, t)
        a_ret, r_p = _retention(z, l, ret_tab_p[0], ret_tab_p[1], ret_norm, b, t)
        a_hg, h_p = _gla(z, l, lb_all, hg_norm, b, t)
        a_att = _dsa_prompt(aq, iq, iw, akb, avb, ikb, b, t, mt)
        outs['rp'].append(r_p); outs['hp'].append(h_p)
        outs['kp'].append(ak.reshape(b, t, ATT_KV_HEADS, ATT_HEAD_DIM))
        outs['vp'].append(av.reshape(b, t, ATT_KV_HEADS, ATT_HEAD_DIM))
        outs['ip'].append(ik.reshape(b, t, IDX_DIM))

        zs = z[mp:mp + nb]
        aq, ak, av, _, _, iq, ik, _, iw = _prelude(zs, nb, l, tabs_s, q_norm, k_norm, idx_k_g, idx_k_b, 1)
        s_ret, s_hg, r_s, h_s = _decode_rec(zs, l, ret_tab_s[0], ret_tab_s[1], ret_norm, hg_norm, lb_all,
                                            state_ret, state_hgrn)
        scores, score_new = _dec_scores(page_table, iq.reshape(nb, IDX_HEADS, IDX_DIM),
                                        iw[:, :IDX_HEADS].reshape(nb, IDX_HEADS, 1),
                                        ik.reshape(nb, 1, IDX_DIM), cache_idx_k, l)
        qh = aq.reshape(nb, ATT_KV_HEADS, group, ATT_HEAD_DIM)
        q_bd = jnp.concatenate(
            [jnp.concatenate([qh[:, n] if m == n else jnp.zeros_like(qh[:, n]) for m in range(ATT_KV_HEADS)], axis=-1)
             for n in range(ATT_KV_HEADS)]
            + [jnp.zeros((nb, max(0, 16 - ATT_HEADS), kvw), aq.dtype)], axis=1)
        s_att = _dec_attn(page_table, scores, score_new, q_bd, ak.reshape(nb, 1, kvw), av.reshape(nb, 1, kvw),
                          ck, cv, l)
        outs['rs'].append(r_s); outs['hs'].append(h_s)
        outs['ks'].append(ak.reshape(nb, 1, ATT_KV_HEADS, ATT_HEAD_DIM))
        outs['vs'].append(av.reshape(nb, 1, ATT_KV_HEADS, ATT_HEAD_DIM))
        outs['is'].append(ik.reshape(nb, 1, IDX_DIM))

        merged = _merge(with_tail(a_ret, s_ret.reshape(nb, -1)), with_tail(a_att, s_att),
                        with_tail(a_hg, s_hg.reshape(nb, -1)), w_up_ret, w_up_att, w_up_hg, z, l, BF16, tm)
        x, xb, ssq = _matmul(merged, [w_out], l, tm=tm, epilogue='resid', res=x, scale=1.0,
                             next_gain=(ffn2_norm, l))
        if l + 1 < depth:
            x, xb, ssq = _ffn(x, xb, ssq, ffn2_w1, ffn2_w3, ffn2_w2, l, tm, (ffn1_norm, l + 1))
        else:
            x = _ffn(x, xb, ssq, ffn2_w1, ffn2_w3, ffn2_w2, l, tm, None)

    st = lambda n: jnp.stack(outs[n])
    return (x[:mp].reshape(b, t, d), x[mp:mp + nb].reshape(nb, 1, d),
            st('rp').astype(state_ret.dtype), st('rs').astype(state_ret.dtype),
            st('hp').astype(state_hgrn.dtype), st('hs').astype(state_hgrn.dtype),
            st('kp'), st('vp'), st('ip'), st('ks'), st('vs'), st('is'))
```

```python
import functools
import math

import numpy as np
import jax
import jax.numpy as jnp
from jax import lax
from jax.experimental import pallas as pl
from jax.experimental.pallas import tpu as pltpu

D_MODEL = 4096
BATCH = 4
SEQ = 2048
DEPTH = 2
DEC_BATCH = 8
DEC_SEQ = 1
PAST_LEN = 16384
PAGE_SIZE = 128

RET_HEADS = 8
RET_DK = 128
RET_DV = 128
ATT_HEADS = 8
ATT_KV_HEADS = 2
ATT_HEAD_DIM = 128
IDX_HEADS = 16
IDX_DIM = 128
IDX_ROPE_DIM = 64
TOPK_MAX = 256
Q_BLOCK = 128
HG_HEADS = 8
HG_DK = 128
HG_DV = 128
D_FF = 11008
ROPE_THETA = 10000.0
EPS = 1e-6
NEG_BIG = -1e30

F32 = jnp.float32
BF16 = jnp.bfloat16
LANE = 128
HD = 128
INT_MIN = -(2 ** 31)
VMEM_LIMIT = 56 * 1024 * 1024
GLA_CHUNK = 64
GLA_SUB = 16
SEARCH_GROUPS = 4
GLA_SAFE_SPAN = 60.0
PAGES_PER_STEP = 16

_NT = (((1,), (1,)), ((), ()))
_TN = (((0,), (0,)), ((), ()))


def _params(sem):
    return pltpu.CompilerParams(dimension_semantics=sem, vmem_limit_bytes=VMEM_LIMIT)


def _bdot(a, b, dims=None):
    a = a.astype(BF16)
    b = b.astype(BF16)
    if dims is None:
        return jnp.dot(a, b, preferred_element_type=F32)
    return lax.dot_general(a, b, dims, preferred_element_type=F32)


def _silu(x):
    return x * jax.nn.sigmoid(x)


def _segments():
    ret_qk = RET_HEADS * RET_DK
    ret_w = RET_HEADS * RET_DV
    att_w = ATT_HEADS * ATT_HEAD_DIM
    kv_w = ATT_KV_HEADS * ATT_HEAD_DIM
    hg_k = HG_HEADS * HG_DK
    hg_w = HG_HEADS * HG_DV
    names = ['r_q', 'r_k', 'r_v', 'r_g', 'a_q', 'a_k', 'a_v', 'i_q', 'i_k', 'i_w',
             'h_f', 'h_q', 'h_i', 'h_g', 'g_ret', 'g_att', 'g_hg']
    widths = [ret_qk, ret_qk, ret_w, ret_w, att_w, kv_w, kv_w, IDX_HEADS * IDX_DIM, IDX_DIM, IDX_HEADS,
              hg_k, hg_k, hg_w, hg_w, D_MODEL, D_MODEL, D_MODEL]
    src = {}
    off = 0
    for n, w in zip(names, widths):
        src[n] = (off, w)
        off += w
    padded = {n: -(-w // LANE) * LANE for n, w in zip(names, widths)}
    order = ['g_ret', 'g_att', 'g_hg'] + sorted(
        [n for n in names if not n.startswith('g_')], key=lambda n: -padded[n])
    dst = {}
    off = 0
    for n in order:
        dst[n] = off
        off += padded[n]
    total = -(-off // 256) * 256
    for n in order:
        assert dst[n] % padded[n] == 0 or n.startswith('g_'), (n, dst[n], padded[n])
    return src, dst, padded, order, total


def _repack_kernel(start_ref, valid_ref, a_ref, o_ref):
    c = pl.program_id(1)
    a = a_ref[0]
    row = lax.broadcasted_iota(jnp.int32, a.shape, 0)
    o_ref[...] = jnp.where(row < valid_ref[c], a, 0.0).T.astype(o_ref.dtype)


def _repack_w_in(w_in):
    src, dst, padded, order, total = _segments()
    depth, kdim, n_in = w_in.shape
    start = np.zeros((total // LANE,), np.int32)
    valid = np.zeros((total // LANE,), np.int32)
    for n in order:
        s, w = src[n]
        for t in range(padded[n] // LANE):
            c = dst[n] // LANE + t
            start[c], valid[c] = s + t * LANE, min(LANE, w - t * LANE)
    assert (start + LANE <= n_in).all() and (start % 8 == 0).all()
    start //= 8
    grid_spec = pltpu.PrefetchScalarGridSpec(
        num_scalar_prefetch=2,
        grid=(depth, total // LANE),
        in_specs=[pl.BlockSpec((pl.Element(1), pl.Element(LANE), pl.Element(kdim)),
                               lambda l, c, start, valid: (l, start[c] * 8, 0))],
        out_specs=pl.BlockSpec((None, kdim, LANE), lambda l, c, start, valid: (l, 0, c)),
    )
    return pl.pallas_call(
        _repack_kernel,
        grid_spec=grid_spec,
        out_shape=jax.ShapeDtypeStruct((depth, kdim, total), BF16),
        compiler_params=_params(("parallel", "parallel")),
        name="repack_w_in",
    )(jnp.asarray(start), jnp.asarray(valid), jnp.swapaxes(w_in, 1, 2))


def _row_prep_kernel(x_ref, g_ref, xb_ref, ssq_ref):
    x = x_ref[...]
    xb_ref[...] = (x * g_ref[...]).astype(xb_ref.dtype)
    ssq_ref[...] = jnp.broadcast_to(jnp.sum(x * x, axis=-1, keepdims=True), ssq_ref.shape)


def _row_tile(m, cap):
    best = None
    for t in range(16, min(m, cap) + 1, 16):
        if m % t == 0:
            best = t
    return best if best is not None else m


def _row_prep(x, g_all, l):
    m, d = x.shape
    tr = _row_tile(m, 512)
    return pl.pallas_call(
        _row_prep_kernel,
        grid=(m // tr,),
        in_specs=[pl.BlockSpec((tr, d), lambda i: (i, 0)),
                  pl.BlockSpec((None, 1, d), lambda i: (l, 0, 0))],
        out_specs=[pl.BlockSpec((tr, d), lambda i: (i, 0)), pl.BlockSpec((tr, LANE), lambda i: (i, 0))],
        out_shape=[jax.ShapeDtypeStruct((m, d), BF16), jax.ShapeDtypeStruct((m, LANE), F32)],
        compiler_params=_params(("parallel",)),
        name="row_prep",
    )(x, g_all.reshape(g_all.shape[0], 1, d))


def _mm_kernel(*refs, n_w, epilogue, scale, row_norm, feed_norm):
    it = iter(refs)
    lhs_ref = next(it)
    w_refs = [next(it) for _ in range(n_w)]
    res_ref = next(it) if epilogue == 'resid' else None
    ssq_ref = next(it) if row_norm else None
    gain_ref = next(it) if feed_norm else None
    out_ref = next(it)
    xb_ref, ssq_out_ref = (next(it), next(it)) if feed_norm else (None, None)

    lhs = lhs_ref[...].astype(BF16)
    vals = [jnp.dot(lhs, w[...].astype(BF16), preferred_element_type=F32) for w in w_refs]
    if row_norm:
        r = lax.rsqrt(ssq_ref[:, 0:1] * (1.0 / lhs_ref.shape[1]) + EPS)
        vals = [v * r for v in vals]
    if epilogue == 'swiglu':
        out = _silu(vals[0]) * vals[1]
    elif epilogue == 'resid':
        out = res_ref[...] + scale * vals[0]
    else:
        out = vals[0]
    out_ref[...] = out.astype(out_ref.dtype)
    if feed_norm:
        xb_ref[...] = (out * gain_ref[...]).astype(xb_ref.dtype)
        part = jnp.broadcast_to(jnp.sum(out * out, axis=-1, keepdims=True), ssq_out_ref.shape)
        j = pl.program_id(1)

        @pl.when(j == 0)
        def _():
            ssq_out_ref[...] = part

        @pl.when(j > 0)
        def _():
            ssq_out_ref[...] += part


def _pick(n, cands):
    for c in cands:
        if n % c == 0:
            return c
    return n


def _matmul(lhs, ws, l, *, tm, epilogue='plain', res=None, scale=1.0, out_dtype=F32, tn=256, single_lhs=None,
            ssq=None, next_gain=None):
    m, kdim = lhs.shape
    n = ws[0].shape[-1]
    tn = _pick(n, (tn, 256, 128))
    assert m % tm == 0 and n % tn == 0
    if single_lhs is None:
        single_lhs = tm * kdim * lhs.dtype.itemsize > (12 << 20)
    lhs_mode = dict(pipeline_mode=pl.Buffered(1)) if single_lhs else {}
    in_specs = [pl.BlockSpec((tm, kdim), lambda i, j: (i, 0), **lhs_mode)]
    in_specs += [pl.BlockSpec((None, kdim, tn), lambda i, j: (l, 0, j)) for _ in ws]
    args = [lhs] + list(ws)
    tile = pl.BlockSpec((tm, tn), lambda i, j: (i, j))
    rows = pl.BlockSpec((tm, LANE), lambda i, j: (i, 0))
    if epilogue == 'resid':
        in_specs.append(tile)
        args.append(res)
    if ssq is not None:
        in_specs.append(rows)
        args.append(ssq)
    out_specs, out_shape = tile, jax.ShapeDtypeStruct((m, n), out_dtype)
    if next_gain is not None:
        gains, gl = next_gain
        in_specs.append(pl.BlockSpec((None, 1, tn), lambda i, j: (gl, 0, j)))
        args.append(gains.reshape(gains.shape[0], 1, n))
        out_specs = [tile, tile, rows]
        out_shape = [out_shape, jax.ShapeDtypeStruct((m, n), BF16), jax.ShapeDtypeStruct((m, LANE), F32)]
    return pl.pallas_call(
        functools.partial(_mm_kernel, n_w=len(ws), epilogue=epilogue, scale=scale,
                          row_norm=ssq is not None, feed_norm=next_gain is not None),
        grid=(m // tm, n // tn),
        in_specs=in_specs,
        out_specs=out_specs,
        out_shape=out_shape,
        compiler_params=_params(("parallel", "arbitrary")),
        name="mm_" + epilogue,
    )(*args)


def _in_proj_kernel(start_ref, lhs_ref, w_ref, ssq_ref, out_ref):
    w = w_ref[0].astype(BF16)
    acc = lax.dot_general(lhs_ref[...], w, _NT, preferred_element_type=F32)
    out_ref[...] = acc * lax.rsqrt(ssq_ref[:, 0:1] * (1.0 / lhs_ref.shape[1]) + EPS)


def _in_proj(xb, ssq, w_in, l, tm):
    src, dst, padded, order, total = _segments()
    depth, kdim, n_in = w_in.shape
    m = xb.shape[0]
    tn = 256
    col_src = np.zeros((total,), np.int64)
    for n in order:
        col_src[dst[n]:dst[n] + padded[n]] = src[n][0] + np.arange(padded[n])
    start = col_src[::tn].copy()
    assert (col_src.reshape(-1, tn) == start[:, None] + np.arange(tn)).all(), "tile is not one source range"
    assert (start % 8 == 0).all() and (start + tn <= n_in).all()
    grid_spec = pltpu.PrefetchScalarGridSpec(
        num_scalar_prefetch=1,
        grid=(m // tm, total // tn),
        in_specs=[pl.BlockSpec((tm, kdim), lambda i, j, start: (i, 0), pipeline_mode=pl.Buffered(1)),
                  pl.BlockSpec((pl.Element(1), pl.Element(tn), pl.Element(kdim)),
                               lambda i, j, start: (l, start[j] * 8, 0)),
                  pl.BlockSpec((tm, LANE), lambda i, j, start: (i, 0))],
        out_specs=pl.BlockSpec((tm, tn), lambda i, j, start: (i, j)),
    )
    return pl.pallas_call(
        _in_proj_kernel,
        grid_spec=grid_spec,
        out_shape=jax.ShapeDtypeStruct((m, total), F32),
        compiler_params=_params(("parallel", "arbitrary")),
        name="in_proj",
    )(jnp.asarray(start // 8, jnp.int32), xb, jnp.swapaxes(w_in, 1, 2), ssq)


def _merge_kernel(ar_ref, aa_ref, ah_ref, wr_ref, wa_ref, wh_ref, gr_ref, ga_ref, gh_ref, o_ref):
    u_r = _bdot(ar_ref[...], wr_ref[...])
    u_a = _bdot(aa_ref[...], wa_ref[...])
    u_h = _bdot(ah_ref[...], wh_ref[...])
    out = (jax.nn.sigmoid(gr_ref[...]) * u_r + jax.nn.sigmoid(ga_ref[...]) * u_a
           + jax.nn.sigmoid(gh_ref[...]) * u_h)
    o_ref[...] = out.astype(o_ref.dtype)


def _merge(a_ret, a_att, a_hg, w_r, w_a, w_h, z, l, out_dtype, tm):
    m = a_ret.shape[0]
    d = w_r.shape[-1]
    tn = _pick(d, (256, 128))
    nb = d // tn
    lhs_spec = lambda a: pl.BlockSpec((tm, a.shape[1]), lambda i, j: (i, 0))
    w_spec = lambda w: pl.BlockSpec((None, w.shape[1], tn), lambda i, j: (l, 0, j))
    gate_spec = lambda g: pl.BlockSpec((tm, tn), lambda i, j: (i, g * nb + j))
    return pl.pallas_call(
        _merge_kernel,
        grid=(m // tm, nb),
        in_specs=[lhs_spec(a_ret), lhs_spec(a_att), lhs_spec(a_hg), w_spec(w_r), w_spec(w_a), w_spec(w_h),
                  gate_spec(0), gate_spec(1), gate_spec(2)],
        out_specs=pl.BlockSpec((tm, tn), lambda i, j: (i, j)),
        out_shape=jax.ShapeDtypeStruct((m, d), out_dtype),
        compiler_params=_params(("parallel", "parallel")),
        name="merge",
    )(a_ret, a_att, a_hg, w_r, w_a, w_h, z, z, z)


def _rot(x, cos, sin, half):
    up = pltpu.roll(x, LANE - half, 1)
    dn = pltpu.roll(x, half, 1)
    lane = lax.broadcasted_iota(jnp.int32, x.shape, 1)
    return x * cos + jnp.where(lane < half, up, dn) * sin


def _rope_tables(pos, freqs, width):
    ang = pos.astype(F32)[:, None] * freqs[None, :]
    cos, sin = jnp.cos(ang), jnp.sin(ang)
    t = pos.shape[0]
    pad_c = jnp.ones((t, LANE - width), F32)
    pad_s = jnp.zeros((t, LANE - width), F32)
    return (jnp.concatenate([cos, cos, pad_c], axis=1), jnp.concatenate([-sin, sin, pad_s], axis=1))


def _prelude_kernel(zq_ref, zk_ref, zv_ref, ziq_ref, zik_ref, ziw_ref, ca_ref, sa_ref, ci_ref, si_ref,
                    qn_ref, kn_ref, ig_ref, ib_ref,
                    aq_ref, ak_ref, av_ref, akb_ref, avb_ref, iq_ref, ik_ref, ikb_ref, iw_ref):
    ca, sa, ci, si = ca_ref[...], sa_ref[...], ci_ref[...], si_ref[...]

    def head_norm(x, g):
        return x * lax.rsqrt(jnp.mean(x * x, axis=-1, keepdims=True) + EPS) * g

    for h in range(ATT_HEADS):
        sl = slice(h * HD, (h + 1) * HD)
        aq_ref[:, sl] = _rot(head_norm(zq_ref[:, sl], qn_ref[...]), ca, sa, HD // 2).astype(aq_ref.dtype)
    for h in range(ATT_KV_HEADS):
        sl = slice(h * HD, (h + 1) * HD)
        k = _rot(head_norm(zk_ref[:, sl], kn_ref[...]), ca, sa, HD // 2)
        ak_ref[:, sl] = k
        akb_ref[:, sl] = k.astype(akb_ref.dtype)
    v = zv_ref[...]
    av_ref[...] = v
    avb_ref[...] = v.astype(avb_ref.dtype)
    for h in range(IDX_HEADS):
        sl = slice(h * HD, (h + 1) * HD)
        iq_ref[:, sl] = (_rot(ziq_ref[:, sl], ci, si, IDX_ROPE_DIM // 2) * (IDX_DIM ** -0.5)).astype(iq_ref.dtype)
    x = zik_ref[...]
    mu = jnp.mean(x, axis=-1, keepdims=True)
    var = jnp.mean(jnp.square(x - mu), axis=-1, keepdims=True)
    ik = _rot((x - mu) * lax.rsqrt(var + EPS) * ig_ref[...] + ib_ref[...], ci, si, IDX_ROPE_DIM // 2)
    ik_ref[...] = ik
    ikb_ref[...] = ik.astype(ikb_ref.dtype)
    iw_ref[...] = ziw_ref[...] * (IDX_HEADS ** -0.5)


def _prelude(z, m, l, tabs, q_norm, k_norm, idx_g, idx_b, t_len):
    _, dst, padded, _, _ = _segments()
    tr = min(m, 256)
    nt = t_len // tr if t_len >= tr else 1
    lowp = BF16 if tr >= 16 else F32
    aw, kvw, iw = padded['a_q'], padded['a_k'], padded['i_q']

    def zspec(name):
        w = padded[name]
        return pl.BlockSpec((tr, w), lambda i: (i, dst[name] // w))

    tab_spec = pl.BlockSpec((tr, LANE), lambda i: (i % nt, 0))
    vec_spec = pl.BlockSpec((None, 1, HD), lambda i: (l, 0, 0))
    row = lambda w: pl.BlockSpec((tr, w), lambda i: (i, 0))
    shp = lambda w, dt: jax.ShapeDtypeStruct((m, w), dt)
    vec = lambda a: a.reshape(a.shape[0], 1, HD)
    return pl.pallas_call(
        _prelude_kernel,
        grid=(m // tr,),
        in_specs=[zspec('a_q'), zspec('a_k'), zspec('a_v'), zspec('i_q'), zspec('i_k'), zspec('i_w'),
                  tab_spec, tab_spec, tab_spec, tab_spec, vec_spec, vec_spec, vec_spec, vec_spec],
        out_specs=[row(aw), row(kvw), row(kvw), row(kvw), row(kvw), row(iw), row(HD), row(HD), row(LANE)],
        out_shape=[shp(aw, lowp), shp(kvw, F32), shp(kvw, F32), shp(kvw, lowp), shp(kvw, lowp),
                   shp(iw, lowp), shp(HD, F32), shp(HD, lowp), shp(LANE, F32)],
        compiler_params=_params(("parallel",)),
        name="attn_prelude",
    )(z, z, z, z, z, z, *tabs, vec(q_norm), vec(k_norm), vec(idx_g), vec(idx_b))


def _ret_gamma_log(h):
    return math.log(1.0 - 2.0 ** (-5.0 - h))


def _retention_kernel(lg_ref, q_ref, k_ref, v_ref, g_ref, cos_ref, sin_ref, nrm_ref, o_ref, st_ref, s_scr, *, nc):
    c = pl.program_id(2)
    cr = q_ref.shape[0]
    lg = lg_ref[:, 0:1]

    @pl.when(c == 0)
    def _():
        s_scr[...] = jnp.zeros_like(s_scr)

    cos, sin = cos_ref[...], sin_ref[...]
    q = _rot(q_ref[...], cos, sin, RET_DK // 2)
    k = _rot(k_ref[...], cos, sin, RET_DK // 2) * (RET_DK ** -0.5)
    v = v_ref[...]
    ti = lax.broadcasted_iota(jnp.int32, (cr, cr), 0)
    si = lax.broadcasted_iota(jnp.int32, (cr, cr), 1)
    diff = (ti - si).astype(F32)
    intra = jnp.where(diff >= 0, jnp.exp(lg * jnp.maximum(diff, 0.0)), 0.0)
    tcol = lax.broadcasted_iota(jnp.int32, (cr, 1), 0).astype(F32)
    q_dec = jnp.exp(lg * (tcol + 1.0))
    k_dec = jnp.exp(lg * (cr - 1.0 - tcol))
    s_dec = jnp.exp(lg * cr)
    s = s_scr[...]
    a = _bdot(q, k, _NT) * intra
    o = _bdot(a, v) + _bdot(q * q_dec, s)
    s_new = s * s_dec + _bdot(k * k_dec, v, _TN)
    s_scr[...] = s_new
    y = o * lax.rsqrt(jnp.mean(o * o, axis=-1, keepdims=True) + EPS) * nrm_ref[...]
    o_ref[...] = (y * _silu(g_ref[...])).astype(o_ref.dtype)

    @pl.when(c == nc - 1)
    def _():
        st_ref[...] = s_new


def _retention(z, l, cos, sin, ret_norm, b, t):
    _, dst, _, _, _ = _segments()
    cr = min(t, 256)
    nc = t // cr
    lg = jnp.broadcast_to(
        jnp.asarray([_ret_gamma_log(h) for h in range(RET_HEADS)], F32)[:, None, None], (RET_HEADS, 1, LANE))

    def zspec(name):
        return pl.BlockSpec((cr, HD), lambda bi, h, c: (bi * nc + c, dst[name] // HD + h))

    tab = pl.BlockSpec((cr, LANE), lambda bi, h, c: (c, 0))
    return pl.pallas_call(
        functools.partial(_retention_kernel, nc=nc),
        grid=(b, RET_HEADS, nc),
        in_specs=[pl.BlockSpec((None, 1, LANE), lambda bi, h, c: (h, 0, 0)),
                  zspec('r_q'), zspec('r_k'), zspec('r_v'), zspec('r_g'), tab, tab,
                  pl.BlockSpec((None, 1, HD), lambda bi, h, c: (l, 0, h))],
        out_specs=[pl.BlockSpec((cr, HD), lambda bi, h, c: (bi * nc + c, h)),
                   pl.BlockSpec((None, None, RET_DK, RET_DV), lambda bi, h, c: (bi, h, 0, 0))],
        out_shape=[jax.ShapeDtypeStruct((z.shape[0], RET_HEADS * RET_DV), BF16),
                   jax.ShapeDtypeStruct((b, RET_HEADS, RET_DK, RET_DV), F32)],
        scratch_shapes=[pltpu.VMEM((RET_DK, RET_DV), F32)],
        compiler_params=_params(("parallel", "parallel", "arbitrary")),
        name="retention",
    )(lg, z, z, z, z, cos, sin, ret_norm.reshape(ret_norm.shape[0], 1, -1))


def _hgrn_gates(fa, lb):
    log_f = jnp.minimum(fa, 0.0) - jnp.log1p(jnp.exp(-jnp.abs(fa))) + jnp.log1p(lb * jnp.exp(-fa))
    hk = (1.0 - lb) * jax.nn.sigmoid(-fa)
    return log_f, hk


def _split3(x):
    hi = x.astype(BF16)
    r1 = x - hi.astype(F32)
    mid = r1.astype(BF16)
    lo = (r1 - mid.astype(F32)).astype(BF16)
    return hi, mid, lo


def _gla_state_step(q, k, v, b, st):
    b_last = b[GLA_CHUNK - 1:GLA_CHUNK, :]
    o = _bdot(q * jnp.exp(b), st, _NT)
    kd = k * jnp.exp(b_last - b)
    return o, st * jnp.exp(b_last) + _bdot(v, kd, _TN)


def _gla_intra_anchored(q, k, v, b):
    c = GLA_CHUNK
    nsub = c // GLA_SUB
    ti = lax.broadcasted_iota(jnp.int32, (c, c), 0)
    si = lax.broadcasted_iota(jnp.int32, (c, c), 1)
    anchors = [jnp.zeros((1, HD), F32)] + [b[i * GLA_SUB - 1:i * GLA_SUB, :] for i in range(1, nsub)]
    m_rows = jnp.concatenate([jnp.broadcast_to(m, (GLA_SUB, HD)) for m in anchors], axis=0)
    qs = q * jnp.exp(b - m_rows)
    a = jnp.zeros((c, c), F32)
    for i in range(nsub):
        ks = k * jnp.exp(jnp.minimum(anchors[i] - b, GLA_SAFE_SPAN))
        blk = (ti // GLA_SUB == i) & (si <= ti)
        a = a + jnp.where(blk, _bdot(qs, ks, _NT), 0.0)
    return _bdot(a, v)


def _gla_intra_pairwise(q, k, v, b):
    c = GLA_CHUNK
    nsub = c // GLA_SUB
    ti = lax.broadcasted_iota(jnp.int32, (c, c), 0)
    si = lax.broadcasted_iota(jnp.int32, (c, c), 1)
    a_off = jnp.zeros((c, c), F32)
    for i in range(1, nsub):
        m = b[i * GLA_SUB - 1:i * GLA_SUB, :]
        qs = q * jnp.exp(jnp.minimum(b - m, 0.0))
        ks = k * jnp.exp(jnp.minimum(m - b, 0.0))
        blk = (ti // GLA_SUB == i) & (si < i * GLA_SUB)
        a_off = a_off + jnp.where(blk, _bdot(qs, ks, _NT), 0.0)
    o = _bdot(a_off, v)
    rows = lax.broadcasted_iota(jnp.int32, (GLA_SUB, 1), 0)
    diag = []
    for i in range(nsub):
        sl = slice(i * GLA_SUB, (i + 1) * GLA_SUB)
        qi, ki, vi, bi = q[sl], k[sl], v[sl], b[sl]
        oi = jnp.zeros((GLA_SUB, HD), F32)
        for s in range(GLA_SUB):
            d = jnp.exp(jnp.minimum(bi - bi[s:s + 1], 0.0)) * qi * ki[s:s + 1]
            w = jnp.where(rows >= s, jnp.sum(d, axis=-1, keepdims=True), 0.0)
            oi = oi + w * vi[s:s + 1]
        diag.append(oi)
    return o + jnp.concatenate(diag, axis=0)


def _gla_kernel(f_ref, q_ref, i_ref, g_ref, lb_ref, nrm_ref, o_ref, st_ref, s_scr, *, nc, n_inner):
    c = pl.program_id(2)
    blk = f_ref.shape[0]

    @pl.when(c == 0)
    def _():
        s_scr[...] = jnp.zeros_like(s_scr)

    log_f, hk = _hgrn_gates(f_ref[...], lb_ref[...])
    hq = _silu(q_ref[...])
    v = i_ref[...]
    ti = lax.broadcasted_iota(jnp.int32, (blk, blk), 0)
    si = lax.broadcasted_iota(jnp.int32, (blk, blk), 1)
    tri = ((ti >= si) & (ti // GLA_CHUNK == si // GLA_CHUNK)).astype(BF16)
    hi, mid, lo = _split3(log_f)
    b = (jnp.dot(tri, hi, preferred_element_type=F32) + jnp.dot(tri, mid, preferred_element_type=F32)
         + jnp.dot(tri, lo, preferred_element_type=F32))
    sub_sum = jnp.sum(log_f.reshape(blk // GLA_SUB, GLA_SUB, HD), axis=1)
    anchored_ok = jnp.min(sub_sum) >= -GLA_SAFE_SPAN

    def run(intra):
        st = s_scr[...]
        outs = []
        for ci in range(n_inner):
            sl = slice(ci * GLA_CHUNK, (ci + 1) * GLA_CHUNK)
            o_inter, st = _gla_state_step(hq[sl], hk[sl], v[sl], b[sl], st)
            outs.append(o_inter + intra(hq[sl], hk[sl], v[sl], b[sl]))
        s_scr[...] = st
        o = jnp.concatenate(outs, axis=0)
        y = o * lax.rsqrt(jnp.mean(o * o, axis=-1, keepdims=True) + EPS) * nrm_ref[...]
        o_ref[...] = (y * jax.nn.sigmoid(g_ref[...])).astype(o_ref.dtype)

    lax.cond(anchored_ok, lambda: run(_gla_intra_anchored), lambda: run(_gla_intra_pairwise))

    @pl.when(c == nc - 1)
    def _():
        st_ref[...] = s_scr[...].T


def _gla(z, l, lb, hg_norm, b, t):
    _, dst, _, _, _ = _segments()
    blk = min(t, 256)
    assert blk % GLA_CHUNK == 0
    nc = t // blk

    def zspec(name):
        return pl.BlockSpec((blk, HD), lambda bi, h, c: (bi * nc + c, dst[name] // HD + h))

    vec = lambda: pl.BlockSpec((None, 1, HD), lambda bi, h, c: (l, 0, h))
    return pl.pallas_call(
        functools.partial(_gla_kernel, nc=nc, n_inner=blk // GLA_CHUNK),
        grid=(b, HG_HEADS, nc),
        in_specs=[zspec('h_f'), zspec('h_q'), zspec('h_i'), zspec('h_g'), vec(), vec()],
        out_specs=[pl.BlockSpec((blk, HD), lambda bi, h, c: (bi * nc + c, h)),
                   pl.BlockSpec((None, None, HG_DK, HG_DV), lambda bi, h, c: (bi, h, 0, 0))],
        out_shape=[jax.ShapeDtypeStruct((z.shape[0], HG_HEADS * HG_DV), BF16),
                   jax.ShapeDtypeStruct((b, HG_HEADS, HG_DK, HG_DV), F32)],
        scratch_shapes=[pltpu.VMEM((HG_DV, HG_DK), F32)],
        compiler_params=_params(("parallel", "parallel", "arbitrary")),
        name="hgrn2",
    )(z, z, z, z, lb.reshape(lb.shape[0], 1, -1), hg_norm.reshape(hg_norm.shape[0], 1, -1))


def _to_col(row):
    n = row.shape[1]
    eye = lax.broadcasted_iota(jnp.int32, (n, n), 0) == lax.broadcasted_iota(jnp.int32, (n, n), 1)
    return jnp.sum(jnp.where(eye, row, 0.0), axis=1, keepdims=True)


def _decode_rec_kernel(rq_ref, rk_ref, rv_ref, rg_ref, hf_ref, hq_ref, hi_ref, hg_ref,
                       cos_ref, sin_ref, rn_ref, hn_ref, lb_ref, sr_ref, sh_ref,
                       or_ref, oh_ref, nr_ref, nh_ref):
    cos, sin = cos_ref[...], sin_ref[...]

    def rms(o, g):
        return o * lax.rsqrt(jnp.mean(o * o, axis=-1, keepdims=True) + EPS) * g

    for h in range(RET_HEADS):
        sl = slice(h * HD, (h + 1) * HD)
        gamma = math.exp(_ret_gamma_log(h))
        q = _rot(rq_ref[:, sl], cos, sin, RET_DK // 2)
        k = _rot(rk_ref[:, sl], cos, sin, RET_DK // 2) * (RET_DK ** -0.5)
        v = rv_ref[:, sl]
        s = sr_ref[h]
        o = jnp.sum(q * k, axis=-1, keepdims=True) * v + jnp.sum(_to_col(q * gamma) * s, axis=0, keepdims=True)
        nr_ref[h] = s * gamma + _to_col(k) * v
        or_ref[:, sl] = rms(o, rn_ref[:, sl]) * _silu(rg_ref[:, sl])

    for h in range(HG_HEADS):
        sl = slice(h * HD, (h + 1) * HD)
        log_f, k = _hgrn_gates(hf_ref[:, sl], lb_ref[:, sl])
        q = _silu(hq_ref[:, sl])
        v = hi_ref[:, sl]
        s = sh_ref[h]
        eb = jnp.exp(log_f)
        o = jnp.sum(q * k, axis=-1, keepdims=True) * v + jnp.sum(_to_col(q * eb) * s, axis=0, keepdims=True)
        nh_ref[h] = s * _to_col(eb) + _to_col(k) * v
        oh_ref[:, sl] = rms(o, hn_ref[:, sl]) * jax.nn.sigmoid(hg_ref[:, sl])


def _decode_rec(z, l, cos, sin, ret_norm, hg_norm, lb, state_ret, state_hgrn):
    _, dst, padded, _, _ = _segments()
    nb = z.shape[0]

    z = z.reshape(nb, 1, z.shape[1])

    def zspec(name):
        w = padded[name]
        return pl.BlockSpec((None, 1, w), lambda bi: (bi, 0, dst[name] // w))

    one = lambda w: pl.BlockSpec((1, w), lambda bi: (0, 0))
    vec = lambda a: pl.BlockSpec((None, 1, a.shape[-1]), lambda bi: (l, 0, 0))
    st = lambda a: pl.BlockSpec((None, None) + a.shape[2:], lambda bi: (l, bi, 0, 0, 0))
    st_out = lambda a: pl.BlockSpec((None,) + a.shape[2:], lambda bi: (bi, 0, 0, 0))
    rw, hw = RET_HEADS * RET_DV, HG_HEADS * HG_DV
    r3 = lambda a: a.reshape(a.shape[0], 1, -1)
    return pl.pallas_call(
        _decode_rec_kernel,
        grid=(nb,),
        in_specs=[zspec('r_q'), zspec('r_k'), zspec('r_v'), zspec('r_g'),
                  zspec('h_f'), zspec('h_q'), zspec('h_i'), zspec('h_g'),
                  one(LANE), one(LANE), vec(ret_norm), vec(hg_norm), vec(lb), st(state_ret), st(state_hgrn)],
        out_specs=[pl.BlockSpec((None, 1, rw), lambda bi: (bi, 0, 0)),
                   pl.BlockSpec((None, 1, hw), lambda bi: (bi, 0, 0)),
                   st_out(state_ret), st_out(state_hgrn)],
        out_shape=[jax.ShapeDtypeStruct((nb, 1, rw), F32), jax.ShapeDtypeStruct((nb, 1, hw), F32),
                   jax.ShapeDtypeStruct(state_ret.shape[1:], F32), jax.ShapeDtypeStruct(state_hgrn.shape[1:], F32)],
        compiler_params=_params(("arbitrary",)),
        name="decode_recurrent",
    )(z, z, z, z, z, z, z, z, cos, sin, r3(ret_norm), r3(hg_norm), r3(lb), state_ret, state_hgrn)


def _sort_key(score):
    bits = lax.bitcast_convert_type(score + 0.0, jnp.int32)
    return bits ^ ((bits >> 31) & jnp.int32(0x7FFFFFFF))


def _count(mask):
    return jnp.sum(mask.astype(F32), axis=-1, keepdims=True)


def _nth_largest_key(count_ge, n_sel, shape):
    return _nth_largest_keys([count_ge], n_sel, shape)[0]


def _nth_largest_keys(count_fns, n_sel, shape):
    zero = jnp.zeros(shape, jnp.int32)
    los = tuple(jnp.where(f(zero) >= n_sel, 0, INT_MIN).astype(jnp.int32) for f in count_fns)

    def body(i, los):
        bit = jnp.left_shift(jnp.int32(1), 30 - i)
        return tuple(jnp.where(f(lo | bit) >= n_sel, lo | bit, lo) for f, lo in zip(count_fns, los))

    return lax.fori_loop(0, 31, body, los)


def _tie_bound(count_eq_below, need, nbits, shape):
    def body(i, j):
        cand = j | jnp.left_shift(jnp.int32(1), nbits - 1 - i)
        return jnp.where(count_eq_below(cand) < need, cand, j)

    return lax.fori_loop(0, nbits, body, jnp.zeros(shape, jnp.int32))


def _dsa_prompt_body(length, aq_ref, iq_ref, iw_ref, k_ref, v_ref, ik_ref, o_ref, key_scr, sel_scr, n_sel):
    j = pl.program_id(1)
    tq = aq_ref.shape[0]
    ik = ik_ref[0:length, :]
    score = jnp.zeros((tq, length), F32)
    for h in range(IDX_HEADS):
        s = lax.dot_general(iq_ref[:, h * HD:(h + 1) * HD], ik, _NT, preferred_element_type=F32)
        score = score + jnp.maximum(s, 0.0) * iw_ref[:, h:h + 1]
    q_pos = j * tq + lax.broadcasted_iota(jnp.int32, (tq, 1), 0)
    col = lax.broadcasted_iota(jnp.int32, (tq, length), 1)
    visible = col <= q_pos
    key_scr[:, 0:length] = _sort_key(jnp.where(visible, score, NEG_BIG))

    rows = tq // SEARCH_GROUPS
    group_count = lambda r: (lambda c: _count(key_scr[r * rows:(r + 1) * rows, 0:length] >= c))
    tau = jnp.concatenate(
        _nth_largest_keys([group_count(r) for r in range(SEARCH_GROUPS)], n_sel, (rows, 1)), axis=0)
    key = key_scr[:, 0:length]
    gt = key > tau
    eq = key == tau
    need = n_sel - _count(gt)
    spare = jnp.max(_count(eq & visible) - need) > 0.0
    bound = lax.cond(
        spare,
        lambda: _tie_bound(lambda c: _count((key_scr[:, 0:length] == tau) & (col < c)), need,
                           max(1, length.bit_length()), (tq, 1)),
        lambda: jnp.full((tq, 1), length, jnp.int32))
    sel_scr[:, 0:length] = ((gt | (eq & (col <= bound))) & visible).astype(F32)

    group = ATT_HEADS // ATT_KV_HEADS
    scale = ATT_HEAD_DIM ** -0.5
    for n in range(ATT_KV_HEADS):
        kn = k_ref[0:length, n * HD:(n + 1) * HD]
        vn = v_ref[0:length, n * HD:(n + 1) * HD]
        for g in range(group):
            sl = slice((n * group + g) * HD, (n * group + g + 1) * HD)
            s = lax.dot_general(aq_ref[:, sl], kn, _NT, preferred_element_type=F32) * scale
            s = jnp.where(sel_scr[:, 0:length] > 0.0, s, NEG_BIG)
            m = jnp.max(s, axis=-1, keepdims=True)
            p = jnp.exp(s - m)
            o = _bdot(p, vn) / jnp.sum(p, axis=-1, keepdims=True)
            o_ref[:, sl] = o.astype(o_ref.dtype)


def _dsa_prompt_kernel(*refs, n_sel, lengths):
    j = pl.program_id(1)
    tq = refs[0].shape[0]
    prev = 0
    for length in lengths:
        @pl.when((j >= prev // tq) & (j < length // tq))
        def _(length=length):
            _dsa_prompt_body(length, *refs, n_sel)
        prev = length


def _dsa_prompt(aq, iq, iw, kb, vb, ikb, b, t, mt):
    tq = min(Q_BLOCK, t)
    nq = t // tq
    n_sel = min(TOPK_MAX, t // 4)
    step = min(t, 512)
    lengths = tuple(range(step, t + 1, step))
    assert t % step == 0 and step % tq == 0 and step >= n_sel
    qrow = lambda w: pl.BlockSpec((tq, w), lambda bi, j: (bi * nq + j, 0))
    krow = lambda w: pl.BlockSpec((t, w), lambda bi, j: (bi, 0))
    return pl.pallas_call(
        functools.partial(_dsa_prompt_kernel, n_sel=n_sel, lengths=lengths),
        grid=(b, nq),
        in_specs=[qrow(aq.shape[1]), qrow(iq.shape[1]), qrow(LANE),
                  krow(kb.shape[1]), krow(vb.shape[1]), krow(HD)],
        out_specs=qrow(aq.shape[1]),
        out_shape=jax.ShapeDtypeStruct((mt, aq.shape[1]), BF16),
        scratch_shapes=[pltpu.VMEM((tq, t), jnp.int32), pltpu.VMEM((tq, t), F32)],
        compiler_params=_params(("parallel", "arbitrary")),
        name="dsa_prompt",
    )(aq, iq, iw, kb, vb, ikb)


def _dec_score_kernel(pt_ref, iq_ref, iw_ref, ikn_ref, *refs):
    pages = refs[:PAGES_PER_STEP]
    sc_ref, new_ref = refs[PAGES_PER_STEP:]
    iq = iq_ref[...].astype(BF16)
    iw = iw_ref[...]
    ik = jnp.concatenate([p[...].astype(BF16) for p in pages], axis=0)
    s = lax.dot_general(iq, ik, _NT, preferred_element_type=F32)
    sc_ref[...] = jnp.sum(jnp.maximum(s, 0.0) * iw, axis=0, keepdims=True)

    @pl.when(pl.program_id(1) == 0)
    def _():
        sn = lax.dot_general(iq, jnp.broadcast_to(ikn_ref[...], (8, HD)).astype(BF16), _NT,
                             preferred_element_type=F32)[:, 0:1]
        new_ref[...] = jnp.broadcast_to(jnp.sum(jnp.maximum(sn, 0.0) * iw, axis=0, keepdims=True), (1, LANE))


def _dec_scores(page_table, iq, iw, ik_new, cache_ik, l):
    nb, n_pages = page_table.shape
    steps = n_pages // PAGES_PER_STEP
    ih = iq.shape[1]
    page = lambda r: pl.BlockSpec((None, None, PAGE_SIZE, HD),
                                  lambda bi, p, pt: (l, pt[bi, p * PAGES_PER_STEP + r], 0, 0))
    grid_spec = pltpu.PrefetchScalarGridSpec(
        num_scalar_prefetch=1,
        grid=(nb, steps),
        in_specs=[pl.BlockSpec((None, ih, HD), lambda bi, p, pt: (bi, 0, 0)),
                  pl.BlockSpec((None, ih, 1), lambda bi, p, pt: (bi, 0, 0)),
                  pl.BlockSpec((None, 1, HD), lambda bi, p, pt: (bi, 0, 0))]
                 + [page(r) for r in range(PAGES_PER_STEP)],
        out_specs=[pl.BlockSpec((None, 1, PAGES_PER_STEP * PAGE_SIZE), lambda bi, p, pt: (bi, 0, p)),
                   pl.BlockSpec((None, 1, LANE), lambda bi, p, pt: (bi, 0, 0))],
    )
    return pl.pallas_call(
        _dec_score_kernel,
        grid_spec=grid_spec,
        out_shape=[jax.ShapeDtypeStruct((nb, 1, n_pages * PAGE_SIZE), F32),
                   jax.ShapeDtypeStruct((nb, 1, LANE), F32)],
        compiler_params=_params(("parallel", "arbitrary")),
        name="dec_scores",
    )(page_table, iq, iw, ik_new, *([cache_ik] * PAGES_PER_STEP))


def _dec_select_kernel(sc_ref, new_ref, tau_ref, bnd_ref, seln_ref, *, n_sel, nbits):
    key = _sort_key(sc_ref[...])
    key_new = _sort_key(new_ref[...])[:, 0:1]
    nb, s_len = key.shape
    idx = lax.broadcasted_iota(jnp.int32, key.shape, 1)
    tau = _nth_largest_key(lambda c: _count(key >= c) + (key_new >= c).astype(F32), n_sel, (nb, 1))
    eq = key == tau
    need = n_sel - _count(key > tau) - (key_new > tau).astype(F32)
    n_eq = _count(eq)
    bound = lax.cond(
        jnp.max(n_eq - need) > 0.0,
        lambda: _tie_bound(lambda c: _count(eq & (idx < c)), need, nbits, (nb, 1)),
        lambda: jnp.full((nb, 1), s_len, jnp.int32))
    sel_new = (key_new > tau) | ((key_new == tau) & (n_eq < need))
    tau_ref[...] = jnp.broadcast_to(tau, tau_ref.shape)
    bnd_ref[...] = jnp.broadcast_to(bound, bnd_ref.shape)
    seln_ref[...] = jnp.broadcast_to(sel_new.astype(F32), seln_ref.shape)


def _dec_select(scores, score_new, n_sel):
    nb, s_len = scores.shape
    out = lambda dt: jax.ShapeDtypeStruct((nb, LANE), dt)
    return pl.pallas_call(
        functools.partial(_dec_select_kernel, n_sel=n_sel, nbits=s_len.bit_length()),
        out_shape=[out(jnp.int32), out(jnp.int32), out(F32)],
        compiler_params=pltpu.CompilerParams(vmem_limit_bytes=VMEM_LIMIT),
        name="dec_select",
    )(scores, score_new)


def _dec_attn_kernel(pt_ref, tau_ref, bnd_ref, seln_ref, sc_ref, q_ref, kn_ref, vn_ref, *refs, steps):
    kp = refs[:PAGES_PER_STEP]
    vp = refs[PAGES_PER_STEP:2 * PAGES_PER_STEP]
    o_ref, m_scr, l_scr, acc_scr = refs[2 * PAGES_PER_STEP:]
    p_id = pl.program_id(1)
    chunk = PAGES_PER_STEP * PAGE_SIZE
    scale = ATT_HEAD_DIM ** -0.5

    @pl.when(p_id == 0)
    def _():
        m_scr[...] = jnp.full_like(m_scr, NEG_BIG)
        l_scr[...] = jnp.zeros_like(l_scr)
        acc_scr[...] = jnp.zeros_like(acc_scr)

    tau = tau_ref[:, 0:1]
    bound = bnd_ref[:, 0:1]
    key_c = _sort_key(sc_ref[...])
    idx_c = p_id * chunk + lax.broadcasted_iota(jnp.int32, key_c.shape, 1)
    sel = (key_c > tau) | ((key_c == tau) & (idx_c <= bound))
    q = q_ref[...].astype(BF16)
    heads = lambda pages, n: jnp.concatenate(
        [r[pl.ds(n, PAGE_SIZE, stride=ATT_KV_HEADS), :].astype(BF16) for r in pages], axis=0)
    s = sum(lax.dot_general(q[:, n * HD:(n + 1) * HD], heads(kp, n), _NT, preferred_element_type=F32)
            for n in range(ATT_KV_HEADS)) * scale
    s = jnp.where(sel, s, NEG_BIG)
    m_old = m_scr[:, 0:1]
    m_new = jnp.maximum(m_old, jnp.max(s, axis=-1, keepdims=True))
    alpha = jnp.exp(m_old - m_new)
    p = jnp.where(sel, jnp.exp(s - m_new), 0.0)
    l_new = alpha * l_scr[:, 0:1] + jnp.sum(p, axis=-1, keepdims=True)
    pb = p.astype(BF16)
    acc_new = alpha * acc_scr[...] + jnp.concatenate(
        [jnp.dot(pb, heads(vp, n), preferred_element_type=F32) for n in range(ATT_KV_HEADS)], axis=1)
    m_scr[...] = jnp.broadcast_to(m_new, m_scr.shape)
    l_scr[...] = jnp.broadcast_to(l_new, l_scr.shape)
    acc_scr[...] = acc_new

    @pl.when(p_id == steps - 1)
    def _():
        sel_new = seln_ref[:, 0:1] > 0.0
        qf = q_ref[...]
        s_new = jnp.sum(qf * kn_ref[...], axis=-1, keepdims=True) * scale
        s_new = jnp.where(sel_new, s_new, NEG_BIG)
        m_fin = jnp.maximum(m_new, s_new)
        a2 = jnp.exp(m_new - m_fin)
        p_new = jnp.where(sel_new, jnp.exp(s_new - m_fin), 0.0)
        l_fin = a2 * l_new + p_new
        acc_fin = (a2 * acc_new + p_new * vn_ref[...]) / l_fin
        group = ATT_HEADS // ATT_KV_HEADS
        for h in range(ATT_HEADS):
            n = h // group
            o_ref[:, h * HD:(h + 1) * HD] = acc_fin[h:h + 1, n * HD:(n + 1) * HD]


def _dec_attn(page_table, scores, score_new, q_bd, k_new, v_new, cache_k, cache_v, l):
    nb, n_pages = page_table.shape
    steps = n_pages // PAGES_PER_STEP
    s_len = n_pages * PAGE_SIZE
    chunk = PAGES_PER_STEP * PAGE_SIZE
    n_sel = min(TOPK_MAX, (s_len + 1) // 4)
    tau, bound, sel_new = _dec_select(scores.reshape(nb, s_len), score_new.reshape(nb, LANE), n_sel)
    row = lambda a: a.reshape(nb, 1, LANE)
    row_spec = pl.BlockSpec((None, 1, LANE), lambda bi, p, pt: (bi, 0, 0))
    kvw = ATT_KV_HEADS * HD
    hp = q_bd.shape[1]
    page = lambda r: pl.BlockSpec((None, None, PAGE_SIZE * ATT_KV_HEADS, HD),
                                  lambda bi, p, pt: (l, pt[bi, p * PAGES_PER_STEP + r], 0, 0))
    grid_spec = pltpu.PrefetchScalarGridSpec(
        num_scalar_prefetch=1,
        grid=(nb, steps),
        in_specs=[row_spec, row_spec, row_spec,
                  pl.BlockSpec((None, 1, chunk), lambda bi, p, pt: (bi, 0, p)),
                  pl.BlockSpec((None, hp, kvw), lambda bi, p, pt: (bi, 0, 0)),
                  pl.BlockSpec((None, 1, kvw), lambda bi, p, pt: (bi, 0, 0)),
                  pl.BlockSpec((None, 1, kvw), lambda bi, p, pt: (bi, 0, 0))]
                 + [page(r) for r in range(PAGES_PER_STEP)] * 2,
        out_specs=pl.BlockSpec((None, 1, ATT_HEADS * HD), lambda bi, p, pt: (bi, 0, 0)),
        scratch_shapes=[pltpu.VMEM((hp, LANE), F32), pltpu.VMEM((hp, LANE), F32),
                        pltpu.VMEM((hp, kvw), F32)],
    )
    return pl.pallas_call(
        functools.partial(_dec_attn_kernel, steps=steps),
        grid_spec=grid_spec,
        out_shape=jax.ShapeDtypeStruct((nb, 1, ATT_HEADS * HD), F32),
        compiler_params=_params(("parallel", "arbitrary")),
        name="dec_attn",
    )(page_table, row(tau), row(bound), row(sel_new), scores, q_bd, k_new, v_new,
      *([cache_k] * PAGES_PER_STEP), *([cache_v] * PAGES_PER_STEP)).reshape(nb, ATT_HEADS * HD)


def _ffn(x, xb, ssq, w1, w3, w2, l, tm, next_gain):
    tm_up = 2 * tm if (x.shape[0] // tm) % 2 == 0 else tm
    g = _matmul(xb, [w1, w3], l, tm=tm_up, epilogue='swiglu', out_dtype=BF16, ssq=ssq)
    return _matmul(g, [w2], l, tm=tm, epilogue='resid', res=x, scale=0.5, next_gain=next_gain)


def _token_tiles(n_tokens):
    n_tiles = max(1, n_tokens // 1024)
    tm = -(-n_tokens // (16 * n_tiles)) * 16
    return tm, n_tiles


def kernel(x_prompt, x_sample, state_ret, state_hgrn, cache_k, cache_v, cache_idx_k, page_table, ffn1_norm, ffn1_w1, ffn1_w3, ffn1_w2, mix_norm, w_in, ret_norm, q_norm, k_norm, idx_k_g, idx_k_b, hg_lb_raw, hg_norm, w_up_ret, w_up_att, w_up_hg, w_out, ffn2_norm, ffn2_w1, ffn2_w3, ffn2_w2):
    b, t, d = x_prompt.shape
    nb = x_sample.shape[0]
    depth = w_in.shape[0]
    kvw = ATT_KV_HEADS * ATT_HEAD_DIM
    group = ATT_HEADS // ATT_KV_HEADS

    lb_soft = jax.nn.softmax(hg_lb_raw.astype(F32), axis=0)
    lb_all = jnp.cumsum(lb_soft, axis=0) - lb_soft[0]
    cast = lambda w: w.astype(BF16)
    ffn1_w2, ffn2_w2 = cast(ffn1_w2), cast(ffn2_w2)
    w_up_ret, w_up_att, w_up_hg, w_out = cast(w_up_ret), cast(w_up_att), cast(w_up_hg), cast(w_out)

    pos_p = jnp.arange(t, dtype=jnp.int32)
    pos_s = jnp.full((nb,), PAST_LEN, jnp.int32)
    ret_f = 1.0 / (ROPE_THETA ** jnp.linspace(0.0, 1.0, RET_DK // 2, dtype=F32))
    att_f = ROPE_THETA ** (-jnp.arange(0, ATT_HEAD_DIM, 2, dtype=F32) / ATT_HEAD_DIM)
    idx_f = ROPE_THETA ** (-jnp.arange(0, IDX_ROPE_DIM, 2, dtype=F32) / IDX_ROPE_DIM)
    tabs_p = _rope_tables(pos_p, att_f, ATT_HEAD_DIM) + _rope_tables(pos_p, idx_f, IDX_ROPE_DIM)
    tabs_s = _rope_tables(pos_s, att_f, ATT_HEAD_DIM) + _rope_tables(pos_s, idx_f, IDX_ROPE_DIM)
    ret_tab_p = _rope_tables(pos_p, ret_f, RET_DK)
    ret_tab_s = _rope_tables(pos_s[:1], ret_f, RET_DK)

    ck = cache_k.reshape(cache_k.shape[:2] + (PAGE_SIZE * ATT_KV_HEADS, HD))
    cv = cache_v.reshape(cache_v.shape[:2] + (PAGE_SIZE * ATT_KV_HEADS, HD))

    mp = b * t
    tm, n_tiles = _token_tiles(mp + nb)
    mt = tm * n_tiles
    x = jnp.concatenate([x_prompt.reshape(mp, d), x_sample.reshape(nb, d), jnp.zeros((mt - mp - nb, d), F32)])

    def with_tail(a, rows):
        tail = jnp.concatenate([rows.astype(a.dtype), jnp.zeros((mt - mp - nb, a.shape[1]), a.dtype)])
        return lax.dynamic_update_slice(a, tail, (mp, 0))

    outs = {n: [] for n in ('rp', 'rs', 'hp', 'hs', 'kp', 'vp', 'ip', 'ks', 'vs', 'is')}
    xb, ssq = _row_prep(x, ffn1_norm, 0)
    for l in range(depth):
        x, xb, ssq = _ffn(x, xb, ssq, ffn1_w1, ffn1_w3, ffn1_w2, l, tm, (mix_norm, l))
        z = _in_proj(xb, ssq, w_in, l, 2 * tm if n_tiles % 2 == 0 else tm)

        aq, ak, av, akb, avb, iq, ik, ikb, iw = _prelude(z, mp, l, tabs_p, q_norm, k_norm, idx_k_g, idx_k_b, t)
        a_ret, r_p = _retention(z, l, ret_tab_p[0], ret_tab_p[1], ret_norm, b, t)
        a_hg, h_p = _gla(z, l, lb_all, hg_norm, b, t)
        a_att = _dsa_prompt(aq, iq, iw, akb, avb, ikb, b, t, mt)
        outs['rp'].append(r_p); outs['hp'].append(h_p)
        outs['kp'].append(ak.reshape(b, t, ATT_KV_HEADS, ATT_HEAD_DIM))
        outs['vp'].append(av.reshape(b, t, ATT_KV_HEADS, ATT_HEAD_DIM))
        outs['ip'].append(ik.reshape(b, t, IDX_DIM))

        zs = z[mp:mp + nb]
        aq, ak, av, _, _, iq, ik, _, iw = _prelude(zs, nb, l, tabs_s, q_norm, k_norm, idx_k_g, idx_k_b, 1)
        s_ret, s_hg, r_s, h_s = _decode_rec(zs, l, ret_tab_s[0], ret_tab_s[1], ret_norm, hg_norm, lb_all,
                                            state_ret, state_hgrn)
        scores, score_new = _dec_scores(page_table, iq.reshape(nb, IDX_HEADS, IDX_DIM),
                                        iw[:, :IDX_HEADS].reshape(nb, IDX_HEADS, 1),
                                        ik.reshape(nb, 1, IDX_DIM), cache_idx_k, l)
        qh = aq.reshape(nb, ATT_KV_HEADS, group, ATT_HEAD_DIM)
        q_bd = jnp.concatenate(
            [jnp.concatenate([qh[:, n] if m == n else jnp.zeros_like(qh[:, n]) for m in range(ATT_KV_HEADS)], axis=-1)
             for n in range(ATT_KV_HEADS)]
            + [jnp.zeros((nb, max(0, 16 - ATT_HEADS), kvw), aq.dtype)], axis=1)
        s_att = _dec_attn(page_table, scores, score_new, q_bd, ak.reshape(nb, 1, kvw), av.reshape(nb, 1, kvw),
                          ck, cv, l)
        outs['rs'].append(r_s); outs['hs'].append(h_s)
        outs['ks'].append(ak.reshape(nb, 1, ATT_KV_HEADS, ATT_HEAD_DIM))
        outs['vs'].append(av.reshape(nb, 1, ATT_KV_HEADS, ATT_HEAD_DIM))
        outs['is'].append(ik.reshape(nb, 1, IDX_DIM))

        merged = _merge(with_tail(a_ret, s_ret.reshape(nb, -1)), with_tail(a_att, s_att),
                        with_tail(a_hg, s_hg.reshape(nb, -1)), w_up_ret, w_up_att, w_up_hg, z, l, BF16, tm)
        x, xb, ssq = _matmul(merged, [w_out], l, tm=tm, epilogue='resid', res=x, scale=1.0,
                             next_gain=(ffn2_norm, l))
        if l + 1 < depth:
            x, xb, ssq = _ffn(x, xb, ssq, ffn2_w1, ffn2_w3, ffn2_w2, l, tm, (ffn1_norm, l + 1))
        else:
            x = _ffn(x, xb, ssq, ffn2_w1, ffn2_w3, ffn2_w2, l, tm, None)

    st = lambda n: jnp.stack(outs[n])
    return (x[:mp].reshape(b, t, d), x[mp:mp + nb].reshape(nb, 1, d),
            st('rp').astype(state_ret.dtype), st('rs').astype(state_ret.dtype),
            st('hp').astype(state_hgrn.dtype), st('hs').astype(state_hgrn.dtype),
            st('kp'), st('vp'), st('ip'), st('ks'), st('vs'), st('is'))
```

```python
import functools
import math

import numpy as np
import jax
import jax.numpy as jnp
from jax import lax
from jax.experimental import pallas as pl
from jax.experimental.pallas import tpu as pltpu

D_MODEL = 4096
BATCH = 4
SEQ = 2048
DEPTH = 2
DEC_BATCH = 8
DEC_SEQ = 1
PAST_LEN = 16384
PAGE_SIZE = 128

RET_HEADS = 8
RET_DK = 128
RET_DV = 128
ATT_HEADS = 8
ATT_KV_HEADS = 2
ATT_HEAD_DIM = 128
IDX_HEADS = 16
IDX_DIM = 128
IDX_ROPE_DIM = 64
TOPK_MAX = 256
Q_BLOCK = 128
HG_HEADS = 8
HG_DK = 128
HG_DV = 128
D_FF = 11008
ROPE_THETA = 10000.0
EPS = 1e-6
NEG_BIG = -1e30

F32 = jnp.float32
BF16 = jnp.bfloat16
LANE = 128
HD = 128
INT_MIN = -(2 ** 31)
VMEM_LIMIT = 56 * 1024 * 1024
GLA_CHUNK = 64
GLA_SUB = 16
SEARCH_GROUPS = 4
GLA_SAFE_SPAN = 60.0
PAGES_PER_STEP = 16

_NT = (((1,), (1,)), ((), ()))
_TN = (((0,), (0,)), ((), ()))


def _params(sem):
    return pltpu.CompilerParams(dimension_semantics=sem, vmem_limit_bytes=VMEM_LIMIT)


def _bdot(a, b, dims=None):
    a = a.astype(BF16)
    b = b.astype(BF16)
    if dims is None:
        return jnp.dot(a, b, preferred_element_type=F32)
    return lax.dot_general(a, b, dims, preferred_element_type=F32)


def _silu(x):
    return x * jax.nn.sigmoid(x)


def _segments():
    ret_qk = RET_HEADS * RET_DK
    ret_w = RET_HEADS * RET_DV
    att_w = ATT_HEADS * ATT_HEAD_DIM
    kv_w = ATT_KV_HEADS * ATT_HEAD_DIM
    hg_k = HG_HEADS * HG_DK
    hg_w = HG_HEADS * HG_DV
    names = ['r_q', 'r_k', 'r_v', 'r_g', 'a_q', 'a_k', 'a_v', 'i_q', 'i_k', 'i_w',
             'h_f', 'h_q', 'h_i', 'h_g', 'g_ret', 'g_att', 'g_hg']
    widths = [ret_qk, ret_qk, ret_w, ret_w, att_w, kv_w, kv_w, IDX_HEADS * IDX_DIM, IDX_DIM, IDX_HEADS,
              hg_k, hg_k, hg_w, hg_w, D_MODEL, D_MODEL, D_MODEL]
    src = {}
    off = 0
    for n, w in zip(names, widths):
        src[n] = (off, w)
        off += w
    padded = {n: -(-w // LANE) * LANE for n, w in zip(names, widths)}
    order = ['g_ret', 'g_att', 'g_hg'] + sorted(
        [n for n in names if not n.startswith('g_')], key=lambda n: -padded[n])
    dst = {}
    off = 0
    for n in order:
        dst[n] = off
        off += padded[n]
    total = -(-off // 256) * 256
    for n in order:
        assert dst[n] % padded[n] == 0 or n.startswith('g_'), (n, dst[n], padded[n])
    return src, dst, padded, order, total


def _row_prep_kernel(x_ref, g_ref, xb_ref, ssq_ref):
    x = x_ref[...]
    xb_ref[...] = (x * g_ref[...]).astype(xb_ref.dtype)
    ssq_ref[...] = jnp.broadcast_to(jnp.sum(x * x, axis=-1, keepdims=True), ssq_ref.shape)


def _row_tile(m, cap):
    best = None
    for t in range(16, min(m, cap) + 1, 16):
        if m % t == 0:
            best = t
    return best if best is not None else m


def _row_prep(x, g_all, l):
    m, d = x.shape
    tr = _row_tile(m, 512)
    return pl.pallas_call(
        _row_prep_kernel,
        grid=(m // tr,),
        in_specs=[pl.BlockSpec((tr, d), lambda i: (i, 0)),
                  pl.BlockSpec((None, 1, d), lambda i: (l, 0, 0))],
        out_specs=[pl.BlockSpec((tr, d), lambda i: (i, 0)), pl.BlockSpec((tr, LANE), lambda i: (i, 0))],
        out_shape=[jax.ShapeDtypeStruct((m, d), BF16), jax.ShapeDtypeStruct((m, LANE), F32)],
        compiler_params=_params(("parallel",)),
        name="row_prep",
    )(x, g_all.reshape(g_all.shape[0], 1, d))


def _mm_kernel(*refs, n_w, epilogue, scale, row_norm, feed_norm, cast_rows):
    it = iter(refs)
    lhs_ref = next(it)
    w_refs = [next(it) for _ in range(n_w)]
    res_ref = next(it) if epilogue == 'resid' else None
    ssq_ref = next(it) if row_norm else None
    gain_ref = next(it) if feed_norm else None
    cast_in_ref = next(it) if cast_rows else None
    out_ref = next(it)
    xb_ref, ssq_out_ref = (next(it), next(it)) if feed_norm else (None, None)
    if cast_rows:
        next(it)[...] = cast_in_ref[...].astype(BF16)

    lhs = lhs_ref[...].astype(BF16)
    vals = [jnp.dot(lhs, w[...].astype(BF16), preferred_element_type=F32) for w in w_refs]
    if row_norm:
        r = lax.rsqrt(ssq_ref[:, 0:1] * (1.0 / lhs_ref.shape[1]) + EPS)
        vals = [v * r for v in vals]
    if epilogue == 'swiglu':
        out = _silu(vals[0]) * vals[1]
    elif epilogue == 'resid':
        out = res_ref[...] + scale * vals[0]
    else:
        out = vals[0]
    out_ref[...] = out.astype(out_ref.dtype)
    if feed_norm:
        xb_ref[...] = (out * gain_ref[...]).astype(xb_ref.dtype)
        part = jnp.broadcast_to(jnp.sum(out * out, axis=-1, keepdims=True), ssq_out_ref.shape)
        j = pl.program_id(1)

        @pl.when(j == 0)
        def _():
            ssq_out_ref[...] = part

        @pl.when(j > 0)
        def _():
            ssq_out_ref[...] += part


def _pick(n, cands):
    for c in cands:
        if n % c == 0:
            return c
    return n


def _matmul(lhs, ws, l, *, tm, epilogue='plain', res=None, scale=1.0, out_dtype=F32, tn=256, single_lhs=None,
            ssq=None, next_gain=None, cast_src=None):
    assert next_gain is None or cast_src is None
    m, kdim = lhs.shape
    n = ws[0].shape[-1]
    tn = _pick(n, (tn, 256, 128))
    assert m % tm == 0 and n % tn == 0
    if single_lhs is None:
        single_lhs = tm * kdim * lhs.dtype.itemsize > (12 << 20)
    lhs_mode = dict(pipeline_mode=pl.Buffered(1)) if single_lhs else {}
    in_specs = [pl.BlockSpec((tm, kdim), lambda i, j: (i, 0), **lhs_mode)]
    in_specs += [pl.BlockSpec((None, kdim, tn), lambda i, j: (l, 0, j)) for _ in ws]
    args = [lhs] + list(ws)
    tile = pl.BlockSpec((tm, tn), lambda i, j: (i, j))
    rows = pl.BlockSpec((tm, LANE), lambda i, j: (i, 0))
    if epilogue == 'resid':
        in_specs.append(tile)
        args.append(res)
    if ssq is not None:
        in_specs.append(rows)
        args.append(ssq)
    out_specs, out_shape = tile, jax.ShapeDtypeStruct((m, n), out_dtype)
    if next_gain is not None:
        gains, gl = next_gain
        in_specs.append(pl.BlockSpec((None, 1, tn), lambda i, j: (gl, 0, j)))
        args.append(gains.reshape(gains.shape[0], 1, n))
        out_specs = [tile, tile, rows]
        out_shape = [out_shape, jax.ShapeDtypeStruct((m, n), BF16), jax.ShapeDtypeStruct((m, LANE), F32)]
    cast_rows = 0
    if cast_src is not None:
        src, cl = cast_src
        steps, nj = (m // tm) * (n // tn), n // tn
        cast_rows = src.shape[1] // steps
        assert cast_rows * steps == src.shape[1] and cast_rows % 16 == 0
        in_specs.append(pl.BlockSpec((None, cast_rows, src.shape[2]), lambda i, j: (cl, i * nj + j, 0)))
        args.append(src)
        out_specs = [out_specs, pl.BlockSpec((cast_rows, src.shape[2]), lambda i, j: (i * nj + j, 0))]
        out_shape = [out_shape, jax.ShapeDtypeStruct(src.shape[1:], BF16)]
    return pl.pallas_call(
        functools.partial(_mm_kernel, n_w=len(ws), epilogue=epilogue, scale=scale,
                          row_norm=ssq is not None, feed_norm=next_gain is not None, cast_rows=cast_rows),
        grid=(m // tm, n // tn),
        in_specs=in_specs,
        out_specs=out_specs,
        out_shape=out_shape,
        compiler_params=_params(("parallel", "arbitrary")),
        name="mm_" + epilogue,
    )(*args)


def _in_proj_kernel(start_ref, lhs_ref, w_ref, ssq_ref, out_ref):
    w = w_ref[0].astype(BF16)
    acc = lax.dot_general(lhs_ref[...], w, _NT, preferred_element_type=F32)
    out_ref[...] = acc * lax.rsqrt(ssq_ref[:, 0:1] * (1.0 / lhs_ref.shape[1]) + EPS)


def _in_proj(xb, ssq, w_in, l, tm):
    src, dst, padded, order, total = _segments()
    depth, kdim, n_in = w_in.shape
    m = xb.shape[0]
    tn = 256
    col_src = np.zeros((total,), np.int64)
    for n in order:
        col_src[dst[n]:dst[n] + padded[n]] = src[n][0] + np.arange(padded[n])
    start = col_src[::tn].copy()
    assert (col_src.reshape(-1, tn) == start[:, None] + np.arange(tn)).all(), "tile is not one source range"
    assert (start % 8 == 0).all() and (start + tn <= n_in).all()
    grid_spec = pltpu.PrefetchScalarGridSpec(
        num_scalar_prefetch=1,
        grid=(m // tm, total // tn),
        in_specs=[pl.BlockSpec((tm, kdim), lambda i, j, start: (i, 0), pipeline_mode=pl.Buffered(1)),
                  pl.BlockSpec((pl.Element(1), pl.Element(tn), pl.Element(kdim)),
                               lambda i, j, start: (l, start[j] * 8, 0)),
                  pl.BlockSpec((tm, LANE), lambda i, j, start: (i, 0))],
        out_specs=pl.BlockSpec((tm, tn), lambda i, j, start: (i, j)),
    )
    return pl.pallas_call(
        _in_proj_kernel,
        grid_spec=grid_spec,
        out_shape=jax.ShapeDtypeStruct((m, total), F32),
        compiler_params=_params(("parallel", "arbitrary")),
        name="in_proj",
    )(jnp.asarray(start // 8, jnp.int32), xb, jnp.swapaxes(w_in, 1, 2), ssq)


def _merge_kernel(ar_ref, aa_ref, ah_ref, wr_ref, wa_ref, wh_ref, gr_ref, ga_ref, gh_ref, o_ref):
    u_r = _bdot(ar_ref[...], wr_ref[...])
    u_a = _bdot(aa_ref[...], wa_ref[...])
    u_h = _bdot(ah_ref[...], wh_ref[...])
    out = (jax.nn.sigmoid(gr_ref[...]) * u_r + jax.nn.sigmoid(ga_ref[...]) * u_a
           + jax.nn.sigmoid(gh_ref[...]) * u_h)
    o_ref[...] = out.astype(o_ref.dtype)


def _merge(a_ret, a_att, a_hg, w_r, w_a, w_h, z, l, out_dtype, tm):
    m = a_ret.shape[0]
    d = w_r.shape[-1]
    tn = _pick(d, (256, 128))
    nb = d // tn
    lhs_spec = lambda a: pl.BlockSpec((tm, a.shape[1]), lambda i, j: (i, 0))
    w_spec = lambda w: pl.BlockSpec((None, w.shape[1], tn), lambda i, j: (l, 0, j))
    gate_spec = lambda g: pl.BlockSpec((tm, tn), lambda i, j: (i, g * nb + j))
    return pl.pallas_call(
        _merge_kernel,
        grid=(m // tm, nb),
        in_specs=[lhs_spec(a_ret), lhs_spec(a_att), lhs_spec(a_hg), w_spec(w_r), w_spec(w_a), w_spec(w_h),
                  gate_spec(0), gate_spec(1), gate_spec(2)],
        out_specs=pl.BlockSpec((tm, tn), lambda i, j: (i, j)),
        out_shape=jax.ShapeDtypeStruct((m, d), out_dtype),
        compiler_params=_params(("parallel", "parallel")),
        name="merge",
    )(a_ret, a_att, a_hg, w_r, w_a, w_h, z, z, z)


def _rot(x, cos, sin, half):
    up = pltpu.roll(x, LANE - half, 1)
    dn = pltpu.roll(x, half, 1)
    lane = lax.broadcasted_iota(jnp.int32, x.shape, 1)
    return x * cos + jnp.where(lane < half, up, dn) * sin


def _rope_tables(pos, freqs, width):
    ang = pos.astype(F32)[:, None] * freqs[None, :]
    cos, sin = jnp.cos(ang), jnp.sin(ang)
    t = pos.shape[0]
    pad_c = jnp.ones((t, LANE - width), F32)
    pad_s = jnp.zeros((t, LANE - width), F32)
    return (jnp.concatenate([cos, cos, pad_c], axis=1), jnp.concatenate([-sin, sin, pad_s], axis=1))


def _prelude_kernel(zq_ref, zk_ref, zv_ref, ziq_ref, zik_ref, ziw_ref, ca_ref, sa_ref, ci_ref, si_ref,
                    qn_ref, kn_ref, ig_ref, ib_ref,
                    aq_ref, ak_ref, av_ref, akb_ref, avb_ref, iq_ref, ik_ref, ikb_ref, iw_ref):
    ca, sa, ci, si = ca_ref[...], sa_ref[...], ci_ref[...], si_ref[...]

    def head_norm(x, g):
        return x * lax.rsqrt(jnp.mean(x * x, axis=-1, keepdims=True) + EPS) * g

    for h in range(ATT_HEADS):
        sl = slice(h * HD, (h + 1) * HD)
        aq_ref[:, sl] = _rot(head_norm(zq_ref[:, sl], qn_ref[...]), ca, sa, HD // 2).astype(aq_ref.dtype)
    for h in range(ATT_KV_HEADS):
        sl = slice(h * HD, (h + 1) * HD)
        k = _rot(head_norm(zk_ref[:, sl], kn_ref[...]), ca, sa, HD // 2)
        ak_ref[:, sl] = k
        akb_ref[:, sl] = k.astype(akb_ref.dtype)
    v = zv_ref[...]
    av_ref[...] = v
    avb_ref[...] = v.astype(avb_ref.dtype)
    for h in range(IDX_HEADS):
        sl = slice(h * HD, (h + 1) * HD)
        iq_ref[:, sl] = (_rot(ziq_ref[:, sl], ci, si, IDX_ROPE_DIM // 2) * (IDX_DIM ** -0.5)).astype(iq_ref.dtype)
    x = zik_ref[...]
    mu = jnp.mean(x, axis=-1, keepdims=True)
    var = jnp.mean(jnp.square(x - mu), axis=-1, keepdims=True)
    ik = _rot((x - mu) * lax.rsqrt(var + EPS) * ig_ref[...] + ib_ref[...], ci, si, IDX_ROPE_DIM // 2)
    ik_ref[...] = ik
    ikb_ref[...] = ik.astype(ikb_ref.dtype)
    iw_ref[...] = ziw_ref[...] * (IDX_HEADS ** -0.5)


def _prelude(z, m, l, tabs, q_norm, k_norm, idx_g, idx_b, t_len):
    _, dst, padded, _, _ = _segments()
    tr = min(m, 256)
    nt = t_len // tr if t_len >= tr else 1
    lowp = BF16 if tr >= 16 else F32
    aw, kvw, iw = padded['a_q'], padded['a_k'], padded['i_q']

    def zspec(name):
        w = padded[name]
        return pl.BlockSpec((tr, w), lambda i: (i, dst[name] // w))

    tab_spec = pl.BlockSpec((tr, LANE), lambda i: (i % nt, 0))
    vec_spec = pl.BlockSpec((None, 1, HD), lambda i: (l, 0, 0))
    row = lambda w: pl.BlockSpec((tr, w), lambda i: (i, 0))
    shp = lambda w, dt: jax.ShapeDtypeStruct((m, w), dt)
    vec = lambda a: a.reshape(a.shape[0], 1, HD)
    return pl.pallas_call(
        _prelude_kernel,
        grid=(m // tr,),
        in_specs=[zspec('a_q'), zspec('a_k'), zspec('a_v'), zspec('i_q'), zspec('i_k'), zspec('i_w'),
                  tab_spec, tab_spec, tab_spec, tab_spec, vec_spec, vec_spec, vec_spec, vec_spec],
        out_specs=[row(aw), row(kvw), row(kvw), row(kvw), row(kvw), row(iw), row(HD), row(HD), row(LANE)],
        out_shape=[shp(aw, lowp), shp(kvw, F32), shp(kvw, F32), shp(kvw, lowp), shp(kvw, lowp),
                   shp(iw, lowp), shp(HD, F32), shp(HD, lowp), shp(LANE, F32)],
        compiler_params=_params(("parallel",)),
        name="attn_prelude",
    )(z, z, z, z, z, z, *tabs, vec(q_norm), vec(k_norm), vec(idx_g), vec(idx_b))


def _ret_gamma_log(h):
    return math.log(1.0 - 2.0 ** (-5.0 - h))


def _retention_kernel(lg_ref, q_ref, k_ref, v_ref, g_ref, cos_ref, sin_ref, nrm_ref, o_ref, st_ref,
                      s_scr, intra_scr, qdec_scr, kdec_scr, *, nc):
    c = pl.program_id(2)
    cr = q_ref.shape[0]
    lg = lg_ref[:, 0:1]

    @pl.when(c == 0)
    def _():
        s_scr[...] = jnp.zeros_like(s_scr)
        ti = lax.broadcasted_iota(jnp.int32, (cr, cr), 0)
        si = lax.broadcasted_iota(jnp.int32, (cr, cr), 1)
        diff = (ti - si).astype(F32)
        intra_scr[...] = jnp.where(diff >= 0, jnp.exp(lg * jnp.maximum(diff, 0.0)), 0.0)
        tcol = lax.broadcasted_iota(jnp.int32, (cr, LANE), 0).astype(F32)
        qdec_scr[...] = jnp.exp(lg * (tcol + 1.0))
        kdec_scr[...] = jnp.exp(lg * (cr - 1.0 - tcol))

    cos, sin = cos_ref[...], sin_ref[...]
    q = _rot(q_ref[...], cos, sin, RET_DK // 2)
    k = _rot(k_ref[...], cos, sin, RET_DK // 2) * (RET_DK ** -0.5)
    v = v_ref[...]
    s_dec = jnp.exp(lg * cr)
    s = s_scr[...]
    a = _bdot(q, k, _NT) * intra_scr[...]
    o = _bdot(a, v) + _bdot(q * qdec_scr[...], s)
    s_new = s * s_dec + _bdot(k * kdec_scr[...], v, _TN)
    s_scr[...] = s_new
    y = o * lax.rsqrt(jnp.mean(o * o, axis=-1, keepdims=True) + EPS) * nrm_ref[...]
    o_ref[...] = (y * _silu(g_ref[...])).astype(o_ref.dtype)

    @pl.when(c == nc - 1)
    def _():
        st_ref[...] = s_new


def _retention(z, l, cos, sin, ret_norm, b, t):
    _, dst, _, _, _ = _segments()
    cr = min(t, 256)
    nc = t // cr
    lg = jnp.broadcast_to(
        jnp.asarray([_ret_gamma_log(h) for h in range(RET_HEADS)], F32)[:, None, None], (RET_HEADS, 1, LANE))

    def zspec(name):
        return pl.BlockSpec((cr, HD), lambda bi, h, c: (bi * nc + c, dst[name] // HD + h))

    tab = pl.BlockSpec((cr, LANE), lambda bi, h, c: (c, 0))
    return pl.pallas_call(
        functools.partial(_retention_kernel, nc=nc),
        grid=(b, RET_HEADS, nc),
        in_specs=[pl.BlockSpec((None, 1, LANE), lambda bi, h, c: (h, 0, 0)),
                  zspec('r_q'), zspec('r_k'), zspec('r_v'), zspec('r_g'), tab, tab,
                  pl.BlockSpec((None, 1, HD), lambda bi, h, c: (l, 0, h))],
        out_specs=[pl.BlockSpec((cr, HD), lambda bi, h, c: (bi * nc + c, h)),
                   pl.BlockSpec((None, None, RET_DK, RET_DV), lambda bi, h, c: (bi, h, 0, 0))],
        out_shape=[jax.ShapeDtypeStruct((z.shape[0], RET_HEADS * RET_DV), BF16),
                   jax.ShapeDtypeStruct((b, RET_HEADS, RET_DK, RET_DV), F32)],
        scratch_shapes=[pltpu.VMEM((RET_DK, RET_DV), F32), pltpu.VMEM((cr, cr), F32),
                        pltpu.VMEM((cr, LANE), F32), pltpu.VMEM((cr, LANE), F32)],
        compiler_params=_params(("parallel", "parallel", "arbitrary")),
        name="retention",
    )(lg, z, z, z, z, cos, sin, ret_norm.reshape(ret_norm.shape[0], 1, -1))


def _hgrn_gates(fa, lb):
    log_f = jnp.minimum(fa, 0.0) - jnp.log1p(jnp.exp(-jnp.abs(fa))) + jnp.log1p(lb * jnp.exp(-fa))
    hk = (1.0 - lb) * jax.nn.sigmoid(-fa)
    return log_f, hk


def _split3(x):
    hi = x.astype(BF16)
    r1 = x - hi.astype(F32)
    mid = r1.astype(BF16)
    lo = (r1 - mid.astype(F32)).astype(BF16)
    return hi, mid, lo


def _gla_state_step(q, k, v, b, st):
    b_last = b[GLA_CHUNK - 1:GLA_CHUNK, :]
    o = _bdot(q * jnp.exp(b), st, _NT)
    kd = k * jnp.exp(b_last - b)
    return o, st * jnp.exp(b_last) + _bdot(v, kd, _TN)


def _gla_intra_anchored(q, k, v, b):
    c = GLA_CHUNK
    nsub = c // GLA_SUB
    ti = lax.broadcasted_iota(jnp.int32, (c, c), 0)
    si = lax.broadcasted_iota(jnp.int32, (c, c), 1)
    anchors = [jnp.zeros((1, HD), F32)] + [b[i * GLA_SUB - 1:i * GLA_SUB, :] for i in range(1, nsub)]
    m_rows = jnp.concatenate([jnp.broadcast_to(m, (GLA_SUB, HD)) for m in anchors], axis=0)
    qs = q * jnp.exp(b - m_rows)
    a = jnp.zeros((c, c), F32)
    for i in range(nsub):
        ks = k * jnp.exp(jnp.minimum(anchors[i] - b, GLA_SAFE_SPAN))
        blk = (ti // GLA_SUB == i) & (si <= ti)
        a = a + jnp.where(blk, _bdot(qs, ks, _NT), 0.0)
    return _bdot(a, v)


def _gla_intra_pairwise(q, k, v, b):
    c = GLA_CHUNK
    nsub = c // GLA_SUB
    ti = lax.broadcasted_iota(jnp.int32, (c, c), 0)
    si = lax.broadcasted_iota(jnp.int32, (c, c), 1)
    a_off = jnp.zeros((c, c), F32)
    for i in range(1, nsub):
        m = b[i * GLA_SUB - 1:i * GLA_SUB, :]
        qs = q * jnp.exp(jnp.minimum(b - m, 0.0))
        ks = k * jnp.exp(jnp.minimum(m - b, 0.0))
        blk = (ti // GLA_SUB == i) & (si < i * GLA_SUB)
        a_off = a_off + jnp.where(blk, _bdot(qs, ks, _NT), 0.0)
    o = _bdot(a_off, v)
    rows = lax.broadcasted_iota(jnp.int32, (GLA_SUB, 1), 0)
    diag = []
    for i in range(nsub):
        sl = slice(i * GLA_SUB, (i + 1) * GLA_SUB)
        qi, ki, vi, bi = q[sl], k[sl], v[sl], b[sl]
        oi = jnp.zeros((GLA_SUB, HD), F32)
        for s in range(GLA_SUB):
            d = jnp.exp(jnp.minimum(bi - bi[s:s + 1], 0.0)) * qi * ki[s:s + 1]
            w = jnp.where(rows >= s, jnp.sum(d, axis=-1, keepdims=True), 0.0)
            oi = oi + w * vi[s:s + 1]
        diag.append(oi)
    return o + jnp.concatenate(diag, axis=0)


def _gla_kernel(f_ref, q_ref, i_ref, g_ref, lb_ref, nrm_ref, o_ref, st_ref, s_scr, *, nc, n_inner):
    c = pl.program_id(2)
    blk = f_ref.shape[0]

    @pl.when(c == 0)
    def _():
        s_scr[...] = jnp.zeros_like(s_scr)

    log_f, hk = _hgrn_gates(f_ref[...], lb_ref[...])
    hq = _silu(q_ref[...])
    v = i_ref[...]
    ti = lax.broadcasted_iota(jnp.int32, (blk, blk), 0)
    si = lax.broadcasted_iota(jnp.int32, (blk, blk), 1)
    tri = ((ti >= si) & (ti // GLA_CHUNK == si // GLA_CHUNK)).astype(BF16)
    hi, mid, lo = _split3(log_f)
    b = (jnp.dot(tri, hi, preferred_element_type=F32) + jnp.dot(tri, mid, preferred_element_type=F32)
         + jnp.dot(tri, lo, preferred_element_type=F32))
    sub_sum = jnp.sum(log_f.reshape(blk // GLA_SUB, GLA_SUB, HD), axis=1)
    anchored_ok = jnp.min(sub_sum) >= -GLA_SAFE_SPAN

    def run(intra):
        st = s_scr[...]
        outs = []
        for ci in range(n_inner):
            sl = slice(ci * GLA_CHUNK, (ci + 1) * GLA_CHUNK)
            o_inter, st = _gla_state_step(hq[sl], hk[sl], v[sl], b[sl], st)
            outs.append(o_inter + intra(hq[sl], hk[sl], v[sl], b[sl]))
        s_scr[...] = st
        o = jnp.concatenate(outs, axis=0)
        y = o * lax.rsqrt(jnp.mean(o * o, axis=-1, keepdims=True) + EPS) * nrm_ref[...]
        o_ref[...] = (y * jax.nn.sigmoid(g_ref[...])).astype(o_ref.dtype)

    lax.cond(anchored_ok, lambda: run(_gla_intra_anchored), lambda: run(_gla_intra_pairwise))

    @pl.when(c == nc - 1)
    def _():
        st_ref[...] = s_scr[...].T


def _gla(z, l, lb, hg_norm, b, t):
    _, dst, _, _, _ = _segments()
    blk = min(t, 256)
    assert blk % GLA_CHUNK == 0
    nc = t // blk

    def zspec(name):
        return pl.BlockSpec((blk, HD), lambda bi, h, c: (bi * nc + c, dst[name] // HD + h))

    vec = lambda: pl.BlockSpec((None, 1, HD), lambda bi, h, c: (l, 0, h))
    return pl.pallas_call(
        functools.partial(_gla_kernel, nc=nc, n_inner=blk // GLA_CHUNK),
        grid=(b, HG_HEADS, nc),
        in_specs=[zspec('h_f'), zspec('h_q'), zspec('h_i'), zspec('h_g'), vec(), vec()],
        out_specs=[pl.BlockSpec((blk, HD), lambda bi, h, c: (bi * nc + c, h)),
                   pl.BlockSpec((None, None, HG_DK, HG_DV), lambda bi, h, c: (bi, h, 0, 0))],
        out_shape=[jax.ShapeDtypeStruct((z.shape[0], HG_HEADS * HG_DV), BF16),
                   jax.ShapeDtypeStruct((b, HG_HEADS, HG_DK, HG_DV), F32)],
        scratch_shapes=[pltpu.VMEM((HG_DV, HG_DK), F32)],
        compiler_params=_params(("parallel", "parallel", "arbitrary")),
        name="hgrn2",
    )(z, z, z, z, lb.reshape(lb.shape[0], 1, -1), hg_norm.reshape(hg_norm.shape[0], 1, -1))


def _to_col(row):
    n = row.shape[1]
    eye = lax.broadcasted_iota(jnp.int32, (n, n), 0) == lax.broadcasted_iota(jnp.int32, (n, n), 1)
    return jnp.sum(jnp.where(eye, row, 0.0), axis=1, keepdims=True)


def _decode_rec_kernel(rq_ref, rk_ref, rv_ref, rg_ref, hf_ref, hq_ref, hi_ref, hg_ref,
                       cos_ref, sin_ref, rn_ref, hn_ref, lb_ref, sr_ref, sh_ref,
                       or_ref, oh_ref, nr_ref, nh_ref):
    cos, sin = cos_ref[...], sin_ref[...]

    def rms(o, g):
        return o * lax.rsqrt(jnp.mean(o * o, axis=-1, keepdims=True) + EPS) * g

    for h in range(RET_HEADS):
        sl = slice(h * HD, (h + 1) * HD)
        gamma = math.exp(_ret_gamma_log(h))
        q = _rot(rq_ref[:, sl], cos, sin, RET_DK // 2)
        k = _rot(rk_ref[:, sl], cos, sin, RET_DK // 2) * (RET_DK ** -0.5)
        v = rv_ref[:, sl]
        s = sr_ref[h]
        o = jnp.sum(q * k, axis=-1, keepdims=True) * v + jnp.sum(_to_col(q * gamma) * s, axis=0, keepdims=True)
        nr_ref[h] = s * gamma + _to_col(k) * v
        or_ref[:, sl] = rms(o, rn_ref[:, sl]) * _silu(rg_ref[:, sl])

    for h in range(HG_HEADS):
        sl = slice(h * HD, (h + 1) * HD)
        log_f, k = _hgrn_gates(hf_ref[:, sl], lb_ref[:, sl])
        q = _silu(hq_ref[:, sl])
        v = hi_ref[:, sl]
        s = sh_ref[h]
        eb = jnp.exp(log_f)
        o = jnp.sum(q * k, axis=-1, keepdims=True) * v + jnp.sum(_to_col(q * eb) * s, axis=0, keepdims=True)
        nh_ref[h] = s * _to_col(eb) + _to_col(k) * v
        oh_ref[:, sl] = rms(o, hn_ref[:, sl]) * jax.nn.sigmoid(hg_ref[:, sl])


def _decode_rec(z, l, cos, sin, ret_norm, hg_norm, lb, state_ret, state_hgrn):
    _, dst, padded, _, _ = _segments()
    nb = z.shape[0]

    z = z.reshape(nb, 1, z.shape[1])

    def zspec(name):
        w = padded[name]
        return pl.BlockSpec((None, 1, w), lambda bi: (bi, 0, dst[name] // w))

    one = lambda w: pl.BlockSpec((1, w), lambda bi: (0, 0))
    vec = lambda a: pl.BlockSpec((None, 1, a.shape[-1]), lambda bi: (l, 0, 0))
    st = lambda a: pl.BlockSpec((None, None) + a.shape[2:], lambda bi: (l, bi, 0, 0, 0))
    st_out = lambda a: pl.BlockSpec((None,) + a.shape[2:], lambda bi: (bi, 0, 0, 0))
    rw, hw = RET_HEADS * RET_DV, HG_HEADS * HG_DV
    r3 = lambda a: a.reshape(a.shape[0], 1, -1)
    return pl.pallas_call(
        _decode_rec_kernel,
        grid=(nb,),
        in_specs=[zspec('r_q'), zspec('r_k'), zspec('r_v'), zspec('r_g'),
                  zspec('h_f'), zspec('h_q'), zspec('h_i'), zspec('h_g'),
                  one(LANE), one(LANE), vec(ret_norm), vec(hg_norm), vec(lb), st(state_ret), st(state_hgrn)],
        out_specs=[pl.BlockSpec((None, 1, rw), lambda bi: (bi, 0, 0)),
                   pl.BlockSpec((None, 1, hw), lambda bi: (bi, 0, 0)),
                   st_out(state_ret), st_out(state_hgrn)],
        out_shape=[jax.ShapeDtypeStruct((nb, 1, rw), F32), jax.ShapeDtypeStruct((nb, 1, hw), F32),
                   jax.ShapeDtypeStruct(state_ret.shape[1:], F32), jax.ShapeDtypeStruct(state_hgrn.shape[1:], F32)],
        compiler_params=_params(("arbitrary",)),
        name="decode_recurrent",
    )(z, z, z, z, z, z, z, z, cos, sin, r3(ret_norm), r3(hg_norm), r3(lb), state_ret, state_hgrn)


def _sort_key(score):
    bits = lax.bitcast_convert_type(score + 0.0, jnp.int32)
    return bits ^ ((bits >> 31) & jnp.int32(0x7FFFFFFF))


def _count(mask):
    return jnp.sum(mask.astype(F32), axis=-1, keepdims=True)


def _nth_largest_key(count_ge, n_sel, shape):
    return _nth_largest_keys([count_ge], n_sel, shape)[0]


def _nth_largest_keys(count_fns, n_sel, shape):
    zero = jnp.zeros(shape, jnp.int32)
    los = tuple(jnp.where(f(zero) >= n_sel, 0, INT_MIN).astype(jnp.int32) for f in count_fns)

    def body(i, los):
        bit = jnp.left_shift(jnp.int32(1), 30 - i)
        return tuple(jnp.where(f(lo | bit) >= n_sel, lo | bit, lo) for f, lo in zip(count_fns, los))

    return lax.fori_loop(0, 31, body, los)


def _tie_bound(count_eq_below, need, nbits, shape):
    def body(i, j):
        cand = j | jnp.left_shift(jnp.int32(1), nbits - 1 - i)
        return jnp.where(count_eq_below(cand) < need, cand, j)

    return lax.fori_loop(0, nbits, body, jnp.zeros(shape, jnp.int32))


def _dsa_prompt_body(length, aq_ref, iq_ref, iw_ref, k_ref, v_ref, ik_ref, o_ref, key_scr, sel_scr, n_sel):
    j = pl.program_id(1)
    tq = aq_ref.shape[0]
    ik = ik_ref[0:length, :]
    score = jnp.zeros((tq, length), F32)
    for h in range(IDX_HEADS):
        s = lax.dot_general(iq_ref[:, h * HD:(h + 1) * HD], ik, _NT, preferred_element_type=F32)
        score = score + jnp.maximum(s, 0.0) * iw_ref[:, h:h + 1]
    q_pos = j * tq + lax.broadcasted_iota(jnp.int32, (tq, 1), 0)
    col = lax.broadcasted_iota(jnp.int32, (tq, length), 1)
    visible = col <= q_pos
    key_scr[:, 0:length] = _sort_key(jnp.where(visible, score, NEG_BIG))

    rows = tq // SEARCH_GROUPS
    group_count = lambda r: (lambda c: _count(key_scr[r * rows:(r + 1) * rows, 0:length] >= c))
    tau = jnp.concatenate(
        _nth_largest_keys([group_count(r) for r in range(SEARCH_GROUPS)], n_sel, (rows, 1)), axis=0)
    key = key_scr[:, 0:length]
    gt = key > tau
    eq = key == tau
    need = n_sel - _count(gt)
    spare = jnp.max(_count(eq & visible) - need) > 0.0
    bound = lax.cond(
        spare,
        lambda: _tie_bound(lambda c: _count((key_scr[:, 0:length] == tau) & (col < c)), need,
                           max(1, length.bit_length()), (tq, 1)),
        lambda: jnp.full((tq, 1), length, jnp.int32))
    sel_scr[:, 0:length] = ((gt | (eq & (col <= bound))) & visible).astype(F32)

    group = ATT_HEADS // ATT_KV_HEADS
    scale = ATT_HEAD_DIM ** -0.5
    for n in range(ATT_KV_HEADS):
        kn = k_ref[0:length, n * HD:(n + 1) * HD]
        vn = v_ref[0:length, n * HD:(n + 1) * HD]
        for g in range(group):
            sl = slice((n * group + g) * HD, (n * group + g + 1) * HD)
            s = lax.dot_general(aq_ref[:, sl], kn, _NT, preferred_element_type=F32) * scale
            s = jnp.where(sel_scr[:, 0:length] > 0.0, s, NEG_BIG)
            m = jnp.max(s, axis=-1, keepdims=True)
            p = jnp.exp(s - m)
            o = _bdot(p, vn) / jnp.sum(p, axis=-1, keepdims=True)
            o_ref[:, sl] = o.astype(o_ref.dtype)


def _dsa_prompt_kernel(*refs, n_sel, lengths):
    j = pl.program_id(1)
    tq = refs[0].shape[0]
    prev = 0
    for length in lengths:
        @pl.when((j >= prev // tq) & (j < length // tq))
        def _(length=length):
            _dsa_prompt_body(length, *refs, n_sel)
        prev = length


def _dsa_prompt(aq, iq, iw, kb, vb, ikb, b, t, mt):
    tq = min(Q_BLOCK, t)
    nq = t // tq
    n_sel = min(TOPK_MAX, t // 4)
    step = min(t, 512)
    lengths = tuple(range(step, t + 1, step))
    assert t % step == 0 and step % tq == 0 and step >= n_sel
    qrow = lambda w: pl.BlockSpec((tq, w), lambda bi, j: (bi * nq + j, 0))
    krow = lambda w: pl.BlockSpec((t, w), lambda bi, j: (bi, 0))
    return pl.pallas_call(
        functools.partial(_dsa_prompt_kernel, n_sel=n_sel, lengths=lengths),
        grid=(b, nq),
        in_specs=[qrow(aq.shape[1]), qrow(iq.shape[1]), qrow(LANE),
                  krow(kb.shape[1]), krow(vb.shape[1]), krow(HD)],
        out_specs=qrow(aq.shape[1]),
        out_shape=jax.ShapeDtypeStruct((mt, aq.shape[1]), BF16),
        scratch_shapes=[pltpu.VMEM((tq, t), jnp.int32), pltpu.VMEM((tq, t), F32)],
        compiler_params=_params(("parallel", "arbitrary")),
        name="dsa_prompt",
    )(aq, iq, iw, kb, vb, ikb)


def _dec_score_kernel(pt_ref, iq_ref, iw_ref, ikn_ref, *refs):
    pages = refs[:PAGES_PER_STEP]
    sc_ref, new_ref = refs[PAGES_PER_STEP:]
    iq = iq_ref[...].astype(BF16)
    iw = iw_ref[...]
    ik = jnp.concatenate([p[...].astype(BF16) for p in pages], axis=0)
    s = lax.dot_general(iq, ik, _NT, preferred_element_type=F32)
    sc_ref[...] = jnp.sum(jnp.maximum(s, 0.0) * iw, axis=0, keepdims=True)

    @pl.when(pl.program_id(1) == 0)
    def _():
        sn = lax.dot_general(iq, jnp.broadcast_to(ikn_ref[...], (8, HD)).astype(BF16), _NT,
                             preferred_element_type=F32)[:, 0:1]
        new_ref[...] = jnp.broadcast_to(jnp.sum(jnp.maximum(sn, 0.0) * iw, axis=0, keepdims=True), (1, LANE))


def _dec_scores(page_table, iq, iw, ik_new, cache_ik, l):
    nb, n_pages = page_table.shape
    steps = n_pages // PAGES_PER_STEP
    ih = iq.shape[1]
    page = lambda r: pl.BlockSpec((None, None, PAGE_SIZE, HD),
                                  lambda bi, p, pt: (l, pt[bi, p * PAGES_PER_STEP + r], 0, 0))
    grid_spec = pltpu.PrefetchScalarGridSpec(
        num_scalar_prefetch=1,
        grid=(nb, steps),
        in_specs=[pl.BlockSpec((None, ih, HD), lambda bi, p, pt: (bi, 0, 0)),
                  pl.BlockSpec((None, ih, 1), lambda bi, p, pt: (bi, 0, 0)),
                  pl.BlockSpec((None, 1, HD), lambda bi, p, pt: (bi, 0, 0))]
                 + [page(r) for r in range(PAGES_PER_STEP)],
        out_specs=[pl.BlockSpec((None, 1, PAGES_PER_STEP * PAGE_SIZE), lambda bi, p, pt: (bi, 0, p)),
                   pl.BlockSpec((None, 1, LANE), lambda bi, p, pt: (bi, 0, 0))],
    )
    return pl.pallas_call(
        _dec_score_kernel,
        grid_spec=grid_spec,
        out_shape=[jax.ShapeDtypeStruct((nb, 1, n_pages * PAGE_SIZE), F32),
                   jax.ShapeDtypeStruct((nb, 1, LANE), F32)],
        compiler_params=_params(("parallel", "arbitrary")),
        name="dec_scores",
    )(page_table, iq, iw, ik_new, *([cache_ik] * PAGES_PER_STEP))


def _dec_select_kernel(sc_ref, new_ref, tau_ref, bnd_ref, seln_ref, *, n_sel, nbits):
    key = _sort_key(sc_ref[...])
    key_new = _sort_key(new_ref[...])[:, 0:1]
    nb, s_len = key.shape
    idx = lax.broadcasted_iota(jnp.int32, key.shape, 1)
    tau = _nth_largest_key(lambda c: _count(key >= c) + (key_new >= c).astype(F32), n_sel, (nb, 1))
    eq = key == tau
    need = n_sel - _count(key > tau) - (key_new > tau).astype(F32)
    n_eq = _count(eq)
    bound = lax.cond(
        jnp.max(n_eq - need) > 0.0,
        lambda: _tie_bound(lambda c: _count(eq & (idx < c)), need, nbits, (nb, 1)),
        lambda: jnp.full((nb, 1), s_len, jnp.int32))
    sel_new = (key_new > tau) | ((key_new == tau) & (n_eq < need))
    tau_ref[...] = jnp.broadcast_to(tau, tau_ref.shape)
    bnd_ref[...] = jnp.broadcast_to(bound, bnd_ref.shape)
    seln_ref[...] = jnp.broadcast_to(sel_new.astype(F32), seln_ref.shape)


def _dec_select(scores, score_new, n_sel):
    nb, s_len = scores.shape
    out = lambda dt: jax.ShapeDtypeStruct((nb, LANE), dt)
    return pl.pallas_call(
        functools.partial(_dec_select_kernel, n_sel=n_sel, nbits=s_len.bit_length()),
        out_shape=[out(jnp.int32), out(jnp.int32), out(F32)],
        compiler_params=pltpu.CompilerParams(vmem_limit_bytes=VMEM_LIMIT),
        name="dec_select",
    )(scores, score_new)


def _dec_attn_kernel(pt_ref, tau_ref, bnd_ref, seln_ref, sc_ref, q_ref, kn_ref, vn_ref, *refs, steps):
    kp = refs[:PAGES_PER_STEP]
    vp = refs[PAGES_PER_STEP:2 * PAGES_PER_STEP]
    o_ref, m_scr, l_scr, acc_scr = refs[2 * PAGES_PER_STEP:]
    p_id = pl.program_id(1)
    chunk = PAGES_PER_STEP * PAGE_SIZE
    scale = ATT_HEAD_DIM ** -0.5

    @pl.when(p_id == 0)
    def _():
        m_scr[...] = jnp.full_like(m_scr, NEG_BIG)
        l_scr[...] = jnp.zeros_like(l_scr)
        acc_scr[...] = jnp.zeros_like(acc_scr)

    tau = tau_ref[:, 0:1]
    bound = bnd_ref[:, 0:1]
    key_c = _sort_key(sc_ref[...])
    idx_c = p_id * chunk + lax.broadcasted_iota(jnp.int32, key_c.shape, 1)
    sel = (key_c > tau) | ((key_c == tau) & (idx_c <= bound))
    q = q_ref[...].astype(BF16)
    heads = lambda pages, n: jnp.concatenate(
        [r[pl.ds(n, PAGE_SIZE, stride=ATT_KV_HEADS), :].astype(BF16) for r in pages], axis=0)
    s = sum(lax.dot_general(q[:, n * HD:(n + 1) * HD], heads(kp, n), _NT, preferred_element_type=F32)
            for n in range(ATT_KV_HEADS)) * scale
    s = jnp.where(sel, s, NEG_BIG)
    m_old = m_scr[:, 0:1]
    m_new = jnp.maximum(m_old, jnp.max(s, axis=-1, keepdims=True))
    alpha = jnp.exp(m_old - m_new)
    p = jnp.where(sel, jnp.exp(s - m_new), 0.0)
    l_new = alpha * l_scr[:, 0:1] + jnp.sum(p, axis=-1, keepdims=True)
    pb = p.astype(BF16)
    acc_new = alpha * acc_scr[...] + jnp.concatenate(
        [jnp.dot(pb, heads(vp, n), preferred_element_type=F32) for n in range(ATT_KV_HEADS)], axis=1)
    m_scr[...] = jnp.broadcast_to(m_new, m_scr.shape)
    l_scr[...] = jnp.broadcast_to(l_new, l_scr.shape)
    acc_scr[...] = acc_new

    @pl.when(p_id == steps - 1)
    def _():
        sel_new = seln_ref[:, 0:1] > 0.0
        qf = q_ref[...]
        s_new = jnp.sum(qf * kn_ref[...], axis=-1, keepdims=True) * scale
        s_new = jnp.where(sel_new, s_new, NEG_BIG)
        m_fin = jnp.maximum(m_new, s_new)
        a2 = jnp.exp(m_new - m_fin)
        p_new = jnp.where(sel_new, jnp.exp(s_new - m_fin), 0.0)
        l_fin = a2 * l_new + p_new
        acc_fin = (a2 * acc_new + p_new * vn_ref[...]) / l_fin
        group = ATT_HEADS // ATT_KV_HEADS
        for h in range(ATT_HEADS):
            n = h // group
            o_ref[:, h * HD:(h + 1) * HD] = acc_fin[h:h + 1, n * HD:(n + 1) * HD]


def _dec_attn(page_table, scores, score_new, q_bd, k_new, v_new, cache_k, cache_v, l):
    nb, n_pages = page_table.shape
    steps = n_pages // PAGES_PER_STEP
    s_len = n_pages * PAGE_SIZE
    chunk = PAGES_PER_STEP * PAGE_SIZE
    n_sel = min(TOPK_MAX, (s_len + 1) // 4)
    tau, bound, sel_new = _dec_select(scores.reshape(nb, s_len), score_new.reshape(nb, LANE), n_sel)
    row = lambda a: a.reshape(nb, 1, LANE)
    row_spec = pl.BlockSpec((None, 1, LANE), lambda bi, p, pt: (bi, 0, 0))
    kvw = ATT_KV_HEADS * HD
    hp = q_bd.shape[1]
    page = lambda r: pl.BlockSpec((None, None, PAGE_SIZE * ATT_KV_HEADS, HD),
                                  lambda bi, p, pt: (l, pt[bi, p * PAGES_PER_STEP + r], 0, 0))
    grid_spec = pltpu.PrefetchScalarGridSpec(
        num_scalar_prefetch=1,
        grid=(nb, steps),
        in_specs=[row_spec, row_spec, row_spec,
                  pl.BlockSpec((None, 1, chunk), lambda bi, p, pt: (bi, 0, p)),
                  pl.BlockSpec((None, hp, kvw), lambda bi, p, pt: (bi, 0, 0)),
                  pl.BlockSpec((None, 1, kvw), lambda bi, p, pt: (bi, 0, 0)),
                  pl.BlockSpec((None, 1, kvw), lambda bi, p, pt: (bi, 0, 0))]
                 + [page(r) for r in range(PAGES_PER_STEP)] * 2,
        out_specs=pl.BlockSpec((None, 1, ATT_HEADS * HD), lambda bi, p, pt: (bi, 0, 0)),
        scratch_shapes=[pltpu.VMEM((hp, LANE), F32), pltpu.VMEM((hp, LANE), F32),
                        pltpu.VMEM((hp, kvw), F32)],
    )
    return pl.pallas_call(
        functools.partial(_dec_attn_kernel, steps=steps),
        grid_spec=grid_spec,
        out_shape=jax.ShapeDtypeStruct((nb, 1, ATT_HEADS * HD), F32),
        compiler_params=_params(("parallel", "arbitrary")),
        name="dec_attn",
    )(page_table, row(tau), row(bound), row(sel_new), scores, q_bd, k_new, v_new,
      *([cache_k] * PAGES_PER_STEP), *([cache_v] * PAGES_PER_STEP)).reshape(nb, ATT_HEADS * HD)


def _ffn(x, xb, ssq, w1, w3, w2, l, tm, next_gain):
    tm_up = 2 * tm if (x.shape[0] // tm) % 2 == 0 else tm
    g, w2b = _matmul(xb, [w1, w3], l, tm=tm_up, epilogue='swiglu', out_dtype=BF16, ssq=ssq, cast_src=(w2, l))
    return _matmul(g, [w2b[None]], 0, tm=tm, epilogue='resid', res=x, scale=0.5, next_gain=next_gain)


def _token_tiles(n_tokens):
    n_tiles = max(1, n_tokens // 1024)
    tm = -(-n_tokens // (16 * n_tiles)) * 16
    return tm, n_tiles


def kernel(x_prompt, x_sample, state_ret, state_hgrn, cache_k, cache_v, cache_idx_k, page_table, ffn1_norm, ffn1_w1, ffn1_w3, ffn1_w2, mix_norm, w_in, ret_norm, q_norm, k_norm, idx_k_g, idx_k_b, hg_lb_raw, hg_norm, w_up_ret, w_up_att, w_up_hg, w_out, ffn2_norm, ffn2_w1, ffn2_w3, ffn2_w2):
    b, t, d = x_prompt.shape
    nb = x_sample.shape[0]
    depth = w_in.shape[0]
    kvw = ATT_KV_HEADS * ATT_HEAD_DIM
    group = ATT_HEADS // ATT_KV_HEADS

    lb_soft = jax.nn.softmax(hg_lb_raw.astype(F32), axis=0)
    lb_all = jnp.cumsum(lb_soft, axis=0) - lb_soft[0]
    cast = lambda w: w.astype(BF16)
    w_up_ret, w_up_att, w_up_hg, w_out = cast(w_up_ret), cast(w_up_att), cast(w_up_hg), cast(w_out)

    pos_p = jnp.arange(t, dtype=jnp.int32)
    pos_s = jnp.full((nb,), PAST_LEN, jnp.int32)
    ret_f = 1.0 / (ROPE_THETA ** jnp.linspace(0.0, 1.0, RET_DK // 2, dtype=F32))
    att_f = ROPE_THETA ** (-jnp.arange(0, ATT_HEAD_DIM, 2, dtype=F32) / ATT_HEAD_DIM)
    idx_f = ROPE_THETA ** (-jnp.arange(0, IDX_ROPE_DIM, 2, dtype=F32) / IDX_ROPE_DIM)
    tabs_p = _rope_tables(pos_p, att_f, ATT_HEAD_DIM) + _rope_tables(pos_p, idx_f, IDX_ROPE_DIM)
    tabs_s = _rope_tables(pos_s, att_f, ATT_HEAD_DIM) + _rope_tables(pos_s, idx_f, IDX_ROPE_DIM)
    ret_tab_p = _rope_tables(pos_p, ret_f, RET_DK)
    ret_tab_s = _rope_tables(pos_s[:1], ret_f, RET_DK)

    ck = cache_k.reshape(cache_k.shape[:2] + (PAGE_SIZE * ATT_KV_HEADS, HD))
    cv = cache_v.reshape(cache_v.shape[:2] + (PAGE_SIZE * ATT_KV_HEADS, HD))

    mp = b * t
    tm, n_tiles = _token_tiles(mp + nb)
    mt = tm * n_tiles
    x = jnp.concatenate([x_prompt.reshape(mp, d), x_sample.reshape(nb, d), jnp.zeros((mt - mp - nb, d), F32)])

    def with_tail(a, rows):
        tail = jnp.concatenate([rows.astype(a.dtype), jnp.zeros((mt - mp - nb, a.shape[1]), a.dtype)])
        return lax.dynamic_update_slice(a, tail, (mp, 0))

    outs = {n: [] for n in ('rp', 'rs', 'hp', 'hs', 'kp', 'vp', 'ip', 'ks', 'vs', 'is')}
    xb, ssq = _row_prep(x, ffn1_norm, 0)
    for l in range(depth):
        x, xb, ssq = _ffn(x, xb, ssq, ffn1_w1, ffn1_w3, ffn1_w2, l, tm, (mix_norm, l))
        z = _in_proj(xb, ssq, w_in, l, 2 * tm if n_tiles % 2 == 0 else tm)

        aq, ak, av, akb, avb, iq, ik, ikb, iw = _prelude(z, mp, l, tabs_p, q_norm, k_norm, idx_k_g, idx_k_b, t)
        a_ret, r_p = _retention(z, l, ret_tab_p[0], ret_tab_p[1], ret_norm, b, t)
        a_hg, h_p = _gla(z, l, lb_all, hg_norm, b, t)
        a_att = _dsa_prompt(aq, iq, iw, akb, avb, ikb, b, t, mt)
        outs['rp'].append(r_p); outs['hp'].append(h_p)
        outs['kp'].append(ak.reshape(b, t, ATT_KV_HEADS, ATT_HEAD_DIM))
        outs['vp'].append(av.reshape(b, t, ATT_KV_HEADS, ATT_HEAD_DIM))
        outs['ip'].append(ik.reshape(b, t, IDX_DIM))

        zs = z[mp:mp + nb]
        aq, ak, av, _, _, iq, ik, _, iw = _prelude(zs, nb, l, tabs_s, q_norm, k_norm, idx_k_g, idx_k_b, 1)
        s_ret, s_hg, r_s, h_s = _decode_rec(zs, l, ret_tab_s[0], ret_tab_s[1], ret_norm, hg_norm, lb_all,
                                            state_ret, state_hgrn)
        scores, score_new = _dec_scores(page_table, iq.reshape(nb, IDX_HEADS, IDX_DIM),
                                        iw[:, :IDX_HEADS].reshape(nb, IDX_HEADS, 1),
                                        ik.reshape(nb, 1, IDX_DIM), cache_idx_k, l)
        qh = aq.reshape(nb, ATT_KV_HEADS, group, ATT_HEAD_DIM)
        q_bd = jnp.concatenate(
            [jnp.concatenate([qh[:, n] if m == n else jnp.zeros_like(qh[:, n]) for m in range(ATT_KV_HEADS)], axis=-1)
             for n in range(ATT_KV_HEADS)]
            + [jnp.zeros((nb, max(0, 16 - ATT_HEADS), kvw), aq.dtype)], axis=1)
        s_att = _dec_attn(page_table, scores, score_new, q_bd, ak.reshape(nb, 1, kvw), av.reshape(nb, 1, kvw),
                          ck, cv, l)
        outs['rs'].append(r_s); outs['hs'].append(h_s)
        outs['ks'].append(ak.reshape(nb, 1, ATT_KV_HEADS, ATT_HEAD_DIM))
        outs['vs'].append(av.reshape(nb, 1, ATT_KV_HEADS, ATT_HEAD_DIM))
        outs['is'].append(ik.reshape(nb, 1, IDX_DIM))

        merged = _merge(with_tail(a_ret, s_ret.reshape(nb, -1)), with_tail(a_att, s_att),
                        with_tail(a_hg, s_hg.reshape(nb, -1)), w_up_ret, w_up_att, w_up_hg, z, l, BF16, tm)
        x, xb, ssq = _matmul(merged, [w_out], l, tm=tm, epilogue='resid', res=x, scale=1.0,
                             next_gain=(ffn2_norm, l))
        if l + 1 < depth:
            x, xb, ssq = _ffn(x, xb, ssq, ffn2_w1, ffn2_w3, ffn2_w2, l, tm, (ffn1_norm, l + 1))
        else:
            x = _ffn(x, xb, ssq, ffn2_w1, ffn2_w3, ffn2_w2, l, tm, None)

    st = lambda n: jnp.stack(outs[n])
    return (x[:mp].reshape(b, t, d), x[mp:mp + nb].reshape(nb, 1, d),
            st('rp').astype(state_ret.dtype), st('rs').astype(state_ret.dtype),
            st('hp').astype(state_hgrn.dtype), st('hs').astype(state_hgrn.dtype),
            st('kp'), st('vp'), st('ip'), st('ks'), st('vs'), st('is'))
```

```python
import functools
import math

import numpy as np
import jax
import jax.numpy as jnp
from jax import lax
from jax.experimental import pallas as pl
from jax.experimental.pallas import tpu as pltpu

D_MODEL = 4096
BATCH = 4
SEQ = 2048
DEPTH = 2
DEC_BATCH = 8
DEC_SEQ = 1
PAST_LEN = 16384
PAGE_SIZE = 128

RET_HEADS = 8
RET_DK = 128
RET_DV = 128
ATT_HEADS = 8
ATT_KV_HEADS = 2
ATT_HEAD_DIM = 128
IDX_HEADS = 16
IDX_DIM = 128
IDX_ROPE_DIM = 64
TOPK_MAX = 256
Q_BLOCK = 128
HG_HEADS = 8
HG_DK = 128
HG_DV = 128
D_FF = 11008
ROPE_THETA = 10000.0
EPS = 1e-6
NEG_BIG = -1e30

F32 = jnp.float32
BF16 = jnp.bfloat16
LANE = 128
HD = 128
INT_MIN = -(2 ** 31)
VMEM_LIMIT = 56 * 1024 * 1024
GLA_CHUNK = 64
GLA_SUB = 16
SEARCH_GROUPS = 4
SEARCH_UNROLL = 6
GLA_SAFE_SPAN = 60.0
PAGES_PER_STEP = 16

_NT = (((1,), (1,)), ((), ()))
_TN = (((0,), (0,)), ((), ()))


def _params(sem):
    return pltpu.CompilerParams(dimension_semantics=sem, vmem_limit_bytes=VMEM_LIMIT)


def _bdot(a, b, dims=None):
    a = a.astype(BF16)
    b = b.astype(BF16)
    if dims is None:
        return jnp.dot(a, b, preferred_element_type=F32)
    return lax.dot_general(a, b, dims, preferred_element_type=F32)


def _silu(x):
    return x * jax.nn.sigmoid(x)


def _segments():
    ret_qk = RET_HEADS * RET_DK
    ret_w = RET_HEADS * RET_DV
    att_w = ATT_HEADS * ATT_HEAD_DIM
    kv_w = ATT_KV_HEADS * ATT_HEAD_DIM
    hg_k = HG_HEADS * HG_DK
    hg_w = HG_HEADS * HG_DV
    names = ['r_q', 'r_k', 'r_v', 'r_g', 'a_q', 'a_k', 'a_v', 'i_q', 'i_k', 'i_w',
             'h_f', 'h_q', 'h_i', 'h_g', 'g_ret', 'g_att', 'g_hg']
    widths = [ret_qk, ret_qk, ret_w, ret_w, att_w, kv_w, kv_w, IDX_HEADS * IDX_DIM, IDX_DIM, IDX_HEADS,
              hg_k, hg_k, hg_w, hg_w, D_MODEL, D_MODEL, D_MODEL]
    src = {}
    off = 0
    for n, w in zip(names, widths):
        src[n] = (off, w)
        off += w
    padded = {n: -(-w // LANE) * LANE for n, w in zip(names, widths)}
    order = ['g_ret', 'g_att', 'g_hg'] + sorted(
        [n for n in names if not n.startswith('g_')], key=lambda n: -padded[n])
    dst = {}
    off = 0
    for n in order:
        dst[n] = off
        off += padded[n]
    total = -(-off // 256) * 256
    for n in order:
        assert dst[n] % padded[n] == 0 or n.startswith('g_'), (n, dst[n], padded[n])
    return src, dst, padded, order, total


def _row_prep_kernel(x_ref, g_ref, xb_ref, ssq_ref):
    x = x_ref[...]
    xb_ref[...] = (x * g_ref[...]).astype(xb_ref.dtype)
    ssq_ref[...] = jnp.broadcast_to(jnp.sum(x * x, axis=-1, keepdims=True), ssq_ref.shape)


def _row_tile(m, cap):
    best = None
    for t in range(16, min(m, cap) + 1, 16):
        if m % t == 0:
            best = t
    return best if best is not None else m


def _row_prep(x, g_all, l):
    m, d = x.shape
    tr = _row_tile(m, 512)
    return pl.pallas_call(
        _row_prep_kernel,
        grid=(m // tr,),
        in_specs=[pl.BlockSpec((tr, d), lambda i: (i, 0)),
                  pl.BlockSpec((None, 1, d), lambda i: (l, 0, 0))],
        out_specs=[pl.BlockSpec((tr, d), lambda i: (i, 0)), pl.BlockSpec((tr, LANE), lambda i: (i, 0))],
        out_shape=[jax.ShapeDtypeStruct((m, d), BF16), jax.ShapeDtypeStruct((m, LANE), F32)],
        compiler_params=_params(("parallel",)),
        name="row_prep",
    )(x, g_all.reshape(g_all.shape[0], 1, d))


def _mm_kernel(*refs, n_w, epilogue, scale, row_norm, feed_norm, cast_rows):
    it = iter(refs)
    lhs_ref = next(it)
    w_refs = [next(it) for _ in range(n_w)]
    res_ref = next(it) if epilogue == 'resid' else None
    ssq_ref = next(it) if row_norm else None
    gain_ref = next(it) if feed_norm else None
    cast_in_ref = next(it) if cast_rows else None
    out_ref = next(it)
    xb_ref, ssq_out_ref = (next(it), next(it)) if feed_norm else (None, None)
    if cast_rows:
        next(it)[...] = cast_in_ref[...].astype(BF16)

    lhs = lhs_ref[...].astype(BF16)
    vals = [jnp.dot(lhs, w[...].astype(BF16), preferred_element_type=F32) for w in w_refs]
    if row_norm:
        r = lax.rsqrt(ssq_ref[:, 0:1] * (1.0 / lhs_ref.shape[1]) + EPS)
        vals = [v * r for v in vals]
    if epilogue == 'swiglu':
        out = _silu(vals[0]) * vals[1]
    elif epilogue == 'resid':
        out = res_ref[...] + scale * vals[0]
    else:
        out = vals[0]
    out_ref[...] = out.astype(out_ref.dtype)
    if feed_norm:
        xb_ref[...] = (out * gain_ref[...]).astype(xb_ref.dtype)
        part = jnp.broadcast_to(jnp.sum(out * out, axis=-1, keepdims=True), ssq_out_ref.shape)
        j = pl.program_id(1)

        @pl.when(j == 0)
        def _():
            ssq_out_ref[...] = part

        @pl.when(j > 0)
        def _():
            ssq_out_ref[...] += part


def _pick(n, cands):
    for c in cands:
        if n % c == 0:
            return c
    return n


def _matmul(lhs, ws, l, *, tm, epilogue='plain', res=None, scale=1.0, out_dtype=F32, tn=256, single_lhs=None,
            ssq=None, next_gain=None, cast_src=None):
    assert next_gain is None or cast_src is None
    m, kdim = lhs.shape
    n = ws[0].shape[-1]
    tn = _pick(n, (tn, 256, 128))
    assert m % tm == 0 and n % tn == 0
    if single_lhs is None:
        single_lhs = tm * kdim * lhs.dtype.itemsize > (12 << 20)
    lhs_mode = dict(pipeline_mode=pl.Buffered(1)) if single_lhs else {}
    in_specs = [pl.BlockSpec((tm, kdim), lambda i, j: (i, 0), **lhs_mode)]
    in_specs += [pl.BlockSpec((None, kdim, tn), lambda i, j: (l, 0, j)) for _ in ws]
    args = [lhs] + list(ws)
    tile = pl.BlockSpec((tm, tn), lambda i, j: (i, j))
    rows = pl.BlockSpec((tm, LANE), lambda i, j: (i, 0))
    if epilogue == 'resid':
        in_specs.append(tile)
        args.append(res)
    if ssq is not None:
        in_specs.append(rows)
        args.append(ssq)
    out_specs, out_shape = tile, jax.ShapeDtypeStruct((m, n), out_dtype)
    if next_gain is not None:
        gains, gl = next_gain
        in_specs.append(pl.BlockSpec((None, 1, tn), lambda i, j: (gl, 0, j)))
        args.append(gains.reshape(gains.shape[0], 1, n))
        out_specs = [tile, tile, rows]
        out_shape = [out_shape, jax.ShapeDtypeStruct((m, n), BF16), jax.ShapeDtypeStruct((m, LANE), F32)]
    cast_rows = 0
    if cast_src is not None:
        src, cl = cast_src
        steps, nj = (m // tm) * (n // tn), n // tn
        cast_rows = src.shape[1] // steps
        assert cast_rows * steps == src.shape[1] and cast_rows % 16 == 0
        in_specs.append(pl.BlockSpec((None, cast_rows, src.shape[2]), lambda i, j: (cl, i * nj + j, 0)))
        args.append(src)
        out_specs = [out_specs, pl.BlockSpec((cast_rows, src.shape[2]), lambda i, j: (i * nj + j, 0))]
        out_shape = [out_shape, jax.ShapeDtypeStruct(src.shape[1:], BF16)]
    return pl.pallas_call(
        functools.partial(_mm_kernel, n_w=len(ws), epilogue=epilogue, scale=scale,
                          row_norm=ssq is not None, feed_norm=next_gain is not None, cast_rows=cast_rows),
        grid=(m // tm, n // tn),
        in_specs=in_specs,
        out_specs=out_specs,
        out_shape=out_shape,
        compiler_params=_params(("parallel", "arbitrary")),
        name="mm_" + epilogue,
    )(*args)


def _in_proj_kernel(start_ref, lhs_ref, w_ref, ssq_ref, out_ref):
    w = w_ref[0].astype(BF16)
    acc = lax.dot_general(lhs_ref[...], w, _NT, preferred_element_type=F32)
    out_ref[...] = acc * lax.rsqrt(ssq_ref[:, 0:1] * (1.0 / lhs_ref.shape[1]) + EPS)


def _in_proj(xb, ssq, w_in, l, tm):
    src, dst, padded, order, total = _segments()
    depth, kdim, n_in = w_in.shape
    m = xb.shape[0]
    tn = 256
    col_src = np.zeros((total,), np.int64)
    for n in order:
        col_src[dst[n]:dst[n] + padded[n]] = src[n][0] + np.arange(padded[n])
    start = col_src[::tn].copy()
    assert (col_src.reshape(-1, tn) == start[:, None] + np.arange(tn)).all(), "tile is not one source range"
    assert (start % 8 == 0).all() and (start + tn <= n_in).all()
    grid_spec = pltpu.PrefetchScalarGridSpec(
        num_scalar_prefetch=1,
        grid=(m // tm, total // tn),
        in_specs=[pl.BlockSpec((tm, kdim), lambda i, j, start: (i, 0), pipeline_mode=pl.Buffered(1)),
                  pl.BlockSpec((pl.Element(1), pl.Element(tn), pl.Element(kdim)),
                               lambda i, j, start: (l, start[j] * 8, 0)),
                  pl.BlockSpec((tm, LANE), lambda i, j, start: (i, 0))],
        out_specs=pl.BlockSpec((tm, tn), lambda i, j, start: (i, j)),
    )
    return pl.pallas_call(
        _in_proj_kernel,
        grid_spec=grid_spec,
        out_shape=jax.ShapeDtypeStruct((m, total), F32),
        compiler_params=_params(("parallel", "arbitrary")),
        name="in_proj",
    )(jnp.asarray(start // 8, jnp.int32), xb, jnp.swapaxes(w_in, 1, 2), ssq)


def _merge_kernel(ar_ref, aa_ref, ah_ref, wr_ref, wa_ref, wh_ref, gr_ref, ga_ref, gh_ref, o_ref):
    u_r = _bdot(ar_ref[...], wr_ref[...])
    u_a = _bdot(aa_ref[...], wa_ref[...])
    u_h = _bdot(ah_ref[...], wh_ref[...])
    out = (jax.nn.sigmoid(gr_ref[...]) * u_r + jax.nn.sigmoid(ga_ref[...]) * u_a
           + jax.nn.sigmoid(gh_ref[...]) * u_h)
    o_ref[...] = out.astype(o_ref.dtype)


def _merge(a_ret, a_att, a_hg, w_r, w_a, w_h, z, l, out_dtype, tm):
    m = a_ret.shape[0]
    d = w_r.shape[-1]
    tn = _pick(d, (256, 128))
    nb = d // tn
    lhs_spec = lambda a: pl.BlockSpec((tm, a.shape[1]), lambda i, j: (i, 0))
    w_spec = lambda w: pl.BlockSpec((None, w.shape[1], tn), lambda i, j: (l, 0, j))
    gate_spec = lambda g: pl.BlockSpec((tm, tn), lambda i, j: (i, g * nb + j))
    return pl.pallas_call(
        _merge_kernel,
        grid=(m // tm, nb),
        in_specs=[lhs_spec(a_ret), lhs_spec(a_att), lhs_spec(a_hg), w_spec(w_r), w_spec(w_a), w_spec(w_h),
                  gate_spec(0), gate_spec(1), gate_spec(2)],
        out_specs=pl.BlockSpec((tm, tn), lambda i, j: (i, j)),
        out_shape=jax.ShapeDtypeStruct((m, d), out_dtype),
        compiler_params=_params(("parallel", "parallel")),
        name="merge",
    )(a_ret, a_att, a_hg, w_r, w_a, w_h, z, z, z)


def _rot(x, cos, sin, half):
    up = pltpu.roll(x, LANE - half, 1)
    dn = pltpu.roll(x, half, 1)
    lane = lax.broadcasted_iota(jnp.int32, x.shape, 1)
    return x * cos + jnp.where(lane < half, up, dn) * sin


def _rope_tables(pos, freqs, width):
    ang = pos.astype(F32)[:, None] * freqs[None, :]
    cos, sin = jnp.cos(ang), jnp.sin(ang)
    t = pos.shape[0]
    pad_c = jnp.ones((t, LANE - width), F32)
    pad_s = jnp.zeros((t, LANE - width), F32)
    return (jnp.concatenate([cos, cos, pad_c], axis=1), jnp.concatenate([-sin, sin, pad_s], axis=1))


def _prelude_kernel(zq_ref, zk_ref, zv_ref, ziq_ref, zik_ref, ziw_ref, ca_ref, sa_ref, ci_ref, si_ref,
                    qn_ref, kn_ref, ig_ref, ib_ref,
                    aq_ref, ak_ref, av_ref, akb_ref, avb_ref, iq_ref, ik_ref, ikb_ref, iw_ref):
    ca, sa, ci, si = ca_ref[...], sa_ref[...], ci_ref[...], si_ref[...]

    def head_norm(x, g):
        return x * lax.rsqrt(jnp.mean(x * x, axis=-1, keepdims=True) + EPS) * g

    for h in range(ATT_HEADS):
        sl = slice(h * HD, (h + 1) * HD)
        aq_ref[:, sl] = _rot(head_norm(zq_ref[:, sl], qn_ref[...]), ca, sa, HD // 2).astype(aq_ref.dtype)
    for h in range(ATT_KV_HEADS):
        sl = slice(h * HD, (h + 1) * HD)
        k = _rot(head_norm(zk_ref[:, sl], kn_ref[...]), ca, sa, HD // 2)
        ak_ref[:, sl] = k
        akb_ref[:, sl] = k.astype(akb_ref.dtype)
    v = zv_ref[...]
    av_ref[...] = v
    avb_ref[...] = v.astype(avb_ref.dtype)
    for h in range(IDX_HEADS):
        sl = slice(h * HD, (h + 1) * HD)
        iq_ref[:, sl] = (_rot(ziq_ref[:, sl], ci, si, IDX_ROPE_DIM // 2) * (IDX_DIM ** -0.5)).astype(iq_ref.dtype)
    x = zik_ref[...]
    mu = jnp.mean(x, axis=-1, keepdims=True)
    var = jnp.mean(jnp.square(x - mu), axis=-1, keepdims=True)
    ik = _rot((x - mu) * lax.rsqrt(var + EPS) * ig_ref[...] + ib_ref[...], ci, si, IDX_ROPE_DIM // 2)
    ik_ref[...] = ik
    ikb_ref[...] = ik.astype(ikb_ref.dtype)
    iw_ref[...] = ziw_ref[...] * (IDX_HEADS ** -0.5)


def _prelude(z, m, l, tabs, q_norm, k_norm, idx_g, idx_b, t_len):
    _, dst, padded, _, _ = _segments()
    tr = min(m, 256)
    nt = t_len // tr if t_len >= tr else 1
    lowp = BF16 if tr >= 16 else F32
    aw, kvw, iw = padded['a_q'], padded['a_k'], padded['i_q']

    def zspec(name):
        w = padded[name]
        return pl.BlockSpec((tr, w), lambda i: (i, dst[name] // w))

    tab_spec = pl.BlockSpec((tr, LANE), lambda i: (i % nt, 0))
    vec_spec = pl.BlockSpec((None, 1, HD), lambda i: (l, 0, 0))
    row = lambda w: pl.BlockSpec((tr, w), lambda i: (i, 0))
    shp = lambda w, dt: jax.ShapeDtypeStruct((m, w), dt)
    vec = lambda a: a.reshape(a.shape[0], 1, HD)
    return pl.pallas_call(
        _prelude_kernel,
        grid=(m // tr,),
        in_specs=[zspec('a_q'), zspec('a_k'), zspec('a_v'), zspec('i_q'), zspec('i_k'), zspec('i_w'),
                  tab_spec, tab_spec, tab_spec, tab_spec, vec_spec, vec_spec, vec_spec, vec_spec],
        out_specs=[row(aw), row(kvw), row(kvw), row(kvw), row(kvw), row(iw), row(HD), row(HD), row(LANE)],
        out_shape=[shp(aw, lowp), shp(kvw, F32), shp(kvw, F32), shp(kvw, lowp), shp(kvw, lowp),
                   shp(iw, lowp), shp(HD, F32), shp(HD, lowp), shp(LANE, F32)],
        compiler_params=_params(("parallel",)),
        name="attn_prelude",
    )(z, z, z, z, z, z, *tabs, vec(q_norm), vec(k_norm), vec(idx_g), vec(idx_b))


def _ret_gamma_log(h):
    return math.log(1.0 - 2.0 ** (-5.0 - h))


def _retention_kernel(lg_ref, q_ref, k_ref, v_ref, g_ref, cos_ref, sin_ref, nrm_ref, o_ref, st_ref,
                      s_scr, intra_scr, qdec_scr, kdec_scr, *, nc):
    c = pl.program_id(2)
    cr = q_ref.shape[0]
    lg = lg_ref[:, 0:1]

    @pl.when(c == 0)
    def _():
        s_scr[...] = jnp.zeros_like(s_scr)
        ti = lax.broadcasted_iota(jnp.int32, (cr, cr), 0)
        si = lax.broadcasted_iota(jnp.int32, (cr, cr), 1)
        diff = (ti - si).astype(F32)
        intra_scr[...] = jnp.where(diff >= 0, jnp.exp(lg * jnp.maximum(diff, 0.0)), 0.0)
        tcol = lax.broadcasted_iota(jnp.int32, (cr, LANE), 0).astype(F32)
        qdec_scr[...] = jnp.exp(lg * (tcol + 1.0))
        kdec_scr[...] = jnp.exp(lg * (cr - 1.0 - tcol))

    cos, sin = cos_ref[...], sin_ref[...]
    q = _rot(q_ref[...], cos, sin, RET_DK // 2)
    k = _rot(k_ref[...], cos, sin, RET_DK // 2) * (RET_DK ** -0.5)
    v = v_ref[...]
    s_dec = jnp.exp(lg * cr)
    s = s_scr[...]
    a = _bdot(q, k, _NT) * intra_scr[...]
    o = _bdot(a, v) + _bdot(q * qdec_scr[...], s)
    s_new = s * s_dec + _bdot(k * kdec_scr[...], v, _TN)
    s_scr[...] = s_new
    y = o * lax.rsqrt(jnp.mean(o * o, axis=-1, keepdims=True) + EPS) * nrm_ref[...]
    o_ref[...] = (y * _silu(g_ref[...])).astype(o_ref.dtype)

    @pl.when(c == nc - 1)
    def _():
        st_ref[...] = s_new


def _retention(z, l, cos, sin, ret_norm, b, t):
    _, dst, _, _, _ = _segments()
    cr = min(t, 256)
    nc = t // cr
    lg = jnp.broadcast_to(
        jnp.asarray([_ret_gamma_log(h) for h in range(RET_HEADS)], F32)[:, None, None], (RET_HEADS, 1, LANE))

    def zspec(name):
        return pl.BlockSpec((cr, HD), lambda bi, h, c: (bi * nc + c, dst[name] // HD + h))

    tab = pl.BlockSpec((cr, LANE), lambda bi, h, c: (c, 0))
    return pl.pallas_call(
        functools.partial(_retention_kernel, nc=nc),
        grid=(b, RET_HEADS, nc),
        in_specs=[pl.BlockSpec((None, 1, LANE), lambda bi, h, c: (h, 0, 0)),
                  zspec('r_q'), zspec('r_k'), zspec('r_v'), zspec('r_g'), tab, tab,
                  pl.BlockSpec((None, 1, HD), lambda bi, h, c: (l, 0, h))],
        out_specs=[pl.BlockSpec((cr, HD), lambda bi, h, c: (bi * nc + c, h)),
                   pl.BlockSpec((None, None, RET_DK, RET_DV), lambda bi, h, c: (bi, h, 0, 0))],
        out_shape=[jax.ShapeDtypeStruct((z.shape[0], RET_HEADS * RET_DV), BF16),
                   jax.ShapeDtypeStruct((b, RET_HEADS, RET_DK, RET_DV), F32)],
        scratch_shapes=[pltpu.VMEM((RET_DK, RET_DV), F32), pltpu.VMEM((cr, cr), F32),
                        pltpu.VMEM((cr, LANE), F32), pltpu.VMEM((cr, LANE), F32)],
        compiler_params=_params(("parallel", "parallel", "arbitrary")),
        name="retention",
    )(lg, z, z, z, z, cos, sin, ret_norm.reshape(ret_norm.shape[0], 1, -1))


def _hgrn_gates(fa, lb):
    log_f = jnp.minimum(fa, 0.0) - jnp.log1p(jnp.exp(-jnp.abs(fa))) + jnp.log1p(lb * jnp.exp(-fa))
    hk = (1.0 - lb) * jax.nn.sigmoid(-fa)
    return log_f, hk


def _split3(x):
    hi = x.astype(BF16)
    r1 = x - hi.astype(F32)
    mid = r1.astype(BF16)
    lo = (r1 - mid.astype(F32)).astype(BF16)
    return hi, mid, lo


def _gla_state_step(q, k, v, b, st):
    b_last = b[GLA_CHUNK - 1:GLA_CHUNK, :]
    o = _bdot(q * jnp.exp(b), st, _NT)
    kd = k * jnp.exp(b_last - b)
    return o, st * jnp.exp(b_last) + _bdot(v, kd, _TN)


def _gla_intra_anchored(q, k, v, b):
    c = GLA_CHUNK
    nsub = c // GLA_SUB
    ti = lax.broadcasted_iota(jnp.int32, (c, c), 0)
    si = lax.broadcasted_iota(jnp.int32, (c, c), 1)
    anchors = [jnp.zeros((1, HD), F32)] + [b[i * GLA_SUB - 1:i * GLA_SUB, :] for i in range(1, nsub)]
    m_rows = jnp.concatenate([jnp.broadcast_to(m, (GLA_SUB, HD)) for m in anchors], axis=0)
    qs = q * jnp.exp(b - m_rows)
    a = jnp.zeros((c, c), F32)
    for i in range(nsub):
        ks = k * jnp.exp(jnp.minimum(anchors[i] - b, GLA_SAFE_SPAN))
        blk = (ti // GLA_SUB == i) & (si <= ti)
        a = a + jnp.where(blk, _bdot(qs, ks, _NT), 0.0)
    return _bdot(a, v)


def _gla_intra_pairwise(q, k, v, b):
    c = GLA_CHUNK
    nsub = c // GLA_SUB
    ti = lax.broadcasted_iota(jnp.int32, (c, c), 0)
    si = lax.broadcasted_iota(jnp.int32, (c, c), 1)
    a_off = jnp.zeros((c, c), F32)
    for i in range(1, nsub):
        m = b[i * GLA_SUB - 1:i * GLA_SUB, :]
        qs = q * jnp.exp(jnp.minimum(b - m, 0.0))
        ks = k * jnp.exp(jnp.minimum(m - b, 0.0))
        blk = (ti // GLA_SUB == i) & (si < i * GLA_SUB)
        a_off = a_off + jnp.where(blk, _bdot(qs, ks, _NT), 0.0)
    o = _bdot(a_off, v)
    rows = lax.broadcasted_iota(jnp.int32, (GLA_SUB, 1), 0)
    diag = []
    for i in range(nsub):
        sl = slice(i * GLA_SUB, (i + 1) * GLA_SUB)
        qi, ki, vi, bi = q[sl], k[sl], v[sl], b[sl]
        oi = jnp.zeros((GLA_SUB, HD), F32)
        for s in range(GLA_SUB):
            d = jnp.exp(jnp.minimum(bi - bi[s:s + 1], 0.0)) * qi * ki[s:s + 1]
            w = jnp.where(rows >= s, jnp.sum(d, axis=-1, keepdims=True), 0.0)
            oi = oi + w * vi[s:s + 1]
        diag.append(oi)
    return o + jnp.concatenate(diag, axis=0)


def _gla_kernel(f_ref, q_ref, i_ref, g_ref, lb_ref, nrm_ref, o_ref, st_ref, s_scr, *, nc, n_inner):
    c = pl.program_id(2)
    blk = f_ref.shape[0]

    @pl.when(c == 0)
    def _():
        s_scr[...] = jnp.zeros_like(s_scr)

    log_f, hk = _hgrn_gates(f_ref[...], lb_ref[...])
    hq = _silu(q_ref[...])
    v = i_ref[...]
    ti = lax.broadcasted_iota(jnp.int32, (blk, blk), 0)
    si = lax.broadcasted_iota(jnp.int32, (blk, blk), 1)
    tri = ((ti >= si) & (ti // GLA_CHUNK == si // GLA_CHUNK)).astype(BF16)
    hi, mid, lo = _split3(log_f)
    b = (jnp.dot(tri, hi, preferred_element_type=F32) + jnp.dot(tri, mid, preferred_element_type=F32)
         + jnp.dot(tri, lo, preferred_element_type=F32))
    sub_sum = jnp.sum(log_f.reshape(blk // GLA_SUB, GLA_SUB, log_f.shape[1]), axis=1)
    anchored_ok = jnp.min(sub_sum) >= -GLA_SAFE_SPAN
    n_heads = log_f.shape[1] // HD

    def run(intra):
        for hh in range(n_heads):
            hs = slice(hh * HD, (hh + 1) * HD)
            st = s_scr[hh]
            outs = []
            for ci in range(n_inner):
                sl = slice(ci * GLA_CHUNK, (ci + 1) * GLA_CHUNK)
                qc, kc, vc, bc = hq[sl, hs], hk[sl, hs], v[sl, hs], b[sl, hs]
                o_inter, st = _gla_state_step(qc, kc, vc, bc, st)
                outs.append(o_inter + intra(qc, kc, vc, bc))
            s_scr[hh] = st
            o = jnp.concatenate(outs, axis=0)
            y = o * lax.rsqrt(jnp.mean(o * o, axis=-1, keepdims=True) + EPS) * nrm_ref[:, hs]
            o_ref[:, hs] = (y * jax.nn.sigmoid(g_ref[:, hs])).astype(o_ref.dtype)

    lax.cond(anchored_ok, lambda: run(_gla_intra_anchored), lambda: run(_gla_intra_pairwise))

    @pl.when(c == nc - 1)
    def _():
        for hh in range(n_heads):
            st_ref[hh] = s_scr[hh].T


def _gla(z, l, lb, hg_norm, b, t):
    _, dst, _, _, _ = _segments()
    blk = min(t, 256)
    assert blk % GLA_CHUNK == 0
    nc = t // blk

    hps = 2 if HG_HEADS % 2 == 0 else 1
    w = hps * HD

    def zspec(name):
        assert dst[name] % w == 0
        return pl.BlockSpec((blk, w), lambda bi, h, c: (bi * nc + c, dst[name] // w + h))

    vec = lambda: pl.BlockSpec((None, 1, w), lambda bi, h, c: (l, 0, h))
    return pl.pallas_call(
        functools.partial(_gla_kernel, nc=nc, n_inner=blk // GLA_CHUNK),
        grid=(b, HG_HEADS // hps, nc),
        in_specs=[zspec('h_f'), zspec('h_q'), zspec('h_i'), zspec('h_g'), vec(), vec()],
        out_specs=[pl.BlockSpec((blk, w), lambda bi, h, c: (bi * nc + c, h)),
                   pl.BlockSpec((None, hps, HG_DK, HG_DV), lambda bi, h, c: (bi, h, 0, 0))],
        out_shape=[jax.ShapeDtypeStruct((z.shape[0], HG_HEADS * HG_DV), BF16),
                   jax.ShapeDtypeStruct((b, HG_HEADS, HG_DK, HG_DV), F32)],
        scratch_shapes=[pltpu.VMEM((hps, HG_DV, HG_DK), F32)],
        compiler_params=_params(("parallel", "parallel", "arbitrary")),
        name="hgrn2",
    )(z, z, z, z, lb.reshape(lb.shape[0], 1, -1), hg_norm.reshape(hg_norm.shape[0], 1, -1))


def _to_col(row):
    n = row.shape[1]
    eye = lax.broadcasted_iota(jnp.int32, (n, n), 0) == lax.broadcasted_iota(jnp.int32, (n, n), 1)
    return jnp.sum(jnp.where(eye, row, 0.0), axis=1, keepdims=True)


def _decode_rec_kernel(rq_ref, rk_ref, rv_ref, rg_ref, hf_ref, hq_ref, hi_ref, hg_ref,
                       cos_ref, sin_ref, rn_ref, hn_ref, lb_ref, sr_ref, sh_ref,
                       or_ref, oh_ref, nr_ref, nh_ref):
    cos, sin = cos_ref[...], sin_ref[...]

    def rms(o, g):
        return o * lax.rsqrt(jnp.mean(o * o, axis=-1, keepdims=True) + EPS) * g

    for h in range(RET_HEADS):
        sl = slice(h * HD, (h + 1) * HD)
        gamma = math.exp(_ret_gamma_log(h))
        q = _rot(rq_ref[:, sl], cos, sin, RET_DK // 2)
        k = _rot(rk_ref[:, sl], cos, sin, RET_DK // 2) * (RET_DK ** -0.5)
        v = rv_ref[:, sl]
        s = sr_ref[h]
        o = jnp.sum(q * k, axis=-1, keepdims=True) * v + jnp.sum(_to_col(q * gamma) * s, axis=0, keepdims=True)
        nr_ref[h] = s * gamma + _to_col(k) * v
        or_ref[:, sl] = rms(o, rn_ref[:, sl]) * _silu(rg_ref[:, sl])

    for h in range(HG_HEADS):
        sl = slice(h * HD, (h + 1) * HD)
        log_f, k = _hgrn_gates(hf_ref[:, sl], lb_ref[:, sl])
        q = _silu(hq_ref[:, sl])
        v = hi_ref[:, sl]
        s = sh_ref[h]
        eb = jnp.exp(log_f)
        o = jnp.sum(q * k, axis=-1, keepdims=True) * v + jnp.sum(_to_col(q * eb) * s, axis=0, keepdims=True)
        nh_ref[h] = s * _to_col(eb) + _to_col(k) * v
        oh_ref[:, sl] = rms(o, hn_ref[:, sl]) * jax.nn.sigmoid(hg_ref[:, sl])


def _decode_rec(z, l, cos, sin, ret_norm, hg_norm, lb, state_ret, state_hgrn):
    _, dst, padded, _, _ = _segments()
    nb = z.shape[0]

    z = z.reshape(nb, 1, z.shape[1])

    def zspec(name):
        w = padded[name]
        return pl.BlockSpec((None, 1, w), lambda bi: (bi, 0, dst[name] // w))

    one = lambda w: pl.BlockSpec((1, w), lambda bi: (0, 0))
    vec = lambda a: pl.BlockSpec((None, 1, a.shape[-1]), lambda bi: (l, 0, 0))
    st = lambda a: pl.BlockSpec((None, None) + a.shape[2:], lambda bi: (l, bi, 0, 0, 0))
    st_out = lambda a: pl.BlockSpec((None,) + a.shape[2:], lambda bi: (bi, 0, 0, 0))
    rw, hw = RET_HEADS * RET_DV, HG_HEADS * HG_DV
    r3 = lambda a: a.reshape(a.shape[0], 1, -1)
    return pl.pallas_call(
        _decode_rec_kernel,
        grid=(nb,),
        in_specs=[zspec('r_q'), zspec('r_k'), zspec('r_v'), zspec('r_g'),
                  zspec('h_f'), zspec('h_q'), zspec('h_i'), zspec('h_g'),
                  one(LANE), one(LANE), vec(ret_norm), vec(hg_norm), vec(lb), st(state_ret), st(state_hgrn)],
        out_specs=[pl.BlockSpec((None, 1, rw), lambda bi: (bi, 0, 0)),
                   pl.BlockSpec((None, 1, hw), lambda bi: (bi, 0, 0)),
                   st_out(state_ret), st_out(state_hgrn)],
        out_shape=[jax.ShapeDtypeStruct((nb, 1, rw), F32), jax.ShapeDtypeStruct((nb, 1, hw), F32),
                   jax.ShapeDtypeStruct(state_ret.shape[1:], F32), jax.ShapeDtypeStruct(state_hgrn.shape[1:], F32)],
        compiler_params=_params(("arbitrary",)),
        name="decode_recurrent",
    )(z, z, z, z, z, z, z, z, cos, sin, r3(ret_norm), r3(hg_norm), r3(lb), state_ret, state_hgrn)


def _sort_key(score):
    bits = lax.bitcast_convert_type(score + 0.0, jnp.int32)
    return bits ^ ((bits >> 31) & jnp.int32(0x7FFFFFFF))


def _count(mask):
    return jnp.sum(mask.astype(F32), axis=-1, keepdims=True)


def _nth_largest_key(count_ge, n_sel, shape):
    return _nth_largest_keys([count_ge], n_sel, shape)[0]


def _nth_largest_keys(count_fns, n_sel, shape):
    zero = jnp.zeros(shape, jnp.int32)
    los = tuple(jnp.where(f(zero) >= n_sel, 0, INT_MIN).astype(jnp.int32) for f in count_fns)

    def body(i, los):
        bit = jnp.left_shift(jnp.int32(1), 30 - i)
        return tuple(jnp.where(f(lo | bit) >= n_sel, lo | bit, lo) for f, lo in zip(count_fns, los))

    los = body(0, los)
    return lax.fori_loop(1, 31, body, los, unroll=SEARCH_UNROLL)


def _tie_bound(count_eq_below, need, nbits, shape):
    def body(i, j):
        cand = j | jnp.left_shift(jnp.int32(1), nbits - 1 - i)
        return jnp.where(count_eq_below(cand) < need, cand, j)

    return lax.fori_loop(0, nbits, body, jnp.zeros(shape, jnp.int32))


def _dsa_prompt_body(length, aq_ref, iq_ref, iw_ref, k_ref, v_ref, ik_ref, o_ref, key_scr, sel_scr, n_sel):
    j = pl.program_id(1)
    tq = aq_ref.shape[0]
    ik = ik_ref[0:length, :]
    score = jnp.zeros((tq, length), F32)
    for h in range(IDX_HEADS):
        s = lax.dot_general(iq_ref[:, h * HD:(h + 1) * HD], ik, _NT, preferred_element_type=F32)
        score = score + jnp.maximum(s, 0.0) * iw_ref[:, h:h + 1]
    q_pos = j * tq + lax.broadcasted_iota(jnp.int32, (tq, 1), 0)
    col = lax.broadcasted_iota(jnp.int32, (tq, length), 1)
    visible = col <= q_pos
    key_scr[:, 0:length] = _sort_key(jnp.where(visible, score, NEG_BIG))

    rows = tq // SEARCH_GROUPS
    group_count = lambda r: (lambda c: _count(key_scr[r * rows:(r + 1) * rows, 0:length] >= c))
    tau = jnp.concatenate(
        _nth_largest_keys([group_count(r) for r in range(SEARCH_GROUPS)], n_sel, (rows, 1)), axis=0)
    key = key_scr[:, 0:length]
    gt = key > tau
    eq = key == tau
    need = n_sel - _count(gt)
    spare = jnp.max(_count(eq & visible) - need) > 0.0
    bound = lax.cond(
        spare,
        lambda: _tie_bound(lambda c: _count((key_scr[:, 0:length] == tau) & (col < c)), need,
                           max(1, length.bit_length()), (tq, 1)),
        lambda: jnp.full((tq, 1), length, jnp.int32))
    sel_scr[:, 0:length] = ((gt | (eq & (col <= bound))) & visible).astype(F32)

    group = ATT_HEADS // ATT_KV_HEADS
    scale = ATT_HEAD_DIM ** -0.5
    for n in range(ATT_KV_HEADS):
        kn = k_ref[0:length, n * HD:(n + 1) * HD]
        vn = v_ref[0:length, n * HD:(n + 1) * HD]
        for g in range(group):
            sl = slice((n * group + g) * HD, (n * group + g + 1) * HD)
            s = lax.dot_general(aq_ref[:, sl], kn, _NT, preferred_element_type=F32) * scale
            s = jnp.where(sel_scr[:, 0:length] > 0.0, s, NEG_BIG)
            m = jnp.max(s, axis=-1, keepdims=True)
            p = jnp.exp(s - m)
            o = _bdot(p, vn) / jnp.sum(p, axis=-1, keepdims=True)
            o_ref[:, sl] = o.astype(o_ref.dtype)


def _dsa_prompt_kernel(*refs, n_sel, lengths):
    j = pl.program_id(1)
    tq = refs[0].shape[0]
    prev = 0
    for length in lengths:
        @pl.when((j >= prev // tq) & (j < length // tq))
        def _(length=length):
            _dsa_prompt_body(length, *refs, n_sel)
        prev = length


def _dsa_prompt(aq, iq, iw, kb, vb, ikb, b, t, mt):
    tq = min(Q_BLOCK, t)
    nq = t // tq
    n_sel = min(TOPK_MAX, t // 4)
    step = min(t, 512)
    lengths = tuple(range(step, t + 1, step))
    assert t % step == 0 and step % tq == 0 and step >= n_sel
    qrow = lambda w: pl.BlockSpec((tq, w), lambda bi, j: (bi * nq + j, 0))
    krow = lambda w: pl.BlockSpec((t, w), lambda bi, j: (bi, 0))
    return pl.pallas_call(
        functools.partial(_dsa_prompt_kernel, n_sel=n_sel, lengths=lengths),
        grid=(b, nq),
        in_specs=[qrow(aq.shape[1]), qrow(iq.shape[1]), qrow(LANE),
                  krow(kb.shape[1]), krow(vb.shape[1]), krow(HD)],
        out_specs=qrow(aq.shape[1]),
        out_shape=jax.ShapeDtypeStruct((mt, aq.shape[1]), BF16),
        scratch_shapes=[pltpu.VMEM((tq, t), jnp.int32), pltpu.VMEM((tq, t), F32)],
        compiler_params=_params(("parallel", "arbitrary")),
        name="dsa_prompt",
    )(aq, iq, iw, kb, vb, ikb)


def _dec_score_kernel(pt_ref, iq_ref, iw_ref, ikn_ref, *refs):
    pages = refs[:PAGES_PER_STEP]
    sc_ref, new_ref = refs[PAGES_PER_STEP:]
    iq = iq_ref[...].astype(BF16)
    iw = iw_ref[...]
    ik = jnp.concatenate([p[...].astype(BF16) for p in pages], axis=0)
    s = lax.dot_general(iq, ik, _NT, preferred_element_type=F32)
    sc_ref[...] = jnp.sum(jnp.maximum(s, 0.0) * iw, axis=0, keepdims=True)

    @pl.when(pl.program_id(1) == 0)
    def _():
        sn = lax.dot_general(iq, jnp.broadcast_to(ikn_ref[...], (8, HD)).astype(BF16), _NT,
                             preferred_element_type=F32)[:, 0:1]
        new_ref[...] = jnp.broadcast_to(jnp.sum(jnp.maximum(sn, 0.0) * iw, axis=0, keepdims=True), (1, LANE))


def _dec_scores(page_table, iq, iw, ik_new, cache_ik, l):
    nb, n_pages = page_table.shape
    steps = n_pages // PAGES_PER_STEP
    ih = iq.shape[1]
    page = lambda r: pl.BlockSpec((None, None, PAGE_SIZE, HD),
                                  lambda bi, p, pt: (l, pt[bi, p * PAGES_PER_STEP + r], 0, 0))
    grid_spec = pltpu.PrefetchScalarGridSpec(
        num_scalar_prefetch=1,
        grid=(nb, steps),
        in_specs=[pl.BlockSpec((None, ih, HD), lambda bi, p, pt: (bi, 0, 0)),
                  pl.BlockSpec((None, ih, 1), lambda bi, p, pt: (bi, 0, 0)),
                  pl.BlockSpec((None, 1, HD), lambda bi, p, pt: (bi, 0, 0))]
                 + [page(r) for r in range(PAGES_PER_STEP)],
        out_specs=[pl.BlockSpec((None, 1, PAGES_PER_STEP * PAGE_SIZE), lambda bi, p, pt: (bi, 0, p)),
                   pl.BlockSpec((None, 1, LANE), lambda bi, p, pt: (bi, 0, 0))],
    )
    return pl.pallas_call(
        _dec_score_kernel,
        grid_spec=grid_spec,
        out_shape=[jax.ShapeDtypeStruct((nb, 1, n_pages * PAGE_SIZE), F32),
                   jax.ShapeDtypeStruct((nb, 1, LANE), F32)],
        compiler_params=_params(("parallel", "arbitrary")),
        name="dec_scores",
    )(page_table, iq, iw, ik_new, *([cache_ik] * PAGES_PER_STEP))


def _dec_select_kernel(sc_ref, new_ref, tau_ref, bnd_ref, seln_ref, *, n_sel, nbits):
    key = _sort_key(sc_ref[...])
    key_new = _sort_key(new_ref[...])[:, 0:1]
    nb, s_len = key.shape
    idx = lax.broadcasted_iota(jnp.int32, key.shape, 1)
    tau = _nth_largest_key(lambda c: _count(key >= c) + (key_new >= c).astype(F32), n_sel, (nb, 1))
    eq = key == tau
    need = n_sel - _count(key > tau) - (key_new > tau).astype(F32)
    n_eq = _count(eq)
    bound = lax.cond(
        jnp.max(n_eq - need) > 0.0,
        lambda: _tie_bound(lambda c: _count(eq & (idx < c)), need, nbits, (nb, 1)),
        lambda: jnp.full((nb, 1), s_len, jnp.int32))
    sel_new = (key_new > tau) | ((key_new == tau) & (n_eq < need))
    tau_ref[...] = jnp.broadcast_to(tau, tau_ref.shape)
    bnd_ref[...] = jnp.broadcast_to(bound, bnd_ref.shape)
    seln_ref[...] = jnp.broadcast_to(sel_new.astype(F32), seln_ref.shape)


def _dec_select(scores, score_new, n_sel):
    nb, s_len = scores.shape
    out = lambda dt: jax.ShapeDtypeStruct((nb, LANE), dt)
    return pl.pallas_call(
        functools.partial(_dec_select_kernel, n_sel=n_sel, nbits=s_len.bit_length()),
        out_shape=[out(jnp.int32), out(jnp.int32), out(F32)],
        compiler_params=pltpu.CompilerParams(vmem_limit_bytes=VMEM_LIMIT),
        name="dec_select",
    )(scores, score_new)


def _dec_attn_kernel(pt_ref, tau_ref, bnd_ref, seln_ref, sc_ref, q_ref, kn_ref, vn_ref, *refs, steps):
    kp = refs[:PAGES_PER_STEP]
    vp = refs[PAGES_PER_STEP:2 * PAGES_PER_STEP]
    o_ref, m_scr, l_scr, acc_scr = refs[2 * PAGES_PER_STEP:]
    p_id = pl.program_id(1)
    chunk = PAGES_PER_STEP * PAGE_SIZE
    scale = ATT_HEAD_DIM ** -0.5

    @pl.when(p_id == 0)
    def _():
        m_scr[...] = jnp.full_like(m_scr, NEG_BIG)
        l_scr[...] = jnp.zeros_like(l_scr)
        acc_scr[...] = jnp.zeros_like(acc_scr)

    tau = tau_ref[:, 0:1]
    bound = bnd_ref[:, 0:1]
    key_c = _sort_key(sc_ref[...])
    idx_c = p_id * chunk + lax.broadcasted_iota(jnp.int32, key_c.shape, 1)
    sel = (key_c > tau) | ((key_c == tau) & (idx_c <= bound))
    q = q_ref[...].astype(BF16)
    heads = lambda pages, n: jnp.concatenate(
        [r[pl.ds(n, PAGE_SIZE, stride=ATT_KV_HEADS), :].astype(BF16) for r in pages], axis=0)
    s = sum(lax.dot_general(q[:, n * HD:(n + 1) * HD], heads(kp, n), _NT, preferred_element_type=F32)
            for n in range(ATT_KV_HEADS)) * scale
    s = jnp.where(sel, s, NEG_BIG)
    m_old = m_scr[:, 0:1]
    m_new = jnp.maximum(m_old, jnp.max(s, axis=-1, keepdims=True))
    alpha = jnp.exp(m_old - m_new)
    p = jnp.where(sel, jnp.exp(s - m_new), 0.0)
    l_new = alpha * l_scr[:, 0:1] + jnp.sum(p, axis=-1, keepdims=True)
    pb = p.astype(BF16)
    acc_new = alpha * acc_scr[...] + jnp.concatenate(
        [jnp.dot(pb, heads(vp, n), preferred_element_type=F32) for n in range(ATT_KV_HEADS)], axis=1)
    m_scr[...] = jnp.broadcast_to(m_new, m_scr.shape)
    l_scr[...] = jnp.broadcast_to(l_new, l_scr.shape)
    acc_scr[...] = acc_new

    @pl.when(p_id == steps - 1)
    def _():
        sel_new = seln_ref[:, 0:1] > 0.0
        qf = q_ref[...]
        s_new = jnp.sum(qf * kn_ref[...], axis=-1, keepdims=True) * scale
        s_new = jnp.where(sel_new, s_new, NEG_BIG)
        m_fin = jnp.maximum(m_new, s_new)
        a2 = jnp.exp(m_new - m_fin)
        p_new = jnp.where(sel_new, jnp.exp(s_new - m_fin), 0.0)
        l_fin = a2 * l_new + p_new
        acc_fin = (a2 * acc_new + p_new * vn_ref[...]) / l_fin
        group = ATT_HEADS // ATT_KV_HEADS
        for h in range(ATT_HEADS):
            n = h // group
            o_ref[:, h * HD:(h + 1) * HD] = acc_fin[h:h + 1, n * HD:(n + 1) * HD]


def _dec_attn(page_table, scores, score_new, q_bd, k_new, v_new, cache_k, cache_v, l):
    nb, n_pages = page_table.shape
    steps = n_pages // PAGES_PER_STEP
    s_len = n_pages * PAGE_SIZE
    chunk = PAGES_PER_STEP * PAGE_SIZE
    n_sel = min(TOPK_MAX, (s_len + 1) // 4)
    tau, bound, sel_new = _dec_select(scores.reshape(nb, s_len), score_new.reshape(nb, LANE), n_sel)
    row = lambda a: a.reshape(nb, 1, LANE)
    row_spec = pl.BlockSpec((None, 1, LANE), lambda bi, p, pt: (bi, 0, 0))
    kvw = ATT_KV_HEADS * HD
    hp = q_bd.shape[1]
    page = lambda r: pl.BlockSpec((None, None, PAGE_SIZE * ATT_KV_HEADS, HD),
                                  lambda bi, p, pt: (l, pt[bi, p * PAGES_PER_STEP + r], 0, 0))
    grid_spec = pltpu.PrefetchScalarGridSpec(
        num_scalar_prefetch=1,
        grid=(nb, steps),
        in_specs=[row_spec, row_spec, row_spec,
                  pl.BlockSpec((None, 1, chunk), lambda bi, p, pt: (bi, 0, p)),
                  pl.BlockSpec((None, hp, kvw), lambda bi, p, pt: (bi, 0, 0)),
                  pl.BlockSpec((None, 1, kvw), lambda bi, p, pt: (bi, 0, 0)),
                  pl.BlockSpec((None, 1, kvw), lambda bi, p, pt: (bi, 0, 0))]
                 + [page(r) for r in range(PAGES_PER_STEP)] * 2,
        out_specs=pl.BlockSpec((None, 1, ATT_HEADS * HD), lambda bi, p, pt: (bi, 0, 0)),
        scratch_shapes=[pltpu.VMEM((hp, LANE), F32), pltpu.VMEM((hp, LANE), F32),
                        pltpu.VMEM((hp, kvw), F32)],
    )
    return pl.pallas_call(
        functools.partial(_dec_attn_kernel, steps=steps),
        grid_spec=grid_spec,
        out_shape=jax.ShapeDtypeStruct((nb, 1, ATT_HEADS * HD), F32),
        compiler_params=_params(("parallel", "arbitrary")),
        name="dec_attn",
    )(page_table, row(tau), row(bound), row(sel_new), scores, q_bd, k_new, v_new,
      *([cache_k] * PAGES_PER_STEP), *([cache_v] * PAGES_PER_STEP)).reshape(nb, ATT_HEADS * HD)


def _ffn(x, xb, ssq, w1, w3, w2, l, tm, next_gain):
    tm_up = 2 * tm if (x.shape[0] // tm) % 2 == 0 else tm
    g, w2b = _matmul(xb, [w1, w3], l, tm=tm_up, epilogue='swiglu', out_dtype=BF16, ssq=ssq, cast_src=(w2, l))
    return _matmul(g, [w2b[None]], 0, tm=tm, epilogue='resid', res=x, scale=0.5, next_gain=next_gain)


def _token_tiles(n_tokens):
    n_tiles = max(1, n_tokens // 1024)
    tm = -(-n_tokens // (16 * n_tiles)) * 16
    return tm, n_tiles


def kernel(x_prompt, x_sample, state_ret, state_hgrn, cache_k, cache_v, cache_idx_k, page_table, ffn1_norm, ffn1_w1, ffn1_w3, ffn1_w2, mix_norm, w_in, ret_norm, q_norm, k_norm, idx_k_g, idx_k_b, hg_lb_raw, hg_norm, w_up_ret, w_up_att, w_up_hg, w_out, ffn2_norm, ffn2_w1, ffn2_w3, ffn2_w2):
    b, t, d = x_prompt.shape
    nb = x_sample.shape[0]
    depth = w_in.shape[0]
    kvw = ATT_KV_HEADS * ATT_HEAD_DIM
    group = ATT_HEADS // ATT_KV_HEADS

    lb_soft = jax.nn.softmax(hg_lb_raw.astype(F32), axis=0)
    lb_all = jnp.cumsum(lb_soft, axis=0) - lb_soft[0]
    cast = lambda w: w.astype(BF16)
    w_up_ret, w_up_att, w_up_hg, w_out = cast(w_up_ret), cast(w_up_att), cast(w_up_hg), cast(w_out)

    pos_p = jnp.arange(t, dtype=jnp.int32)
    pos_s = jnp.full((nb,), PAST_LEN, jnp.int32)
    ret_f = 1.0 / (ROPE_THETA ** jnp.linspace(0.0, 1.0, RET_DK // 2, dtype=F32))
    att_f = ROPE_THETA ** (-jnp.arange(0, ATT_HEAD_DIM, 2, dtype=F32) / ATT_HEAD_DIM)
    idx_f = ROPE_THETA ** (-jnp.arange(0, IDX_ROPE_DIM, 2, dtype=F32) / IDX_ROPE_DIM)
    tabs_p = _rope_tables(pos_p, att_f, ATT_HEAD_DIM) + _rope_tables(pos_p, idx_f, IDX_ROPE_DIM)
    tabs_s = _rope_tables(pos_s, att_f, ATT_HEAD_DIM) + _rope_tables(pos_s, idx_f, IDX_ROPE_DIM)
    ret_tab_p = _rope_tables(pos_p, ret_f, RET_DK)
    ret_tab_s = _rope_tables(pos_s[:1], ret_f, RET_DK)

    ck = cache_k.reshape(cache_k.shape[:2] + (PAGE_SIZE * ATT_KV_HEADS, HD))
    cv = cache_v.reshape(cache_v.shape[:2] + (PAGE_SIZE * ATT_KV_HEADS, HD))

    mp = b * t
    tm, n_tiles = _token_tiles(mp + nb)
    mt = tm * n_tiles
    x = jnp.concatenate([x_prompt.reshape(mp, d), x_sample.reshape(nb, d), jnp.zeros((mt - mp - nb, d), F32)])

    def with_tail(a, rows):
        tail = jnp.concatenate([rows.astype(a.dtype), jnp.zeros((mt - mp - nb, a.shape[1]), a.dtype)])
        return lax.dynamic_update_slice(a, tail, (mp, 0))

    outs = {n: [] for n in ('rp', 'rs', 'hp', 'hs', 'kp', 'vp', 'ip', 'ks', 'vs', 'is')}
    xb, ssq = _row_prep(x, ffn1_norm, 0)
    for l in range(depth):
        x, xb, ssq = _ffn(x, xb, ssq, ffn1_w1, ffn1_w3, ffn1_w2, l, tm, (mix_norm, l))
        z = _in_proj(xb, ssq, w_in, l, 2 * tm if n_tiles % 2 == 0 else tm)

        aq, ak, av, akb, avb, iq, ik, ikb, iw = _prelude(z, mp, l, tabs_p, q_norm, k_norm, idx_k_g, idx_k_b, t)
        a_ret, r_p = _retention(z, l, ret_tab_p[0], ret_tab_p[1], ret_norm, b, t)
        a_hg, h_p = _gla(z, l, lb_all, hg_norm, b, t)
        a_att = _dsa_prompt(aq, iq, iw, akb, avb, ikb, b, t, mt)
        outs['rp'].append(r_p); outs['hp'].append(h_p)
        outs['kp'].append(ak.reshape(b, t, ATT_KV_HEADS, ATT_HEAD_DIM))
        outs['vp'].append(av.reshape(b, t, ATT_KV_HEADS, ATT_HEAD_DIM))
        outs['ip'].append(ik.reshape(b, t, IDX_DIM))

        zs = z[mp:mp + nb]
        aq, ak, av, _, _, iq, ik, _, iw = _prelude(zs, nb, l, tabs_s, q_norm, k_norm, idx_k_g, idx_k_b, 1)
        s_ret, s_hg, r_s, h_s = _decode_rec(zs, l, ret_tab_s[0], ret_tab_s[1], ret_norm, hg_norm, lb_all,
                                            state_ret, state_hgrn)
        scores, score_new = _dec_scores(page_table, iq.reshape(nb, IDX_HEADS, IDX_DIM),
                                        iw[:, :IDX_HEADS].reshape(nb, IDX_HEADS, 1),
                                        ik.reshape(nb, 1, IDX_DIM), cache_idx_k, l)
        qh = aq.reshape(nb, ATT_KV_HEADS, group, ATT_HEAD_DIM)
        q_bd = jnp.concatenate(
            [jnp.concatenate([qh[:, n] if m == n else jnp.zeros_like(qh[:, n]) for m in range(ATT_KV_HEADS)], axis=-1)
             for n in range(ATT_KV_HEADS)]
            + [jnp.zeros((nb, max(0, 16 - ATT_HEADS), kvw), aq.dtype)], axis=1)
        s_att = _dec_attn(page_table, scores, score_new, q_bd, ak.reshape(nb, 1, kvw), av.reshape(nb, 1, kvw),
                          ck, cv, l)
        outs['rs'].append(r_s); outs['hs'].append(h_s)
        outs['ks'].append(ak.reshape(nb, 1, ATT_KV_HEADS, ATT_HEAD_DIM))
        outs['vs'].append(av.reshape(nb, 1, ATT_KV_HEADS, ATT_HEAD_DIM))
        outs['is'].append(ik.reshape(nb, 1, IDX_DIM))

        merged = _merge(with_tail(a_ret, s_ret.reshape(nb, -1)), with_tail(a_att, s_att),
                        with_tail(a_hg, s_hg.reshape(nb, -1)), w_up_ret, w_up_att, w_up_hg, z, l, BF16, tm)
        x, xb, ssq = _matmul(merged, [w_out], l, tm=tm, epilogue='resid', res=x, scale=1.0,
                             next_gain=(ffn2_norm, l))
        if l + 1 < depth:
            x, xb, ssq = _ffn(x, xb, ssq, ffn2_w1, ffn2_w3, ffn2_w2, l, tm, (ffn1_norm, l + 1))
        else:
            x = _ffn(x, xb, ssq, ffn2_w1, ffn2_w3, ffn2_w2, l, tm, None)

    st = lambda n: jnp.stack(outs[n])
    return (x[:mp].reshape(b, t, d), x[mp:mp + nb].reshape(nb, 1, d),
            st('rp').astype(state_ret.dtype), st('rs').astype(state_ret.dtype),
            st('hp').astype(state_hgrn.dtype), st('hs').astype(state_hgrn.dtype),
            st('kp'), st('vp'), st('ip'), st('ks'), st('vs'), st('is'))
```

```python
import functools
import math

import numpy as np
import jax
import jax.numpy as jnp
from jax import lax
from jax.experimental import pallas as pl
from jax.experimental.pallas import tpu as pltpu

D_MODEL = 4096
BATCH = 4
SEQ = 2048
DEPTH = 2
DEC_BATCH = 8
DEC_SEQ = 1
PAST_LEN = 16384
PAGE_SIZE = 128

RET_HEADS = 8
RET_DK = 128
RET_DV = 128
ATT_HEADS = 8
ATT_KV_HEADS = 2
ATT_HEAD_DIM = 128
IDX_HEADS = 16
IDX_DIM = 128
IDX_ROPE_DIM = 64
TOPK_MAX = 256
Q_BLOCK = 128
HG_HEADS = 8
HG_DK = 128
HG_DV = 128
D_FF = 11008
ROPE_THETA = 10000.0
EPS = 1e-6
NEG_BIG = -1e30

F32 = jnp.float32
BF16 = jnp.bfloat16
LANE = 128
HD = 128
INT_MIN = -(2 ** 31)
VMEM_LIMIT = 56 * 1024 * 1024
GLA_CHUNK = 64
GLA_SUB = 16
SEARCH_GROUPS = 4
SEARCH_UNROLL = 6
GLA_SAFE_SPAN = 60.0
PAGES_PER_STEP = 16

_NT = (((1,), (1,)), ((), ()))
_TN = (((0,), (0,)), ((), ()))


def _params(sem):
    return pltpu.CompilerParams(dimension_semantics=sem, vmem_limit_bytes=VMEM_LIMIT)


def _bdot(a, b, dims=None):
    a = a.astype(BF16)
    b = b.astype(BF16)
    if dims is None:
        return jnp.dot(a, b, preferred_element_type=F32)
    return lax.dot_general(a, b, dims, preferred_element_type=F32)


def _silu(x):
    return x * jax.nn.sigmoid(x)


def _segments():
    ret_qk = RET_HEADS * RET_DK
    ret_w = RET_HEADS * RET_DV
    att_w = ATT_HEADS * ATT_HEAD_DIM
    kv_w = ATT_KV_HEADS * ATT_HEAD_DIM
    hg_k = HG_HEADS * HG_DK
    hg_w = HG_HEADS * HG_DV
    names = ['r_q', 'r_k', 'r_v', 'r_g', 'a_q', 'a_k', 'a_v', 'i_q', 'i_k', 'i_w',
             'h_f', 'h_q', 'h_i', 'h_g', 'g_ret', 'g_att', 'g_hg']
    widths = [ret_qk, ret_qk, ret_w, ret_w, att_w, kv_w, kv_w, IDX_HEADS * IDX_DIM, IDX_DIM, IDX_HEADS,
              hg_k, hg_k, hg_w, hg_w, D_MODEL, D_MODEL, D_MODEL]
    src = {}
    off = 0
    for n, w in zip(names, widths):
        src[n] = (off, w)
        off += w
    padded = {n: -(-w // LANE) * LANE for n, w in zip(names, widths)}
    order = ['g_ret', 'g_att', 'g_hg'] + sorted(
        [n for n in names if not n.startswith('g_')], key=lambda n: -padded[n])
    dst = {}
    off = 0
    for n in order:
        dst[n] = off
        off += padded[n]
    total = -(-off // 256) * 256
    for n in order:
        assert dst[n] % padded[n] == 0 or n.startswith('g_'), (n, dst[n], padded[n])
    return src, dst, padded, order, total


def _row_prep_kernel(x_ref, g_ref, xb_ref, ssq_ref):
    x = x_ref[...]
    xb_ref[...] = (x * g_ref[...]).astype(xb_ref.dtype)
    ssq_ref[...] = jnp.broadcast_to(jnp.sum(x * x, axis=-1, keepdims=True), ssq_ref.shape)


def _row_tile(m, cap):
    best = None
    for t in range(16, min(m, cap) + 1, 16):
        if m % t == 0:
            best = t
    return best if best is not None else m


def _row_prep(x, g_all, l):
    m, d = x.shape
    tr = _row_tile(m, 512)
    return pl.pallas_call(
        _row_prep_kernel,
        grid=(m // tr,),
        in_specs=[pl.BlockSpec((tr, d), lambda i: (i, 0)),
                  pl.BlockSpec((None, 1, d), lambda i: (l, 0, 0))],
        out_specs=[pl.BlockSpec((tr, d), lambda i: (i, 0)), pl.BlockSpec((tr, LANE), lambda i: (i, 0))],
        out_shape=[jax.ShapeDtypeStruct((m, d), BF16), jax.ShapeDtypeStruct((m, LANE), F32)],
        compiler_params=_params(("parallel",)),
        name="row_prep",
    )(x, g_all.reshape(g_all.shape[0], 1, d))


def _mm_kernel(*refs, n_w, epilogue, scale, row_norm, feed_norm, cast_rows):
    it = iter(refs)
    lhs_ref = next(it)
    w_refs = [next(it) for _ in range(n_w)]
    res_ref = next(it) if epilogue == 'resid' else None
    ssq_ref = next(it) if row_norm else None
    gain_ref = next(it) if feed_norm else None
    cast_in_ref = next(it) if cast_rows else None
    out_ref = next(it)
    xb_ref, ssq_out_ref = (next(it), next(it)) if feed_norm else (None, None)
    if cast_rows:
        next(it)[...] = cast_in_ref[...].astype(BF16)

    lhs = lhs_ref[...].astype(BF16)
    vals = [jnp.dot(lhs, w[...].astype(BF16), preferred_element_type=F32) for w in w_refs]
    if row_norm:
        r = lax.rsqrt(ssq_ref[:, 0:1] * (1.0 / lhs_ref.shape[1]) + EPS)
        vals = [v * r for v in vals]
    if epilogue == 'swiglu':
        out = _silu(vals[0]) * vals[1]
    elif epilogue == 'resid':
        out = res_ref[...] + scale * vals[0]
    else:
        out = vals[0]
    out_ref[...] = out.astype(out_ref.dtype)
    if feed_norm:
        xb_ref[...] = (out * gain_ref[...]).astype(xb_ref.dtype)
        part = jnp.broadcast_to(jnp.sum(out * out, axis=-1, keepdims=True), ssq_out_ref.shape)
        j = pl.program_id(1)

        @pl.when(j == 0)
        def _():
            ssq_out_ref[...] = part

        @pl.when(j > 0)
        def _():
            ssq_out_ref[...] += part


def _pick(n, cands):
    for c in cands:
        if n % c == 0:
            return c
    return n


def _matmul(lhs, ws, l, *, tm, epilogue='plain', res=None, scale=1.0, out_dtype=F32, tn=256, single_lhs=None,
            ssq=None, next_gain=None, cast_src=None):
    assert next_gain is None or cast_src is None
    m, kdim = lhs.shape
    n = ws[0].shape[-1]
    tn = _pick(n, (tn, 256, 128))
    assert m % tm == 0 and n % tn == 0
    if single_lhs is None:
        single_lhs = tm * kdim * lhs.dtype.itemsize > (12 << 20)
    lhs_mode = dict(pipeline_mode=pl.Buffered(1)) if single_lhs else {}
    in_specs = [pl.BlockSpec((tm, kdim), lambda i, j: (i, 0), **lhs_mode)]
    in_specs += [pl.BlockSpec((None, kdim, tn), lambda i, j: (l, 0, j)) for _ in ws]
    args = [lhs] + list(ws)
    tile = pl.BlockSpec((tm, tn), lambda i, j: (i, j))
    rows = pl.BlockSpec((tm, LANE), lambda i, j: (i, 0))
    if epilogue == 'resid':
        in_specs.append(tile)
        args.append(res)
    if ssq is not None:
        in_specs.append(rows)
        args.append(ssq)
    out_specs, out_shape = tile, jax.ShapeDtypeStruct((m, n), out_dtype)
    if next_gain is not None:
        gains, gl = next_gain
        in_specs.append(pl.BlockSpec((None, 1, tn), lambda i, j: (gl, 0, j)))
        args.append(gains.reshape(gains.shape[0], 1, n))
        out_specs = [tile, tile, rows]
        out_shape = [out_shape, jax.ShapeDtypeStruct((m, n), BF16), jax.ShapeDtypeStruct((m, LANE), F32)]
    cast_rows = 0
    if cast_src is not None:
        src, cl = cast_src
        steps, nj = (m // tm) * (n // tn), n // tn
        cast_rows = src.shape[1] // steps
        assert cast_rows * steps == src.shape[1] and cast_rows % 16 == 0
        in_specs.append(pl.BlockSpec((None, cast_rows, src.shape[2]), lambda i, j: (cl, i * nj + j, 0)))
        args.append(src)
        out_specs = [out_specs, pl.BlockSpec((cast_rows, src.shape[2]), lambda i, j: (i * nj + j, 0))]
        out_shape = [out_shape, jax.ShapeDtypeStruct(src.shape[1:], BF16)]
    return pl.pallas_call(
        functools.partial(_mm_kernel, n_w=len(ws), epilogue=epilogue, scale=scale,
                          row_norm=ssq is not None, feed_norm=next_gain is not None, cast_rows=cast_rows),
        grid=(m // tm, n // tn),
        in_specs=in_specs,
        out_specs=out_specs,
        out_shape=out_shape,
        compiler_params=_params(("parallel", "arbitrary")),
        name="mm_" + epilogue,
    )(*args)


def _in_proj_kernel(start_ref, lhs_ref, w_ref, ssq_ref, out_ref):
    w = w_ref[0].astype(BF16)
    acc = lax.dot_general(lhs_ref[...], w, _NT, preferred_element_type=F32)
    out_ref[...] = acc * lax.rsqrt(ssq_ref[:, 0:1] * (1.0 / lhs_ref.shape[1]) + EPS)


def _in_proj(xb, ssq, w_in, l, tm):
    src, dst, padded, order, total = _segments()
    depth, kdim, n_in = w_in.shape
    m = xb.shape[0]
    tn = 256
    col_src = np.zeros((total,), np.int64)
    for n in order:
        col_src[dst[n]:dst[n] + padded[n]] = src[n][0] + np.arange(padded[n])
    start = col_src[::tn].copy()
    assert (col_src.reshape(-1, tn) == start[:, None] + np.arange(tn)).all(), "tile is not one source range"
    assert (start % 8 == 0).all() and (start + tn <= n_in).all()
    grid_spec = pltpu.PrefetchScalarGridSpec(
        num_scalar_prefetch=1,
        grid=(m // tm, total // tn),
        in_specs=[pl.BlockSpec((tm, kdim), lambda i, j, start: (i, 0), pipeline_mode=pl.Buffered(1)),
                  pl.BlockSpec((pl.Element(1), pl.Element(tn), pl.Element(kdim)),
                               lambda i, j, start: (l, start[j] * 8, 0)),
                  pl.BlockSpec((tm, LANE), lambda i, j, start: (i, 0))],
        out_specs=pl.BlockSpec((tm, tn), lambda i, j, start: (i, j)),
    )
    return pl.pallas_call(
        _in_proj_kernel,
        grid_spec=grid_spec,
        out_shape=jax.ShapeDtypeStruct((m, total), F32),
        compiler_params=_params(("parallel", "arbitrary")),
        name="in_proj",
    )(jnp.asarray(start // 8, jnp.int32), xb, jnp.swapaxes(w_in, 1, 2), ssq)


def _merge_kernel(ar_ref, aa_ref, ah_ref, wr_ref, wa_ref, wh_ref, gr_ref, ga_ref, gh_ref, o_ref):
    u_r = _bdot(ar_ref[...], wr_ref[...])
    u_a = _bdot(aa_ref[...], wa_ref[...])
    u_h = _bdot(ah_ref[...], wh_ref[...])
    out = (jax.nn.sigmoid(gr_ref[...]) * u_r + jax.nn.sigmoid(ga_ref[...]) * u_a
           + jax.nn.sigmoid(gh_ref[...]) * u_h)
    o_ref[...] = out.astype(o_ref.dtype)


def _merge(a_ret, a_att, a_hg, w_r, w_a, w_h, z, l, out_dtype, tm):
    m = a_ret.shape[0]
    d = w_r.shape[-1]
    tn = _pick(d, (256, 128))
    nb = d // tn
    lhs_spec = lambda a: pl.BlockSpec((tm, a.shape[1]), lambda i, j: (i, 0))
    w_spec = lambda w: pl.BlockSpec((None, w.shape[1], tn), lambda i, j: (l, 0, j))
    gate_spec = lambda g: pl.BlockSpec((tm, tn), lambda i, j: (i, g * nb + j))
    return pl.pallas_call(
        _merge_kernel,
        grid=(m // tm, nb),
        in_specs=[lhs_spec(a_ret), lhs_spec(a_att), lhs_spec(a_hg), w_spec(w_r), w_spec(w_a), w_spec(w_h),
                  gate_spec(0), gate_spec(1), gate_spec(2)],
        out_specs=pl.BlockSpec((tm, tn), lambda i, j: (i, j)),
        out_shape=jax.ShapeDtypeStruct((m, d), out_dtype),
        compiler_params=_params(("parallel", "parallel")),
        name="merge",
    )(a_ret, a_att, a_hg, w_r, w_a, w_h, z, z, z)


def _rot(x, cos, sin, half):
    up = pltpu.roll(x, LANE - half, 1)
    dn = pltpu.roll(x, half, 1)
    lane = lax.broadcasted_iota(jnp.int32, x.shape, 1)
    return x * cos + jnp.where(lane < half, up, dn) * sin


def _rope_tables(pos, freqs, width):
    ang = pos.astype(F32)[:, None] * freqs[None, :]
    cos, sin = jnp.cos(ang), jnp.sin(ang)
    t = pos.shape[0]
    pad_c = jnp.ones((t, LANE - width), F32)
    pad_s = jnp.zeros((t, LANE - width), F32)
    return (jnp.concatenate([cos, cos, pad_c], axis=1), jnp.concatenate([-sin, sin, pad_s], axis=1))


def _prelude_kernel(zq_ref, zk_ref, zv_ref, ziq_ref, zik_ref, ziw_ref, ca_ref, sa_ref, ci_ref, si_ref,
                    qn_ref, kn_ref, ig_ref, ib_ref,
                    aq_ref, ak_ref, av_ref, akb_ref, avb_ref, iq_ref, ik_ref, ikb_ref, iw_ref):
    ca, sa, ci, si = ca_ref[...], sa_ref[...], ci_ref[...], si_ref[...]

    def head_norm(x, g):
        return x * lax.rsqrt(jnp.mean(x * x, axis=-1, keepdims=True) + EPS) * g

    for h in range(ATT_HEADS):
        sl = slice(h * HD, (h + 1) * HD)
        aq_ref[:, sl] = _rot(head_norm(zq_ref[:, sl], qn_ref[...]), ca, sa, HD // 2).astype(aq_ref.dtype)
    for h in range(ATT_KV_HEADS):
        sl = slice(h * HD, (h + 1) * HD)
        k = _rot(head_norm(zk_ref[:, sl], kn_ref[...]), ca, sa, HD // 2)
        ak_ref[:, sl] = k
        akb_ref[:, sl] = k.astype(akb_ref.dtype)
    v = zv_ref[...]
    av_ref[...] = v
    avb_ref[...] = v.astype(avb_ref.dtype)
    for h in range(IDX_HEADS):
        sl = slice(h * HD, (h + 1) * HD)
        iq_ref[:, sl] = (_rot(ziq_ref[:, sl], ci, si, IDX_ROPE_DIM // 2) * (IDX_DIM ** -0.5)).astype(iq_ref.dtype)
    x = zik_ref[...]
    mu = jnp.mean(x, axis=-1, keepdims=True)
    var = jnp.mean(jnp.square(x - mu), axis=-1, keepdims=True)
    ik = _rot((x - mu) * lax.rsqrt(var + EPS) * ig_ref[...] + ib_ref[...], ci, si, IDX_ROPE_DIM // 2)
    ik_ref[...] = ik
    ikb_ref[...] = ik.astype(ikb_ref.dtype)
    iw_ref[...] = ziw_ref[...] * (IDX_HEADS ** -0.5)


def _prelude(z, m, l, tabs, q_norm, k_norm, idx_g, idx_b, t_len):
    _, dst, padded, _, _ = _segments()
    tr = min(m, 256)
    nt = t_len // tr if t_len >= tr else 1
    lowp = BF16 if tr >= 16 else F32
    aw, kvw, iw = padded['a_q'], padded['a_k'], padded['i_q']

    def zspec(name):
        w = padded[name]
        return pl.BlockSpec((tr, w), lambda i: (i, dst[name] // w))

    tab_spec = pl.BlockSpec((tr, LANE), lambda i: (i % nt, 0))
    vec_spec = pl.BlockSpec((None, 1, HD), lambda i: (l, 0, 0))
    row = lambda w: pl.BlockSpec((tr, w), lambda i: (i, 0))
    shp = lambda w, dt: jax.ShapeDtypeStruct((m, w), dt)
    vec = lambda a: a.reshape(a.shape[0], 1, HD)
    return pl.pallas_call(
        _prelude_kernel,
        grid=(m // tr,),
        in_specs=[zspec('a_q'), zspec('a_k'), zspec('a_v'), zspec('i_q'), zspec('i_k'), zspec('i_w'),
                  tab_spec, tab_spec, tab_spec, tab_spec, vec_spec, vec_spec, vec_spec, vec_spec],
        out_specs=[row(aw), row(kvw), row(kvw), row(kvw), row(kvw), row(iw), row(HD), row(HD), row(LANE)],
        out_shape=[shp(aw, lowp), shp(kvw, F32), shp(kvw, F32), shp(kvw, lowp), shp(kvw, lowp),
                   shp(iw, lowp), shp(HD, F32), shp(HD, lowp), shp(LANE, F32)],
        compiler_params=_params(("parallel",)),
        name="attn_prelude",
    )(z, z, z, z, z, z, *tabs, vec(q_norm), vec(k_norm), vec(idx_g), vec(idx_b))


def _ret_gamma_log(h):
    return math.log(1.0 - 2.0 ** (-5.0 - h))


def _retention_kernel(lg_ref, q_ref, k_ref, v_ref, g_ref, cos_ref, sin_ref, nrm_ref, o_ref, st_ref,
                      s_scr, intra_scr, qdec_scr, kdec_scr, *, nc):
    c = pl.program_id(2)
    cr = q_ref.shape[0]
    n_heads = q_ref.shape[1] // HD

    @pl.when(c == 0)
    def _():
        s_scr[...] = jnp.zeros_like(s_scr)
        ti = lax.broadcasted_iota(jnp.int32, (cr, cr), 0)
        si = lax.broadcasted_iota(jnp.int32, (cr, cr), 1)
        diff = (ti - si).astype(F32)
        tcol = lax.broadcasted_iota(jnp.int32, (cr, LANE), 0).astype(F32)
        for hh in range(n_heads):
            lg = lg_ref[hh][:, 0:1]
            intra_scr[hh] = jnp.where(diff >= 0, jnp.exp(lg * jnp.maximum(diff, 0.0)), 0.0)
            qdec_scr[hh] = jnp.exp(lg * (tcol + 1.0))
            kdec_scr[hh] = jnp.exp(lg * (cr - 1.0 - tcol))

    cos, sin = cos_ref[...], sin_ref[...]
    for hh in range(n_heads):
        hs = slice(hh * HD, (hh + 1) * HD)
        q = _rot(q_ref[:, hs], cos, sin, RET_DK // 2)
        k = _rot(k_ref[:, hs], cos, sin, RET_DK // 2) * (RET_DK ** -0.5)
        v = v_ref[:, hs]
        s_dec = jnp.exp(lg_ref[hh][:, 0:1] * cr)
        s = s_scr[hh]
        a = _bdot(q, k, _NT) * intra_scr[hh]
        o = _bdot(a, v) + _bdot(q * qdec_scr[hh], s)
        s_scr[hh] = s * s_dec + _bdot(k * kdec_scr[hh], v, _TN)
        y = o * lax.rsqrt(jnp.mean(o * o, axis=-1, keepdims=True) + EPS) * nrm_ref[:, hs]
        o_ref[:, hs] = (y * _silu(g_ref[:, hs])).astype(o_ref.dtype)

    @pl.when(c == nc - 1)
    def _():
        st_ref[...] = s_scr[...]


def _retention(z, l, cos, sin, ret_norm, b, t):
    _, dst, _, _, _ = _segments()
    cr = min(t, 256)
    nc = t // cr
    lg = jnp.broadcast_to(
        jnp.asarray([_ret_gamma_log(h) for h in range(RET_HEADS)], F32)[:, None, None], (RET_HEADS, 1, LANE))

    hps = 2 if RET_HEADS % 2 == 0 else 1
    w = hps * HD

    def zspec(name):
        assert dst[name] % w == 0
        return pl.BlockSpec((cr, w), lambda bi, h, c: (bi * nc + c, dst[name] // w + h))

    tab = pl.BlockSpec((cr, LANE), lambda bi, h, c: (c, 0))
    return pl.pallas_call(
        functools.partial(_retention_kernel, nc=nc),
        grid=(b, RET_HEADS // hps, nc),
        in_specs=[pl.BlockSpec((hps, 1, LANE), lambda bi, h, c: (h, 0, 0)),
                  zspec('r_q'), zspec('r_k'), zspec('r_v'), zspec('r_g'), tab, tab,
                  pl.BlockSpec((None, 1, w), lambda bi, h, c: (l, 0, h))],
        out_specs=[pl.BlockSpec((cr, w), lambda bi, h, c: (bi * nc + c, h)),
                   pl.BlockSpec((None, hps, RET_DK, RET_DV), lambda bi, h, c: (bi, h, 0, 0))],
        out_shape=[jax.ShapeDtypeStruct((z.shape[0], RET_HEADS * RET_DV), BF16),
                   jax.ShapeDtypeStruct((b, RET_HEADS, RET_DK, RET_DV), F32)],
        scratch_shapes=[pltpu.VMEM((hps, RET_DK, RET_DV), F32), pltpu.VMEM((hps, cr, cr), F32),
                        pltpu.VMEM((hps, cr, LANE), F32), pltpu.VMEM((hps, cr, LANE), F32)],
        compiler_params=_params(("parallel", "parallel", "arbitrary")),
        name="retention",
    )(lg, z, z, z, z, cos, sin, ret_norm.reshape(ret_norm.shape[0], 1, -1))


def _hgrn_gates(fa, lb):
    log_f = jnp.minimum(fa, 0.0) - jnp.log1p(jnp.exp(-jnp.abs(fa))) + jnp.log1p(lb * jnp.exp(-fa))
    hk = (1.0 - lb) * jax.nn.sigmoid(-fa)
    return log_f, hk


def _split3(x):
    hi = x.astype(BF16)
    r1 = x - hi.astype(F32)
    mid = r1.astype(BF16)
    lo = (r1 - mid.astype(F32)).astype(BF16)
    return hi, mid, lo


def _gla_state_step(q, k, v, b, st):
    b_last = b[GLA_CHUNK - 1:GLA_CHUNK, :]
    o = _bdot(q * jnp.exp(b), st, _NT)
    kd = k * jnp.exp(b_last - b)
    return o, st * jnp.exp(b_last) + _bdot(v, kd, _TN)


def _gla_intra_anchored(q, k, v, b):
    c = GLA_CHUNK
    nsub = c // GLA_SUB
    ti = lax.broadcasted_iota(jnp.int32, (c, c), 0)
    si = lax.broadcasted_iota(jnp.int32, (c, c), 1)
    anchors = [jnp.zeros((1, HD), F32)] + [b[i * GLA_SUB - 1:i * GLA_SUB, :] for i in range(1, nsub)]
    m_rows = jnp.concatenate([jnp.broadcast_to(m, (GLA_SUB, HD)) for m in anchors], axis=0)
    qs = q * jnp.exp(b - m_rows)
    a = jnp.zeros((c, c), F32)
    for i in range(nsub):
        ks = k * jnp.exp(jnp.minimum(anchors[i] - b, GLA_SAFE_SPAN))
        blk = (ti // GLA_SUB == i) & (si <= ti)
        a = a + jnp.where(blk, _bdot(qs, ks, _NT), 0.0)
    return _bdot(a, v)


def _gla_intra_pairwise(q, k, v, b):
    c = GLA_CHUNK
    nsub = c // GLA_SUB
    ti = lax.broadcasted_iota(jnp.int32, (c, c), 0)
    si = lax.broadcasted_iota(jnp.int32, (c, c), 1)
    a_off = jnp.zeros((c, c), F32)
    for i in range(1, nsub):
        m = b[i * GLA_SUB - 1:i * GLA_SUB, :]
        qs = q * jnp.exp(jnp.minimum(b - m, 0.0))
        ks = k * jnp.exp(jnp.minimum(m - b, 0.0))
        blk = (ti // GLA_SUB == i) & (si < i * GLA_SUB)
        a_off = a_off + jnp.where(blk, _bdot(qs, ks, _NT), 0.0)
    o = _bdot(a_off, v)
    rows = lax.broadcasted_iota(jnp.int32, (GLA_SUB, 1), 0)
    diag = []
    for i in range(nsub):
        sl = slice(i * GLA_SUB, (i + 1) * GLA_SUB)
        qi, ki, vi, bi = q[sl], k[sl], v[sl], b[sl]
        oi = jnp.zeros((GLA_SUB, HD), F32)
        for s in range(GLA_SUB):
            d = jnp.exp(jnp.minimum(bi - bi[s:s + 1], 0.0)) * qi * ki[s:s + 1]
            w = jnp.where(rows >= s, jnp.sum(d, axis=-1, keepdims=True), 0.0)
            oi = oi + w * vi[s:s + 1]
        diag.append(oi)
    return o + jnp.concatenate(diag, axis=0)


def _gla_kernel(f_ref, q_ref, i_ref, g_ref, lb_ref, nrm_ref, o_ref, st_ref, s_scr, *, nc, n_inner):
    c = pl.program_id(2)
    blk = f_ref.shape[0]

    @pl.when(c == 0)
    def _():
        s_scr[...] = jnp.zeros_like(s_scr)

    log_f, hk = _hgrn_gates(f_ref[...], lb_ref[...])
    hq = _silu(q_ref[...])
    v = i_ref[...]
    ti = lax.broadcasted_iota(jnp.int32, (blk, blk), 0)
    si = lax.broadcasted_iota(jnp.int32, (blk, blk), 1)
    tri = ((ti >= si) & (ti // GLA_CHUNK == si // GLA_CHUNK)).astype(BF16)
    hi, mid, lo = _split3(log_f)
    b = (jnp.dot(tri, hi, preferred_element_type=F32) + jnp.dot(tri, mid, preferred_element_type=F32)
         + jnp.dot(tri, lo, preferred_element_type=F32))
    sub_sum = jnp.sum(log_f.reshape(blk // GLA_SUB, GLA_SUB, log_f.shape[1]), axis=1)
    anchored_ok = jnp.min(sub_sum) >= -GLA_SAFE_SPAN
    n_heads = log_f.shape[1] // HD

    def run(intra):
        for hh in range(n_heads):
            hs = slice(hh * HD, (hh + 1) * HD)
            st = s_scr[hh]
            outs = []
            for ci in range(n_inner):
                sl = slice(ci * GLA_CHUNK, (ci + 1) * GLA_CHUNK)
                qc, kc, vc, bc = hq[sl, hs], hk[sl, hs], v[sl, hs], b[sl, hs]
                o_inter, st = _gla_state_step(qc, kc, vc, bc, st)
                outs.append(o_inter + intra(qc, kc, vc, bc))
            s_scr[hh] = st
            o = jnp.concatenate(outs, axis=0)
            y = o * lax.rsqrt(jnp.mean(o * o, axis=-1, keepdims=True) + EPS) * nrm_ref[:, hs]
            o_ref[:, hs] = (y * jax.nn.sigmoid(g_ref[:, hs])).astype(o_ref.dtype)

    lax.cond(anchored_ok, lambda: run(_gla_intra_anchored), lambda: run(_gla_intra_pairwise))

    @pl.when(c == nc - 1)
    def _():
        for hh in range(n_heads):
            st_ref[hh] = s_scr[hh].T


def _gla(z, l, lb, hg_norm, b, t):
    _, dst, _, _, _ = _segments()
    blk = min(t, 256)
    assert blk % GLA_CHUNK == 0
    nc = t // blk

    hps = 2 if HG_HEADS % 2 == 0 else 1
    w = hps * HD

    def zspec(name):
        assert dst[name] % w == 0
        return pl.BlockSpec((blk, w), lambda bi, h, c: (bi * nc + c, dst[name] // w + h))

    vec = lambda: pl.BlockSpec((None, 1, w), lambda bi, h, c: (l, 0, h))
    return pl.pallas_call(
        functools.partial(_gla_kernel, nc=nc, n_inner=blk // GLA_CHUNK),
        grid=(b, HG_HEADS // hps, nc),
        in_specs=[zspec('h_f'), zspec('h_q'), zspec('h_i'), zspec('h_g'), vec(), vec()],
        out_specs=[pl.BlockSpec((blk, w), lambda bi, h, c: (bi * nc + c, h)),
                   pl.BlockSpec((None, hps, HG_DK, HG_DV), lambda bi, h, c: (bi, h, 0, 0))],
        out_shape=[jax.ShapeDtypeStruct((z.shape[0], HG_HEADS * HG_DV), BF16),
                   jax.ShapeDtypeStruct((b, HG_HEADS, HG_DK, HG_DV), F32)],
        scratch_shapes=[pltpu.VMEM((hps, HG_DV, HG_DK), F32)],
        compiler_params=_params(("parallel", "parallel", "arbitrary")),
        name="hgrn2",
    )(z, z, z, z, lb.reshape(lb.shape[0], 1, -1), hg_norm.reshape(hg_norm.shape[0], 1, -1))


def _to_col(row):
    n = row.shape[1]
    eye = lax.broadcasted_iota(jnp.int32, (n, n), 0) == lax.broadcasted_iota(jnp.int32, (n, n), 1)
    return jnp.sum(jnp.where(eye, row, 0.0), axis=1, keepdims=True)


def _decode_rec_kernel(rq_ref, rk_ref, rv_ref, rg_ref, hf_ref, hq_ref, hi_ref, hg_ref,
                       cos_ref, sin_ref, rn_ref, hn_ref, lb_ref, sr_ref, sh_ref,
                       or_ref, oh_ref, nr_ref, nh_ref):
    cos, sin = cos_ref[...], sin_ref[...]

    def rms(o, g):
        return o * lax.rsqrt(jnp.mean(o * o, axis=-1, keepdims=True) + EPS) * g

    for h in range(RET_HEADS):
        sl = slice(h * HD, (h + 1) * HD)
        gamma = math.exp(_ret_gamma_log(h))
        q = _rot(rq_ref[:, sl], cos, sin, RET_DK // 2)
        k = _rot(rk_ref[:, sl], cos, sin, RET_DK // 2) * (RET_DK ** -0.5)
        v = rv_ref[:, sl]
        s = sr_ref[h]
        o = jnp.sum(q * k, axis=-1, keepdims=True) * v + jnp.sum(_to_col(q * gamma) * s, axis=0, keepdims=True)
        nr_ref[h] = s * gamma + _to_col(k) * v
        or_ref[:, sl] = rms(o, rn_ref[:, sl]) * _silu(rg_ref[:, sl])

    for h in range(HG_HEADS):
        sl = slice(h * HD, (h + 1) * HD)
        log_f, k = _hgrn_gates(hf_ref[:, sl], lb_ref[:, sl])
        q = _silu(hq_ref[:, sl])
        v = hi_ref[:, sl]
        s = sh_ref[h]
        eb = jnp.exp(log_f)
        o = jnp.sum(q * k, axis=-1, keepdims=True) * v + jnp.sum(_to_col(q * eb) * s, axis=0, keepdims=True)
        nh_ref[h] = s * _to_col(eb) + _to_col(k) * v
        oh_ref[:, sl] = rms(o, hn_ref[:, sl]) * jax.nn.sigmoid(hg_ref[:, sl])


def _decode_rec(z, l, cos, sin, ret_norm, hg_norm, lb, state_ret, state_hgrn):
    _, dst, padded, _, _ = _segments()
    nb = z.shape[0]

    z = z.reshape(nb, 1, z.shape[1])

    def zspec(name):
        w = padded[name]
        return pl.BlockSpec((None, 1, w), lambda bi: (bi, 0, dst[name] // w))

    one = lambda w: pl.BlockSpec((1, w), lambda bi: (0, 0))
    vec = lambda a: pl.BlockSpec((None, 1, a.shape[-1]), lambda bi: (l, 0, 0))
    st = lambda a: pl.BlockSpec((None, None) + a.shape[2:], lambda bi: (l, bi, 0, 0, 0))
    st_out = lambda a: pl.BlockSpec((None,) + a.shape[2:], lambda bi: (bi, 0, 0, 0))
    rw, hw = RET_HEADS * RET_DV, HG_HEADS * HG_DV
    r3 = lambda a: a.reshape(a.shape[0], 1, -1)
    return pl.pallas_call(
        _decode_rec_kernel,
        grid=(nb,),
        in_specs=[zspec('r_q'), zspec('r_k'), zspec('r_v'), zspec('r_g'),
                  zspec('h_f'), zspec('h_q'), zspec('h_i'), zspec('h_g'),
                  one(LANE), one(LANE), vec(ret_norm), vec(hg_norm), vec(lb), st(state_ret), st(state_hgrn)],
        out_specs=[pl.BlockSpec((None, 1, rw), lambda bi: (bi, 0, 0)),
                   pl.BlockSpec((None, 1, hw), lambda bi: (bi, 0, 0)),
                   st_out(state_ret), st_out(state_hgrn)],
        out_shape=[jax.ShapeDtypeStruct((nb, 1, rw), F32), jax.ShapeDtypeStruct((nb, 1, hw), F32),
                   jax.ShapeDtypeStruct(state_ret.shape[1:], F32), jax.ShapeDtypeStruct(state_hgrn.shape[1:], F32)],
        compiler_params=_params(("arbitrary",)),
        name="decode_recurrent",
    )(z, z, z, z, z, z, z, z, cos, sin, r3(ret_norm), r3(hg_norm), r3(lb), state_ret, state_hgrn)


def _sort_key(score):
    bits = lax.bitcast_convert_type(score + 0.0, jnp.int32)
    return bits ^ ((bits >> 31) & jnp.int32(0x7FFFFFFF))


def _count(mask):
    return jnp.sum(mask.astype(F32), axis=-1, keepdims=True)


def _nth_largest_key(count_ge, n_sel, shape):
    return _nth_largest_keys([count_ge], n_sel, shape)[0]


def _nth_largest_keys(count_fns, n_sel, shape):
    zero = jnp.zeros(shape, jnp.int32)
    los = tuple(jnp.where(f(zero) >= n_sel, 0, INT_MIN).astype(jnp.int32) for f in count_fns)

    def body(i, los):
        bit = jnp.left_shift(jnp.int32(1), 30 - i)
        return tuple(jnp.where(f(lo | bit) >= n_sel, lo | bit, lo) for f, lo in zip(count_fns, los))

    los = body(0, los)
    return lax.fori_loop(1, 31, body, los, unroll=SEARCH_UNROLL)


def _tie_bound(count_eq_below, need, nbits, shape):
    def body(i, j):
        cand = j | jnp.left_shift(jnp.int32(1), nbits - 1 - i)
        return jnp.where(count_eq_below(cand) < need, cand, j)

    return lax.fori_loop(0, nbits, body, jnp.zeros(shape, jnp.int32))


def _dsa_prompt_body(length, aq_ref, iq_ref, iw_ref, k_ref, v_ref, ik_ref, o_ref, key_scr, sel_scr, n_sel):
    j = pl.program_id(1)
    tq = aq_ref.shape[0]
    ik = ik_ref[0:length, :]
    score = jnp.zeros((tq, length), F32)
    for h in range(IDX_HEADS):
        s = lax.dot_general(iq_ref[:, h * HD:(h + 1) * HD], ik, _NT, preferred_element_type=F32)
        score = score + jnp.maximum(s, 0.0) * iw_ref[:, h:h + 1]
    q_pos = j * tq + lax.broadcasted_iota(jnp.int32, (tq, 1), 0)
    col = lax.broadcasted_iota(jnp.int32, (tq, length), 1)
    visible = col <= q_pos
    key_scr[:, 0:length] = _sort_key(jnp.where(visible, score, NEG_BIG))

    rows = tq // SEARCH_GROUPS
    group_count = lambda r: (lambda c: _count(key_scr[r * rows:(r + 1) * rows, 0:length] >= c))
    tau = jnp.concatenate(
        _nth_largest_keys([group_count(r) for r in range(SEARCH_GROUPS)], n_sel, (rows, 1)), axis=0)
    key = key_scr[:, 0:length]
    gt = key > tau
    eq = key == tau
    need = n_sel - _count(gt)
    spare = jnp.max(_count(eq & visible) - need) > 0.0
    bound = lax.cond(
        spare,
        lambda: _tie_bound(lambda c: _count((key_scr[:, 0:length] == tau) & (col < c)), need,
                           max(1, length.bit_length()), (tq, 1)),
        lambda: jnp.full((tq, 1), length, jnp.int32))
    sel_scr[:, 0:length] = ((gt | (eq & (col <= bound))) & visible).astype(F32)

    group = ATT_HEADS // ATT_KV_HEADS
    scale = ATT_HEAD_DIM ** -0.5
    for n in range(ATT_KV_HEADS):
        kn = k_ref[0:length, n * HD:(n + 1) * HD]
        vn = v_ref[0:length, n * HD:(n + 1) * HD]
        for g in range(group):
            sl = slice((n * group + g) * HD, (n * group + g + 1) * HD)
            s = lax.dot_general(aq_ref[:, sl], kn, _NT, preferred_element_type=F32) * scale
            s = jnp.where(sel_scr[:, 0:length] > 0.0, s, NEG_BIG)
            m = jnp.max(s, axis=-1, keepdims=True)
            p = jnp.exp(s - m)
            o = _bdot(p, vn) / jnp.sum(p, axis=-1, keepdims=True)
            o_ref[:, sl] = o.astype(o_ref.dtype)


def _dsa_prompt_kernel(*refs, n_sel, lengths):
    j = pl.program_id(1)
    tq = refs[0].shape[0]
    prev = 0
    for length in lengths:
        @pl.when((j >= prev // tq) & (j < length // tq))
        def _(length=length):
            _dsa_prompt_body(length, *refs, n_sel)
        prev = length


def _dsa_prompt(aq, iq, iw, kb, vb, ikb, b, t, mt):
    tq = min(Q_BLOCK, t)
    nq = t // tq
    n_sel = min(TOPK_MAX, t // 4)
    step = min(t, 512)
    lengths = tuple(range(step, t + 1, step))
    assert t % step == 0 and step % tq == 0 and step >= n_sel
    qrow = lambda w: pl.BlockSpec((tq, w), lambda bi, j: (bi * nq + j, 0))
    krow = lambda w: pl.BlockSpec((t, w), lambda bi, j: (bi, 0))
    return pl.pallas_call(
        functools.partial(_dsa_prompt_kernel, n_sel=n_sel, lengths=lengths),
        grid=(b, nq),
        in_specs=[qrow(aq.shape[1]), qrow(iq.shape[1]), qrow(LANE),
                  krow(kb.shape[1]), krow(vb.shape[1]), krow(HD)],
        out_specs=qrow(aq.shape[1]),
        out_shape=jax.ShapeDtypeStruct((mt, aq.shape[1]), BF16),
        scratch_shapes=[pltpu.VMEM((tq, t), jnp.int32), pltpu.VMEM((tq, t), F32)],
        compiler_params=_params(("parallel", "arbitrary")),
        name="dsa_prompt",
    )(aq, iq, iw, kb, vb, ikb)


def _dec_score_kernel(pt_ref, iq_ref, iw_ref, ikn_ref, *refs):
    pages = refs[:PAGES_PER_STEP]
    sc_ref, new_ref = refs[PAGES_PER_STEP:]
    iq = iq_ref[...].astype(BF16)
    iw = iw_ref[...]
    ik = jnp.concatenate([p[...].astype(BF16) for p in pages], axis=0)
    s = lax.dot_general(iq, ik, _NT, preferred_element_type=F32)
    sc_ref[...] = jnp.sum(jnp.maximum(s, 0.0) * iw, axis=0, keepdims=True)

    @pl.when(pl.program_id(1) == 0)
    def _():
        sn = lax.dot_general(iq, jnp.broadcast_to(ikn_ref[...], (8, HD)).astype(BF16), _NT,
                             preferred_element_type=F32)[:, 0:1]
        new_ref[...] = jnp.broadcast_to(jnp.sum(jnp.maximum(sn, 0.0) * iw, axis=0, keepdims=True), (1, LANE))


def _dec_scores(page_table, iq, iw, ik_new, cache_ik, l):
    nb, n_pages = page_table.shape
    steps = n_pages // PAGES_PER_STEP
    ih = iq.shape[1]
    page = lambda r: pl.BlockSpec((None, None, PAGE_SIZE, HD),
                                  lambda bi, p, pt: (l, pt[bi, p * PAGES_PER_STEP + r], 0, 0))
    grid_spec = pltpu.PrefetchScalarGridSpec(
        num_scalar_prefetch=1,
        grid=(nb, steps),
        in_specs=[pl.BlockSpec((None, ih, HD), lambda bi, p, pt: (bi, 0, 0)),
                  pl.BlockSpec((None, ih, 1), lambda bi, p, pt: (bi, 0, 0)),
                  pl.BlockSpec((None, 1, HD), lambda bi, p, pt: (bi, 0, 0))]
                 + [page(r) for r in range(PAGES_PER_STEP)],
        out_specs=[pl.BlockSpec((None, 1, PAGES_PER_STEP * PAGE_SIZE), lambda bi, p, pt: (bi, 0, p)),
                   pl.BlockSpec((None, 1, LANE), lambda bi, p, pt: (bi, 0, 0))],
    )
    return pl.pallas_call(
        _dec_score_kernel,
        grid_spec=grid_spec,
        out_shape=[jax.ShapeDtypeStruct((nb, 1, n_pages * PAGE_SIZE), F32),
                   jax.ShapeDtypeStruct((nb, 1, LANE), F32)],
        compiler_params=_params(("parallel", "arbitrary")),
        name="dec_scores",
    )(page_table, iq, iw, ik_new, *([cache_ik] * PAGES_PER_STEP))


def _dec_select_kernel(sc_ref, new_ref, tau_ref, bnd_ref, seln_ref, *, n_sel, nbits):
    key = _sort_key(sc_ref[...])
    key_new = _sort_key(new_ref[...])[:, 0:1]
    nb, s_len = key.shape
    idx = lax.broadcasted_iota(jnp.int32, key.shape, 1)
    tau = _nth_largest_key(lambda c: _count(key >= c) + (key_new >= c).astype(F32), n_sel, (nb, 1))
    eq = key == tau
    need = n_sel - _count(key > tau) - (key_new > tau).astype(F32)
    n_eq = _count(eq)
    bound = lax.cond(
        jnp.max(n_eq - need) > 0.0,
        lambda: _tie_bound(lambda c: _count(eq & (idx < c)), need, nbits, (nb, 1)),
        lambda: jnp.full((nb, 1), s_len, jnp.int32))
    sel_new = (key_new > tau) | ((key_new == tau) & (n_eq < need))
    tau_ref[...] = jnp.broadcast_to(tau, tau_ref.shape)
    bnd_ref[...] = jnp.broadcast_to(bound, bnd_ref.shape)
    seln_ref[...] = jnp.broadcast_to(sel_new.astype(F32), seln_ref.shape)


def _dec_select(scores, score_new, n_sel):
    nb, s_len = scores.shape
    out = lambda dt: jax.ShapeDtypeStruct((nb, LANE), dt)
    return pl.pallas_call(
        functools.partial(_dec_select_kernel, n_sel=n_sel, nbits=s_len.bit_length()),
        out_shape=[out(jnp.int32), out(jnp.int32), out(F32)],
        compiler_params=pltpu.CompilerParams(vmem_limit_bytes=VMEM_LIMIT),
        name="dec_select",
    )(scores, score_new)


def _dec_attn_kernel(pt_ref, tau_ref, bnd_ref, seln_ref, sc_ref, q_ref, kn_ref, vn_ref, *refs, steps):
    kp = refs[:PAGES_PER_STEP]
    vp = refs[PAGES_PER_STEP:2 * PAGES_PER_STEP]
    o_ref, m_scr, l_scr, acc_scr = refs[2 * PAGES_PER_STEP:]
    p_id = pl.program_id(1)
    chunk = PAGES_PER_STEP * PAGE_SIZE
    scale = ATT_HEAD_DIM ** -0.5

    @pl.when(p_id == 0)
    def _():
        m_scr[...] = jnp.full_like(m_scr, NEG_BIG)
        l_scr[...] = jnp.zeros_like(l_scr)
        acc_scr[...] = jnp.zeros_like(acc_scr)

    tau = tau_ref[:, 0:1]
    bound = bnd_ref[:, 0:1]
    key_c = _sort_key(sc_ref[...])
    idx_c = p_id * chunk + lax.broadcasted_iota(jnp.int32, key_c.shape, 1)
    sel = (key_c > tau) | ((key_c == tau) & (idx_c <= bound))
    q = q_ref[...].astype(BF16)
    heads = lambda pages, n: jnp.concatenate(
        [r[pl.ds(n, PAGE_SIZE, stride=ATT_KV_HEADS), :].astype(BF16) for r in pages], axis=0)
    s = sum(lax.dot_general(q[:, n * HD:(n + 1) * HD], heads(kp, n), _NT, preferred_element_type=F32)
            for n in range(ATT_KV_HEADS)) * scale
    s = jnp.where(sel, s, NEG_BIG)
    m_old = m_scr[:, 0:1]
    m_new = jnp.maximum(m_old, jnp.max(s, axis=-1, keepdims=True))
    alpha = jnp.exp(m_old - m_new)
    p = jnp.where(sel, jnp.exp(s - m_new), 0.0)
    l_new = alpha * l_scr[:, 0:1] + jnp.sum(p, axis=-1, keepdims=True)
    pb = p.astype(BF16)
    acc_new = alpha * acc_scr[...] + jnp.concatenate(
        [jnp.dot(pb, heads(vp, n), preferred_element_type=F32) for n in range(ATT_KV_HEADS)], axis=1)
    m_scr[...] = jnp.broadcast_to(m_new, m_scr.shape)
    l_scr[...] = jnp.broadcast_to(l_new, l_scr.shape)
    acc_scr[...] = acc_new

    @pl.when(p_id == steps - 1)
    def _():
        sel_new = seln_ref[:, 0:1] > 0.0
        qf = q_ref[...]
        s_new = jnp.sum(qf * kn_ref[...], axis=-1, keepdims=True) * scale
        s_new = jnp.where(sel_new, s_new, NEG_BIG)
        m_fin = jnp.maximum(m_new, s_new)
        a2 = jnp.exp(m_new - m_fin)
        p_new = jnp.where(sel_new, jnp.exp(s_new - m_fin), 0.0)
        l_fin = a2 * l_new + p_new
        acc_fin = (a2 * acc_new + p_new * vn_ref[...]) / l_fin
        group = ATT_HEADS // ATT_KV_HEADS
        for h in range(ATT_HEADS):
            n = h // group
            o_ref[:, h * HD:(h + 1) * HD] = acc_fin[h:h + 1, n * HD:(n + 1) * HD]


def _dec_attn(page_table, scores, score_new, q_bd, k_new, v_new, cache_k, cache_v, l):
    nb, n_pages = page_table.shape
    steps = n_pages // PAGES_PER_STEP
    s_len = n_pages * PAGE_SIZE
    chunk = PAGES_PER_STEP * PAGE_SIZE
    n_sel = min(TOPK_MAX, (s_len + 1) // 4)
    tau, bound, sel_new = _dec_select(scores.reshape(nb, s_len), score_new.reshape(nb, LANE), n_sel)
    row = lambda a: a.reshape(nb, 1, LANE)
    row_spec = pl.BlockSpec((None, 1, LANE), lambda bi, p, pt: (bi, 0, 0))
    kvw = ATT_KV_HEADS * HD
    hp = q_bd.shape[1]
    page = lambda r: pl.BlockSpec((None, None, PAGE_SIZE * ATT_KV_HEADS, HD),
                                  lambda bi, p, pt: (l, pt[bi, p * PAGES_PER_STEP + r], 0, 0))
    grid_spec = pltpu.PrefetchScalarGridSpec(
        num_scalar_prefetch=1,
        grid=(nb, steps),
        in_specs=[row_spec, row_spec, row_spec,
                  pl.BlockSpec((None, 1, chunk), lambda bi, p, pt: (bi, 0, p)),
                  pl.BlockSpec((None, hp, kvw), lambda bi, p, pt: (bi, 0, 0)),
                  pl.BlockSpec((None, 1, kvw), lambda bi, p, pt: (bi, 0, 0)),
                  pl.BlockSpec((None, 1, kvw), lambda bi, p, pt: (bi, 0, 0))]
                 + [page(r) for r in range(PAGES_PER_STEP)] * 2,
        out_specs=pl.BlockSpec((None, 1, ATT_HEADS * HD), lambda bi, p, pt: (bi, 0, 0)),
        scratch_shapes=[pltpu.VMEM((hp, LANE), F32), pltpu.VMEM((hp, LANE), F32),
                        pltpu.VMEM((hp, kvw), F32)],
    )
    return pl.pallas_call(
        functools.partial(_dec_attn_kernel, steps=steps),
        grid_spec=grid_spec,
        out_shape=jax.ShapeDtypeStruct((nb, 1, ATT_HEADS * HD), F32),
        compiler_params=_params(("parallel", "arbitrary")),
        name="dec_attn",
    )(page_table, row(tau), row(bound), row(sel_new), scores, q_bd, k_new, v_new,
      *([cache_k] * PAGES_PER_STEP), *([cache_v] * PAGES_PER_STEP)).reshape(nb, ATT_HEADS * HD)


def _ffn(x, xb, ssq, w1, w3, w2, l, tm, next_gain):
    tm_up = 2 * tm if (x.shape[0] // tm) % 2 == 0 else tm
    g, w2b = _matmul(xb, [w1, w3], l, tm=tm_up, epilogue='swiglu', out_dtype=BF16, ssq=ssq, cast_src=(w2, l))
    return _matmul(g, [w2b[None]], 0, tm=tm, epilogue='resid', res=x, scale=0.5, next_gain=next_gain)


def _token_tiles(n_tokens):
    n_tiles = max(1, n_tokens // 1024)
    tm = -(-n_tokens // (16 * n_tiles)) * 16
    return tm, n_tiles


def kernel(x_prompt, x_sample, state_ret, state_hgrn, cache_k, cache_v, cache_idx_k, page_table, ffn1_norm, ffn1_w1, ffn1_w3, ffn1_w2, mix_norm, w_in, ret_norm, q_norm, k_norm, idx_k_g, idx_k_b, hg_lb_raw, hg_norm, w_up_ret, w_up_att, w_up_hg, w_out, ffn2_norm, ffn2_w1, ffn2_w3, ffn2_w2):
    b, t, d = x_prompt.shape
    nb = x_sample.shape[0]
    depth = w_in.shape[0]
    kvw = ATT_KV_HEADS * ATT_HEAD_DIM
    group = ATT_HEADS // ATT_KV_HEADS

    lb_soft = jax.nn.softmax(hg_lb_raw.astype(F32), axis=0)
    lb_all = jnp.cumsum(lb_soft, axis=0) - lb_soft[0]
    cast = lambda w: w.astype(BF16)
    w_up_ret, w_up_att, w_up_hg, w_out = cast(w_up_ret), cast(w_up_att), cast(w_up_hg), cast(w_out)

    pos_p = jnp.arange(t, dtype=jnp.int32)
    pos_s = jnp.full((nb,), PAST_LEN, jnp.int32)
    ret_f = 1.0 / (ROPE_THETA ** jnp.linspace(0.0, 1.0, RET_DK // 2, dtype=F32))
    att_f = ROPE_THETA ** (-jnp.arange(0, ATT_HEAD_DIM, 2, dtype=F32) / ATT_HEAD_DIM)
    idx_f = ROPE_THETA ** (-jnp.arange(0, IDX_ROPE_DIM, 2, dtype=F32) / IDX_ROPE_DIM)
    tabs_p = _rope_tables(pos_p, att_f, ATT_HEAD_DIM) + _rope_tables(pos_p, idx_f, IDX_ROPE_DIM)
    tabs_s = _rope_tables(pos_s, att_f, ATT_HEAD_DIM) + _rope_tables(pos_s, idx_f, IDX_ROPE_DIM)
    ret_tab_p = _rope_tables(pos_p, ret_f, RET_DK)
    ret_tab_s = _rope_tables(pos_s[:1], ret_f, RET_DK)

    ck = cache_k.reshape(cache_k.shape[:2] + (PAGE_SIZE * ATT_KV_HEADS, HD))
    cv = cache_v.reshape(cache_v.shape[:2] + (PAGE_SIZE * ATT_KV_HEADS, HD))

    mp = b * t
    tm, n_tiles = _token_tiles(mp + nb)
    mt = tm * n_tiles
    x = jnp.concatenate([x_prompt.reshape(mp, d), x_sample.reshape(nb, d), jnp.zeros((mt - mp - nb, d), F32)])

    def with_tail(a, rows):
        tail = jnp.concatenate([rows.astype(a.dtype), jnp.zeros((mt - mp - nb, a.shape[1]), a.dtype)])
        return lax.dynamic_update_slice(a, tail, (mp, 0))

    outs = {n: [] for n in ('rp', 'rs', 'hp', 'hs', 'kp', 'vp', 'ip', 'ks', 'vs', 'is')}
    xb, ssq = _row_prep(x, ffn1_norm, 0)
    for l in range(depth):
        x, xb, ssq = _ffn(x, xb, ssq, ffn1_w1, ffn1_w3, ffn1_w2, l, tm, (mix_norm, l))
        z = _in_proj(xb, ssq, w_in, l, 2 * tm if n_tiles % 2 == 0 else tm)

        aq, ak, av, akb, avb, iq, ik, ikb, iw = _prelude(z, mp, l, tabs_p, q_norm, k_norm, idx_k_g, idx_k_b, t)
        a_ret, r_p = _retention(z, l, ret_tab_p[0], ret_tab_p[1], ret_norm, b, t)
        a_hg, h_p = _gla(z, l, lb_all, hg_norm, b, t)
        a_att = _dsa_prompt(aq, iq, iw, akb, avb, ikb, b, t, mt)
        outs['rp'].append(r_p); outs['hp'].append(h_p)
        outs['kp'].append(ak.reshape(b, t, ATT_KV_HEADS, ATT_HEAD_DIM))
        outs['vp'].append(av.reshape(b, t, ATT_KV_HEADS, ATT_HEAD_DIM))
        outs['ip'].append(ik.reshape(b, t, IDX_DIM))

        zs = z[mp:mp + nb]
        aq, ak, av, _, _, iq, ik, _, iw = _prelude(zs, nb, l, tabs_s, q_norm, k_norm, idx_k_g, idx_k_b, 1)
        s_ret, s_hg, r_s, h_s = _decode_rec(zs, l, ret_tab_s[0], ret_tab_s[1], ret_norm, hg_norm, lb_all,
                                            state_ret, state_hgrn)
        scores, score_new = _dec_scores(page_table, iq.reshape(nb, IDX_HEADS, IDX_DIM),
                                        iw[:, :IDX_HEADS].reshape(nb, IDX_HEADS, 1),
                                        ik.reshape(nb, 1, IDX_DIM), cache_idx_k, l)
        qh = aq.reshape(nb, ATT_KV_HEADS, group, ATT_HEAD_DIM)
        q_bd = jnp.concatenate(
            [jnp.concatenate([qh[:, n] if m == n else jnp.zeros_like(qh[:, n]) for m in range(ATT_KV_HEADS)], axis=-1)
             for n in range(ATT_KV_HEADS)]
            + [jnp.zeros((nb, max(0, 16 - ATT_HEADS), kvw), aq.dtype)], axis=1)
        s_att = _dec_attn(page_table, scores, score_new, q_bd, ak.reshape(nb, 1, kvw), av.reshape(nb, 1, kvw),
                          ck, cv, l)
        outs['rs'].append(r_s); outs['hs'].append(h_s)
        outs['ks'].append(ak.reshape(nb, 1, ATT_KV_HEADS, ATT_HEAD_DIM))
        outs['vs'].append(av.reshape(nb, 1, ATT_KV_HEADS, ATT_HEAD_DIM))
        outs['is'].append(ik.reshape(nb, 1, IDX_DIM))

        merged = _merge(with_tail(a_ret, s_ret.reshape(nb, -1)), with_tail(a_att, s_att),
                        with_tail(a_hg, s_hg.reshape(nb, -1)), w_up_ret, w_up_att, w_up_hg, z, l, BF16, tm)
        x, xb, ssq = _matmul(merged, [w_out], l, tm=tm, epilogue='resid', res=x, scale=1.0,
                             next_gain=(ffn2_norm, l))
        if l + 1 < depth:
            x, xb, ssq = _ffn(x, xb, ssq, ffn2_w1, ffn2_w3, ffn2_w2, l, tm, (ffn1_norm, l + 1))
        else:
            x = _ffn(x, xb, ssq, ffn2_w1, ffn2_w3, ffn2_w2, l, tm, None)

    st = lambda n: jnp.stack(outs[n])
    return (x[:mp].reshape(b, t, d), x[mp:mp + nb].reshape(nb, 1, d),
            st('rp').astype(state_ret.dtype), st('rs').astype(state_ret.dtype),
            st('hp').astype(state_hgrn.dtype), st('hs').astype(state_hgrn.dtype),
            st('kp'), st('vp'), st('ip'), st('ks'), st('vs'), st('is'))
```

```python
import functools
import math

import numpy as np
import jax
import jax.numpy as jnp
from jax import lax
from jax.experimental import pallas as pl
from jax.experimental.pallas import tpu as pltpu

D_MODEL = 4096
BATCH = 4
SEQ = 2048
DEPTH = 2
DEC_BATCH = 8
DEC_SEQ = 1
PAST_LEN = 16384
PAGE_SIZE = 128

RET_HEADS = 8
RET_DK = 128
RET_DV = 128
ATT_HEADS = 8
ATT_KV_HEADS = 2
ATT_HEAD_DIM = 128
IDX_HEADS = 16
IDX_DIM = 128
IDX_ROPE_DIM = 64
TOPK_MAX = 256
Q_BLOCK = 128
HG_HEADS = 8
HG_DK = 128
HG_DV = 128
D_FF = 11008
ROPE_THETA = 10000.0
EPS = 1e-6
NEG_BIG = -1e30

F32 = jnp.float32
BF16 = jnp.bfloat16
LANE = 128
HD = 128
INT_MIN = -(2 ** 31)
VMEM_LIMIT = 56 * 1024 * 1024
GLA_CHUNK = 64
GLA_SUB = 16
SEARCH_GROUPS = 4
SEARCH_UNROLL = 6
GLA_SAFE_SPAN = 60.0
PAGES_PER_STEP = 16

_NT = (((1,), (1,)), ((), ()))
_TN = (((0,), (0,)), ((), ()))


def _params(sem):
    return pltpu.CompilerParams(dimension_semantics=sem, vmem_limit_bytes=VMEM_LIMIT)


def _bdot(a, b, dims=None):
    a = a.astype(BF16)
    b = b.astype(BF16)
    if dims is None:
        return jnp.dot(a, b, preferred_element_type=F32)
    return lax.dot_general(a, b, dims, preferred_element_type=F32)


def _silu(x):
    return x * jax.nn.sigmoid(x)


def _segments():
    ret_qk = RET_HEADS * RET_DK
    ret_w = RET_HEADS * RET_DV
    att_w = ATT_HEADS * ATT_HEAD_DIM
    kv_w = ATT_KV_HEADS * ATT_HEAD_DIM
    hg_k = HG_HEADS * HG_DK
    hg_w = HG_HEADS * HG_DV
    names = ['r_q', 'r_k', 'r_v', 'r_g', 'a_q', 'a_k', 'a_v', 'i_q', 'i_k', 'i_w',
             'h_f', 'h_q', 'h_i', 'h_g', 'g_ret', 'g_att', 'g_hg']
    widths = [ret_qk, ret_qk, ret_w, ret_w, att_w, kv_w, kv_w, IDX_HEADS * IDX_DIM, IDX_DIM, IDX_HEADS,
              hg_k, hg_k, hg_w, hg_w, D_MODEL, D_MODEL, D_MODEL]
    src = {}
    off = 0
    for n, w in zip(names, widths):
        src[n] = (off, w)
        off += w
    padded = {n: -(-w // LANE) * LANE for n, w in zip(names, widths)}
    order = ['g_ret', 'g_att', 'g_hg'] + sorted(
        [n for n in names if not n.startswith('g_')], key=lambda n: -padded[n])
    dst = {}
    off = 0
    for n in order:
        dst[n] = off
        off += padded[n]
    total = -(-off // 256) * 256
    for n in order:
        assert dst[n] % padded[n] == 0 or n.startswith('g_'), (n, dst[n], padded[n])
    return src, dst, padded, order, total


def _row_prep_kernel(x_ref, g_ref, xb_ref, ssq_ref):
    x = x_ref[...]
    xb_ref[...] = (x * g_ref[...]).astype(xb_ref.dtype)
    ssq_ref[...] = jnp.broadcast_to(jnp.sum(x * x, axis=-1, keepdims=True), ssq_ref.shape)


def _row_tile(m, cap):
    best = None
    for t in range(16, min(m, cap) + 1, 16):
        if m % t == 0:
            best = t
    return best if best is not None else m


def _row_prep(x, g_all, l):
    m, d = x.shape
    tr = _row_tile(m, 512)
    return pl.pallas_call(
        _row_prep_kernel,
        grid=(m // tr,),
        in_specs=[pl.BlockSpec((tr, d), lambda i: (i, 0)),
                  pl.BlockSpec((None, 1, d), lambda i: (l, 0, 0))],
        out_specs=[pl.BlockSpec((tr, d), lambda i: (i, 0)), pl.BlockSpec((tr, LANE), lambda i: (i, 0))],
        out_shape=[jax.ShapeDtypeStruct((m, d), BF16), jax.ShapeDtypeStruct((m, LANE), F32)],
        compiler_params=_params(("parallel",)),
        name="row_prep",
    )(x, g_all.reshape(g_all.shape[0], 1, d))


def _mm_kernel(*refs, n_w, epilogue, scale, row_norm, feed_norm, cast_rows):
    it = iter(refs)
    lhs_ref = next(it)
    w_refs = [next(it) for _ in range(n_w)]
    res_ref = next(it) if epilogue == 'resid' else None
    ssq_ref = next(it) if row_norm else None
    gain_ref = next(it) if feed_norm else None
    cast_in_ref = next(it) if cast_rows else None
    out_ref = next(it)
    xb_ref, ssq_out_ref = (next(it), next(it)) if feed_norm else (None, None)
    if cast_rows:
        next(it)[...] = cast_in_ref[...].astype(BF16)

    lhs = lhs_ref[...].astype(BF16)
    vals = [jnp.dot(lhs, w[...].astype(BF16), preferred_element_type=F32) for w in w_refs]
    if row_norm:
        r = lax.rsqrt(ssq_ref[:, 0:1] * (1.0 / lhs_ref.shape[1]) + EPS)
        vals = [v * r for v in vals]
    if epilogue == 'swiglu':
        out = _silu(vals[0]) * vals[1]
    elif epilogue == 'resid':
        out = res_ref[...] + scale * vals[0]
    else:
        out = vals[0]
    out_ref[...] = out.astype(out_ref.dtype)
    if feed_norm:
        xb_ref[...] = (out * gain_ref[...]).astype(xb_ref.dtype)
        part = jnp.broadcast_to(jnp.sum(out * out, axis=-1, keepdims=True), ssq_out_ref.shape)
        j = pl.program_id(1)

        @pl.when(j == 0)
        def _():
            ssq_out_ref[...] = part

        @pl.when(j > 0)
        def _():
            ssq_out_ref[...] += part


def _pick(n, cands):
    for c in cands:
        if n % c == 0:
            return c
    return n


def _matmul(lhs, ws, l, *, tm, epilogue='plain', res=None, scale=1.0, out_dtype=F32, tn=256, single_lhs=None,
            ssq=None, next_gain=None, cast_src=None):
    assert next_gain is None or cast_src is None
    m, kdim = lhs.shape
    n = ws[0].shape[-1]
    tn = _pick(n, (tn, 256, 128))
    assert m % tm == 0 and n % tn == 0
    if single_lhs is None:
        single_lhs = tm * kdim * lhs.dtype.itemsize > (12 << 20)
    lhs_mode = dict(pipeline_mode=pl.Buffered(1)) if single_lhs else {}
    in_specs = [pl.BlockSpec((tm, kdim), lambda i, j: (i, 0), **lhs_mode)]
    in_specs += [pl.BlockSpec((None, kdim, tn), lambda i, j: (l, 0, j)) for _ in ws]
    args = [lhs] + list(ws)
    tile = pl.BlockSpec((tm, tn), lambda i, j: (i, j))
    rows = pl.BlockSpec((tm, LANE), lambda i, j: (i, 0))
    if epilogue == 'resid':
        in_specs.append(tile)
        args.append(res)
    if ssq is not None:
        in_specs.append(rows)
        args.append(ssq)
    out_specs, out_shape = tile, jax.ShapeDtypeStruct((m, n), out_dtype)
    if next_gain is not None:
        gains, gl = next_gain
        in_specs.append(pl.BlockSpec((None, 1, tn), lambda i, j: (gl, 0, j)))
        args.append(gains.reshape(gains.shape[0], 1, n))
        out_specs = [tile, tile, rows]
        out_shape = [out_shape, jax.ShapeDtypeStruct((m, n), BF16), jax.ShapeDtypeStruct((m, LANE), F32)]
    cast_rows = 0
    if cast_src is not None:
        src, cl = cast_src
        steps, nj = (m // tm) * (n // tn), n // tn
        cast_rows = src.shape[1] // steps
        assert cast_rows * steps == src.shape[1] and cast_rows % 16 == 0
        in_specs.append(pl.BlockSpec((None, cast_rows, src.shape[2]), lambda i, j: (cl, i * nj + j, 0)))
        args.append(src)
        out_specs = [out_specs, pl.BlockSpec((cast_rows, src.shape[2]), lambda i, j: (i * nj + j, 0))]
        out_shape = [out_shape, jax.ShapeDtypeStruct(src.shape[1:], BF16)]
    return pl.pallas_call(
        functools.partial(_mm_kernel, n_w=len(ws), epilogue=epilogue, scale=scale,
                          row_norm=ssq is not None, feed_norm=next_gain is not None, cast_rows=cast_rows),
        grid=(m // tm, n // tn),
        in_specs=in_specs,
        out_specs=out_specs,
        out_shape=out_shape,
        compiler_params=_params(("parallel", "arbitrary")),
        name="mm_" + epilogue,
    )(*args)


def _in_proj_kernel(start_ref, lhs_ref, w_ref, ssq_ref, out_ref):
    w = w_ref[0].astype(BF16)
    acc = lax.dot_general(lhs_ref[...], w, _NT, preferred_element_type=F32)
    out_ref[...] = acc * lax.rsqrt(ssq_ref[:, 0:1] * (1.0 / lhs_ref.shape[1]) + EPS)


def _in_proj(xb, ssq, w_in, l, tm):
    src, dst, padded, order, total = _segments()
    depth, kdim, n_in = w_in.shape
    m = xb.shape[0]
    tn = 256
    col_src = np.zeros((total,), np.int64)
    for n in order:
        col_src[dst[n]:dst[n] + padded[n]] = src[n][0] + np.arange(padded[n])
    start = col_src[::tn].copy()
    assert (col_src.reshape(-1, tn) == start[:, None] + np.arange(tn)).all(), "tile is not one source range"
    assert (start % 8 == 0).all() and (start + tn <= n_in).all()
    grid_spec = pltpu.PrefetchScalarGridSpec(
        num_scalar_prefetch=1,
        grid=(m // tm, total // tn),
        in_specs=[pl.BlockSpec((tm, kdim), lambda i, j, start: (i, 0), pipeline_mode=pl.Buffered(1)),
                  pl.BlockSpec((pl.Element(1), pl.Element(tn), pl.Element(kdim)),
                               lambda i, j, start: (l, start[j] * 8, 0)),
                  pl.BlockSpec((tm, LANE), lambda i, j, start: (i, 0))],
        out_specs=pl.BlockSpec((tm, tn), lambda i, j, start: (i, j)),
    )
    return pl.pallas_call(
        _in_proj_kernel,
        grid_spec=grid_spec,
        out_shape=jax.ShapeDtypeStruct((m, total), F32),
        compiler_params=_params(("parallel", "arbitrary")),
        name="in_proj",
    )(jnp.asarray(start // 8, jnp.int32), xb, jnp.swapaxes(w_in, 1, 2), ssq)


def _merge_kernel(ar_ref, aa_ref, ah_ref, wr_ref, wa_ref, wh_ref, gr_ref, ga_ref, gh_ref, o_ref):
    u_r = _bdot(ar_ref[...], wr_ref[...])
    u_a = _bdot(aa_ref[...], wa_ref[...])
    u_h = _bdot(ah_ref[...], wh_ref[...])
    out = (jax.nn.sigmoid(gr_ref[...]) * u_r + jax.nn.sigmoid(ga_ref[...]) * u_a
           + jax.nn.sigmoid(gh_ref[...]) * u_h)
    o_ref[...] = out.astype(o_ref.dtype)


def _merge(a_ret, a_att, a_hg, w_r, w_a, w_h, z, l, out_dtype, tm):
    m = a_ret.shape[0]
    d = w_r.shape[-1]
    tn = _pick(d, (256, 128))
    nb = d // tn
    lhs_spec = lambda a: pl.BlockSpec((tm, a.shape[1]), lambda i, j: (i, 0))
    w_spec = lambda w: pl.BlockSpec((None, w.shape[1], tn), lambda i, j: (l, 0, j))
    gate_spec = lambda g: pl.BlockSpec((tm, tn), lambda i, j: (i, g * nb + j))
    return pl.pallas_call(
        _merge_kernel,
        grid=(m // tm, nb),
        in_specs=[lhs_spec(a_ret), lhs_spec(a_att), lhs_spec(a_hg), w_spec(w_r), w_spec(w_a), w_spec(w_h),
                  gate_spec(0), gate_spec(1), gate_spec(2)],
        out_specs=pl.BlockSpec((tm, tn), lambda i, j: (i, j)),
        out_shape=jax.ShapeDtypeStruct((m, d), out_dtype),
        compiler_params=_params(("parallel", "parallel")),
        name="merge",
    )(a_ret, a_att, a_hg, w_r, w_a, w_h, z, z, z)


def _rot(x, cos, sin, half):
    up = pltpu.roll(x, LANE - half, 1)
    dn = pltpu.roll(x, half, 1)
    lane = lax.broadcasted_iota(jnp.int32, x.shape, 1)
    return x * cos + jnp.where(lane < half, up, dn) * sin


def _rope_tables(pos, freqs, width):
    ang = pos.astype(F32)[:, None] * freqs[None, :]
    cos, sin = jnp.cos(ang), jnp.sin(ang)
    t = pos.shape[0]
    pad_c = jnp.ones((t, LANE - width), F32)
    pad_s = jnp.zeros((t, LANE - width), F32)
    return (jnp.concatenate([cos, cos, pad_c], axis=1), jnp.concatenate([-sin, sin, pad_s], axis=1))


def _prelude_kernel(zq_ref, zk_ref, zv_ref, ziq_ref, zik_ref, ziw_ref, ca_ref, sa_ref, ci_ref, si_ref,
                    qn_ref, kn_ref, ig_ref, ib_ref,
                    aq_ref, ak_ref, av_ref, akb_ref, avb_ref, iq_ref, ik_ref, ikb_ref, iw_ref):
    ca, sa, ci, si = ca_ref[...], sa_ref[...], ci_ref[...], si_ref[...]

    def head_norm(x, g):
        return x * lax.rsqrt(jnp.mean(x * x, axis=-1, keepdims=True) + EPS) * g

    for h in range(ATT_HEADS):
        sl = slice(h * HD, (h + 1) * HD)
        aq_ref[:, sl] = _rot(head_norm(zq_ref[:, sl], qn_ref[...]), ca, sa, HD // 2).astype(aq_ref.dtype)
    for h in range(ATT_KV_HEADS):
        sl = slice(h * HD, (h + 1) * HD)
        k = _rot(head_norm(zk_ref[:, sl], kn_ref[...]), ca, sa, HD // 2)
        ak_ref[:, sl] = k
        akb_ref[:, sl] = k.astype(akb_ref.dtype)
    v = zv_ref[...]
    av_ref[...] = v
    avb_ref[...] = v.astype(avb_ref.dtype)
    for h in range(IDX_HEADS):
        sl = slice(h * HD, (h + 1) * HD)
        iq_ref[:, sl] = (_rot(ziq_ref[:, sl], ci, si, IDX_ROPE_DIM // 2) * (IDX_DIM ** -0.5)).astype(iq_ref.dtype)
    x = zik_ref[...]
    mu = jnp.mean(x, axis=-1, keepdims=True)
    var = jnp.mean(jnp.square(x - mu), axis=-1, keepdims=True)
    ik = _rot((x - mu) * lax.rsqrt(var + EPS) * ig_ref[...] + ib_ref[...], ci, si, IDX_ROPE_DIM // 2)
    ik_ref[...] = ik
    ikb_ref[...] = ik.astype(ikb_ref.dtype)
    iw_ref[...] = ziw_ref[...] * (IDX_HEADS ** -0.5)


def _prelude(z, m, l, tabs, q_norm, k_norm, idx_g, idx_b, t_len):
    _, dst, padded, _, _ = _segments()
    tr = min(m, 256)
    nt = t_len // tr if t_len >= tr else 1
    lowp = BF16 if tr >= 16 else F32
    aw, kvw, iw = padded['a_q'], padded['a_k'], padded['i_q']

    def zspec(name):
        w = padded[name]
        return pl.BlockSpec((tr, w), lambda i: (i, dst[name] // w))

    tab_spec = pl.BlockSpec((tr, LANE), lambda i: (i % nt, 0))
    vec_spec = pl.BlockSpec((None, 1, HD), lambda i: (l, 0, 0))
    row = lambda w: pl.BlockSpec((tr, w), lambda i: (i, 0))
    shp = lambda w, dt: jax.ShapeDtypeStruct((m, w), dt)
    vec = lambda a: a.reshape(a.shape[0], 1, HD)
    return pl.pallas_call(
        _prelude_kernel,
        grid=(m // tr,),
        in_specs=[zspec('a_q'), zspec('a_k'), zspec('a_v'), zspec('i_q'), zspec('i_k'), zspec('i_w'),
                  tab_spec, tab_spec, tab_spec, tab_spec, vec_spec, vec_spec, vec_spec, vec_spec],
        out_specs=[row(aw), row(kvw), row(kvw), row(kvw), row(kvw), row(iw), row(HD), row(HD), row(LANE)],
        out_shape=[shp(aw, lowp), shp(kvw, F32), shp(kvw, F32), shp(kvw, lowp), shp(kvw, lowp),
                   shp(iw, lowp), shp(HD, F32), shp(HD, lowp), shp(LANE, F32)],
        compiler_params=_params(("parallel",)),
        name="attn_prelude",
    )(z, z, z, z, z, z, *tabs, vec(q_norm), vec(k_norm), vec(idx_g), vec(idx_b))


def _ret_gamma_log(h):
    return math.log(1.0 - 2.0 ** (-5.0 - h))


def _retention_kernel(lg_ref, q_ref, k_ref, v_ref, g_ref, cos_ref, sin_ref, nrm_ref, o_ref, st_ref,
                      s_scr, intra_scr, qdec_scr, kdec_scr, *, nc):
    c = pl.program_id(2)
    cr = q_ref.shape[0]
    n_heads = q_ref.shape[1] // HD

    @pl.when(c == 0)
    def _():
        s_scr[...] = jnp.zeros_like(s_scr)
        ti = lax.broadcasted_iota(jnp.int32, (cr, cr), 0)
        si = lax.broadcasted_iota(jnp.int32, (cr, cr), 1)
        diff = (ti - si).astype(F32)
        tcol = lax.broadcasted_iota(jnp.int32, (cr, LANE), 0).astype(F32)
        for hh in range(n_heads):
            lg = lg_ref[hh][:, 0:1]
            intra_scr[hh] = jnp.where(diff >= 0, jnp.exp(lg * jnp.maximum(diff, 0.0)), 0.0)
            qdec_scr[hh] = jnp.exp(lg * (tcol + 1.0))
            kdec_scr[hh] = jnp.exp(lg * (cr - 1.0 - tcol))

    cos, sin = cos_ref[...], sin_ref[...]
    for hh in range(n_heads):
        hs = slice(hh * HD, (hh + 1) * HD)
        q = _rot(q_ref[:, hs], cos, sin, RET_DK // 2)
        k = _rot(k_ref[:, hs], cos, sin, RET_DK // 2) * (RET_DK ** -0.5)
        v = v_ref[:, hs]
        s_dec = jnp.exp(lg_ref[hh][:, 0:1] * cr)
        s = s_scr[hh]
        a = _bdot(q, k, _NT) * intra_scr[hh]
        o = _bdot(a, v) + _bdot(q * qdec_scr[hh], s)
        s_scr[hh] = s * s_dec + _bdot(k * kdec_scr[hh], v, _TN)
        y = o * lax.rsqrt(jnp.mean(o * o, axis=-1, keepdims=True) + EPS) * nrm_ref[:, hs]
        o_ref[:, hs] = (y * _silu(g_ref[:, hs])).astype(o_ref.dtype)

    @pl.when(c == nc - 1)
    def _():
        st_ref[...] = s_scr[...]


def _retention(z, l, cos, sin, ret_norm, b, t):
    _, dst, _, _, _ = _segments()
    cr = min(t, 256)
    nc = t // cr
    lg = jnp.broadcast_to(
        jnp.asarray([_ret_gamma_log(h) for h in range(RET_HEADS)], F32)[:, None, None], (RET_HEADS, 1, LANE))

    hps = 4 if RET_HEADS % 4 == 0 else 1
    w = hps * HD

    def zspec(name):
        assert dst[name] % w == 0
        return pl.BlockSpec((cr, w), lambda bi, h, c: (bi * nc + c, dst[name] // w + h))

    tab = pl.BlockSpec((cr, LANE), lambda bi, h, c: (c, 0))
    return pl.pallas_call(
        functools.partial(_retention_kernel, nc=nc),
        grid=(b, RET_HEADS // hps, nc),
        in_specs=[pl.BlockSpec((hps, 1, LANE), lambda bi, h, c: (h, 0, 0)),
                  zspec('r_q'), zspec('r_k'), zspec('r_v'), zspec('r_g'), tab, tab,
                  pl.BlockSpec((None, 1, w), lambda bi, h, c: (l, 0, h))],
        out_specs=[pl.BlockSpec((cr, w), lambda bi, h, c: (bi * nc + c, h)),
                   pl.BlockSpec((None, hps, RET_DK, RET_DV), lambda bi, h, c: (bi, h, 0, 0))],
        out_shape=[jax.ShapeDtypeStruct((z.shape[0], RET_HEADS * RET_DV), BF16),
                   jax.ShapeDtypeStruct((b, RET_HEADS, RET_DK, RET_DV), F32)],
        scratch_shapes=[pltpu.VMEM((hps, RET_DK, RET_DV), F32), pltpu.VMEM((hps, cr, cr), F32),
                        pltpu.VMEM((hps, cr, LANE), F32), pltpu.VMEM((hps, cr, LANE), F32)],
        compiler_params=_params(("parallel", "parallel", "arbitrary")),
        name="retention",
    )(lg, z, z, z, z, cos, sin, ret_norm.reshape(ret_norm.shape[0], 1, -1))


def _hgrn_gates(fa, lb):
    log_f = jnp.minimum(fa, 0.0) - jnp.log1p(jnp.exp(-jnp.abs(fa))) + jnp.log1p(lb * jnp.exp(-fa))
    hk = (1.0 - lb) * jax.nn.sigmoid(-fa)
    return log_f, hk


def _split3(x):
    hi = x.astype(BF16)
    r1 = x - hi.astype(F32)
    mid = r1.astype(BF16)
    lo = (r1 - mid.astype(F32)).astype(BF16)
    return hi, mid, lo


def _gla_state_step(q, k, v, b, st):
    b_last = b[GLA_CHUNK - 1:GLA_CHUNK, :]
    o = _bdot(q * jnp.exp(b), st, _NT)
    kd = k * jnp.exp(b_last - b)
    return o, st * jnp.exp(b_last) + _bdot(v, kd, _TN)


def _gla_intra_anchored(q, k, v, b):
    c = GLA_CHUNK
    nsub = c // GLA_SUB
    ti = lax.broadcasted_iota(jnp.int32, (c, c), 0)
    si = lax.broadcasted_iota(jnp.int32, (c, c), 1)
    anchors = [jnp.zeros((1, HD), F32)] + [b[i * GLA_SUB - 1:i * GLA_SUB, :] for i in range(1, nsub)]
    m_rows = jnp.concatenate([jnp.broadcast_to(m, (GLA_SUB, HD)) for m in anchors], axis=0)
    qs = q * jnp.exp(b - m_rows)
    a = jnp.zeros((c, c), F32)
    for i in range(nsub):
        ks = k * jnp.exp(jnp.minimum(anchors[i] - b, GLA_SAFE_SPAN))
        blk = (ti // GLA_SUB == i) & (si <= ti)
        a = a + jnp.where(blk, _bdot(qs, ks, _NT), 0.0)
    return _bdot(a, v)


def _gla_intra_pairwise(q, k, v, b):
    c = GLA_CHUNK
    nsub = c // GLA_SUB
    ti = lax.broadcasted_iota(jnp.int32, (c, c), 0)
    si = lax.broadcasted_iota(jnp.int32, (c, c), 1)
    a_off = jnp.zeros((c, c), F32)
    for i in range(1, nsub):
        m = b[i * GLA_SUB - 1:i * GLA_SUB, :]
        qs = q * jnp.exp(jnp.minimum(b - m, 0.0))
        ks = k * jnp.exp(jnp.minimum(m - b, 0.0))
        blk = (ti // GLA_SUB == i) & (si < i * GLA_SUB)
        a_off = a_off + jnp.where(blk, _bdot(qs, ks, _NT), 0.0)
    o = _bdot(a_off, v)
    rows = lax.broadcasted_iota(jnp.int32, (GLA_SUB, 1), 0)
    diag = []
    for i in range(nsub):
        sl = slice(i * GLA_SUB, (i + 1) * GLA_SUB)
        qi, ki, vi, bi = q[sl], k[sl], v[sl], b[sl]
        oi = jnp.zeros((GLA_SUB, HD), F32)
        for s in range(GLA_SUB):
            d = jnp.exp(jnp.minimum(bi - bi[s:s + 1], 0.0)) * qi * ki[s:s + 1]
            w = jnp.where(rows >= s, jnp.sum(d, axis=-1, keepdims=True), 0.0)
            oi = oi + w * vi[s:s + 1]
        diag.append(oi)
    return o + jnp.concatenate(diag, axis=0)


def _gla_kernel(f_ref, q_ref, i_ref, g_ref, lb_ref, nrm_ref, o_ref, st_ref, s_scr, *, nc, n_inner):
    c = pl.program_id(2)
    blk = f_ref.shape[0]

    @pl.when(c == 0)
    def _():
        s_scr[...] = jnp.zeros_like(s_scr)

    log_f, hk = _hgrn_gates(f_ref[...], lb_ref[...])
    hq = _silu(q_ref[...])
    v = i_ref[...]
    ti = lax.broadcasted_iota(jnp.int32, (blk, blk), 0)
    si = lax.broadcasted_iota(jnp.int32, (blk, blk), 1)
    tri = ((ti >= si) & (ti // GLA_CHUNK == si // GLA_CHUNK)).astype(BF16)
    hi, mid, lo = _split3(log_f)
    b = (jnp.dot(tri, hi, preferred_element_type=F32) + jnp.dot(tri, mid, preferred_element_type=F32)
         + jnp.dot(tri, lo, preferred_element_type=F32))
    sub_sum = jnp.sum(log_f.reshape(blk // GLA_SUB, GLA_SUB, log_f.shape[1]), axis=1)
    anchored_ok = jnp.min(sub_sum) >= -GLA_SAFE_SPAN
    n_heads = log_f.shape[1] // HD

    def run(intra):
        for hh in range(n_heads):
            hs = slice(hh * HD, (hh + 1) * HD)
            st = s_scr[hh]
            outs = []
            for ci in range(n_inner):
                sl = slice(ci * GLA_CHUNK, (ci + 1) * GLA_CHUNK)
                qc, kc, vc, bc = hq[sl, hs], hk[sl, hs], v[sl, hs], b[sl, hs]
                o_inter, st = _gla_state_step(qc, kc, vc, bc, st)
                outs.append(o_inter + intra(qc, kc, vc, bc))
            s_scr[hh] = st
            o = jnp.concatenate(outs, axis=0)
            y = o * lax.rsqrt(jnp.mean(o * o, axis=-1, keepdims=True) + EPS) * nrm_ref[:, hs]
            o_ref[:, hs] = (y * jax.nn.sigmoid(g_ref[:, hs])).astype(o_ref.dtype)

    lax.cond(anchored_ok, lambda: run(_gla_intra_anchored), lambda: run(_gla_intra_pairwise))

    @pl.when(c == nc - 1)
    def _():
        for hh in range(n_heads):
            st_ref[hh] = s_scr[hh].T


def _gla(z, l, lb, hg_norm, b, t):
    _, dst, _, _, _ = _segments()
    blk = min(t, 256)
    assert blk % GLA_CHUNK == 0
    nc = t // blk

    hps = 4 if HG_HEADS % 4 == 0 else 1
    w = hps * HD

    def zspec(name):
        assert dst[name] % w == 0
        return pl.BlockSpec((blk, w), lambda bi, h, c: (bi * nc + c, dst[name] // w + h))

    vec = lambda: pl.BlockSpec((None, 1, w), lambda bi, h, c: (l, 0, h))
    return pl.pallas_call(
        functools.partial(_gla_kernel, nc=nc, n_inner=blk // GLA_CHUNK),
        grid=(b, HG_HEADS // hps, nc),
        in_specs=[zspec('h_f'), zspec('h_q'), zspec('h_i'), zspec('h_g'), vec(), vec()],
        out_specs=[pl.BlockSpec((blk, w), lambda bi, h, c: (bi * nc + c, h)),
                   pl.BlockSpec((None, hps, HG_DK, HG_DV), lambda bi, h, c: (bi, h, 0, 0))],
        out_shape=[jax.ShapeDtypeStruct((z.shape[0], HG_HEADS * HG_DV), BF16),
                   jax.ShapeDtypeStruct((b, HG_HEADS, HG_DK, HG_DV), F32)],
        scratch_shapes=[pltpu.VMEM((hps, HG_DV, HG_DK), F32)],
        compiler_params=_params(("parallel", "parallel", "arbitrary")),
        name="hgrn2",
    )(z, z, z, z, lb.reshape(lb.shape[0], 1, -1), hg_norm.reshape(hg_norm.shape[0], 1, -1))


def _to_col(row):
    n = row.shape[1]
    eye = lax.broadcasted_iota(jnp.int32, (n, n), 0) == lax.broadcasted_iota(jnp.int32, (n, n), 1)
    return jnp.sum(jnp.where(eye, row, 0.0), axis=1, keepdims=True)


def _decode_rec_kernel(rq_ref, rk_ref, rv_ref, rg_ref, hf_ref, hq_ref, hi_ref, hg_ref,
                       cos_ref, sin_ref, rn_ref, hn_ref, lb_ref, sr_ref, sh_ref,
                       or_ref, oh_ref, nr_ref, nh_ref):
    cos, sin = cos_ref[...], sin_ref[...]

    def rms(o, g):
        return o * lax.rsqrt(jnp.mean(o * o, axis=-1, keepdims=True) + EPS) * g

    for h in range(RET_HEADS):
        sl = slice(h * HD, (h + 1) * HD)
        gamma = math.exp(_ret_gamma_log(h))
        q = _rot(rq_ref[:, sl], cos, sin, RET_DK // 2)
        k = _rot(rk_ref[:, sl], cos, sin, RET_DK // 2) * (RET_DK ** -0.5)
        v = rv_ref[:, sl]
        s = sr_ref[h]
        o = jnp.sum(q * k, axis=-1, keepdims=True) * v + jnp.sum(_to_col(q * gamma) * s, axis=0, keepdims=True)
        nr_ref[h] = s * gamma + _to_col(k) * v
        or_ref[:, sl] = rms(o, rn_ref[:, sl]) * _silu(rg_ref[:, sl])

    for h in range(HG_HEADS):
        sl = slice(h * HD, (h + 1) * HD)
        log_f, k = _hgrn_gates(hf_ref[:, sl], lb_ref[:, sl])
        q = _silu(hq_ref[:, sl])
        v = hi_ref[:, sl]
        s = sh_ref[h]
        eb = jnp.exp(log_f)
        o = jnp.sum(q * k, axis=-1, keepdims=True) * v + jnp.sum(_to_col(q * eb) * s, axis=0, keepdims=True)
        nh_ref[h] = s * _to_col(eb) + _to_col(k) * v
        oh_ref[:, sl] = rms(o, hn_ref[:, sl]) * jax.nn.sigmoid(hg_ref[:, sl])


def _decode_rec(z, l, cos, sin, ret_norm, hg_norm, lb, state_ret, state_hgrn):
    _, dst, padded, _, _ = _segments()
    nb = z.shape[0]

    z = z.reshape(nb, 1, z.shape[1])

    def zspec(name):
        w = padded[name]
        return pl.BlockSpec((None, 1, w), lambda bi: (bi, 0, dst[name] // w))

    one = lambda w: pl.BlockSpec((1, w), lambda bi: (0, 0))
    vec = lambda a: pl.BlockSpec((None, 1, a.shape[-1]), lambda bi: (l, 0, 0))
    st = lambda a: pl.BlockSpec((None, None) + a.shape[2:], lambda bi: (l, bi, 0, 0, 0))
    st_out = lambda a: pl.BlockSpec((None,) + a.shape[2:], lambda bi: (bi, 0, 0, 0))
    rw, hw = RET_HEADS * RET_DV, HG_HEADS * HG_DV
    r3 = lambda a: a.reshape(a.shape[0], 1, -1)
    return pl.pallas_call(
        _decode_rec_kernel,
        grid=(nb,),
        in_specs=[zspec('r_q'), zspec('r_k'), zspec('r_v'), zspec('r_g'),
                  zspec('h_f'), zspec('h_q'), zspec('h_i'), zspec('h_g'),
                  one(LANE), one(LANE), vec(ret_norm), vec(hg_norm), vec(lb), st(state_ret), st(state_hgrn)],
        out_specs=[pl.BlockSpec((None, 1, rw), lambda bi: (bi, 0, 0)),
                   pl.BlockSpec((None, 1, hw), lambda bi: (bi, 0, 0)),
                   st_out(state_ret), st_out(state_hgrn)],
        out_shape=[jax.ShapeDtypeStruct((nb, 1, rw), F32), jax.ShapeDtypeStruct((nb, 1, hw), F32),
                   jax.ShapeDtypeStruct(state_ret.shape[1:], F32), jax.ShapeDtypeStruct(state_hgrn.shape[1:], F32)],
        compiler_params=_params(("arbitrary",)),
        name="decode_recurrent",
    )(z, z, z, z, z, z, z, z, cos, sin, r3(ret_norm), r3(hg_norm), r3(lb), state_ret, state_hgrn)


def _sort_key(score):
    bits = lax.bitcast_convert_type(score + 0.0, jnp.int32)
    return bits ^ ((bits >> 31) & jnp.int32(0x7FFFFFFF))


def _count(mask):
    return jnp.sum(mask.astype(F32), axis=-1, keepdims=True)


def _nth_largest_key(count_ge, n_sel, shape):
    return _nth_largest_keys([count_ge], n_sel, shape)[0]


def _nth_largest_keys(count_fns, n_sel, shape):
    zero = jnp.zeros(shape, jnp.int32)
    los = tuple(jnp.where(f(zero) >= n_sel, 0, INT_MIN).astype(jnp.int32) for f in count_fns)

    def body(i, los):
        bit = jnp.left_shift(jnp.int32(1), 30 - i)
        return tuple(jnp.where(f(lo | bit) >= n_sel, lo | bit, lo) for f, lo in zip(count_fns, los))

    los = body(0, los)
    return lax.fori_loop(1, 31, body, los, unroll=SEARCH_UNROLL)


def _tie_bound(count_eq_below, need, nbits, shape):
    def body(i, j):
        cand = j | jnp.left_shift(jnp.int32(1), nbits - 1 - i)
        return jnp.where(count_eq_below(cand) < need, cand, j)

    return lax.fori_loop(0, nbits, body, jnp.zeros(shape, jnp.int32))


def _dsa_prompt_body(length, aq_ref, iq_ref, iw_ref, k_ref, v_ref, ik_ref, o_ref, key_scr, sel_scr, n_sel):
    j = pl.program_id(1)
    tq = aq_ref.shape[0]
    ik = ik_ref[0:length, :]
    score = jnp.zeros((tq, length), F32)
    for h in range(IDX_HEADS):
        s = lax.dot_general(iq_ref[:, h * HD:(h + 1) * HD], ik, _NT, preferred_element_type=F32)
        score = score + jnp.maximum(s, 0.0) * iw_ref[:, h:h + 1]
    q_pos = j * tq + lax.broadcasted_iota(jnp.int32, (tq, 1), 0)
    col = lax.broadcasted_iota(jnp.int32, (tq, length), 1)
    visible = col <= q_pos
    key_scr[:, 0:length] = _sort_key(jnp.where(visible, score, NEG_BIG))

    rows = tq // SEARCH_GROUPS
    group_count = lambda r: (lambda c: _count(key_scr[r * rows:(r + 1) * rows, 0:length] >= c))
    tau = jnp.concatenate(
        _nth_largest_keys([group_count(r) for r in range(SEARCH_GROUPS)], n_sel, (rows, 1)), axis=0)
    key = key_scr[:, 0:length]
    gt = key > tau
    eq = key == tau
    need = n_sel - _count(gt)
    spare = jnp.max(_count(eq & visible) - need) > 0.0
    bound = lax.cond(
        spare,
        lambda: _tie_bound(lambda c: _count((key_scr[:, 0:length] == tau) & (col < c)), need,
                           max(1, length.bit_length()), (tq, 1)),
        lambda: jnp.full((tq, 1), length, jnp.int32))
    sel_scr[:, 0:length] = ((gt | (eq & (col <= bound))) & visible).astype(F32)

    group = ATT_HEADS // ATT_KV_HEADS
    scale = ATT_HEAD_DIM ** -0.5
    for n in range(ATT_KV_HEADS):
        kn = k_ref[0:length, n * HD:(n + 1) * HD]
        vn = v_ref[0:length, n * HD:(n + 1) * HD]
        for g in range(group):
            sl = slice((n * group + g) * HD, (n * group + g + 1) * HD)
            s = lax.dot_general(aq_ref[:, sl], kn, _NT, preferred_element_type=F32) * scale
            s = jnp.where(sel_scr[:, 0:length] > 0.0, s, NEG_BIG)
            m = jnp.max(s, axis=-1, keepdims=True)
            p = jnp.exp(s - m)
            o = _bdot(p, vn) / jnp.sum(p, axis=-1, keepdims=True)
            o_ref[:, sl] = o.astype(o_ref.dtype)


def _dsa_prompt_kernel(*refs, n_sel, lengths):
    j = pl.program_id(1)
    tq = refs[0].shape[0]
    prev = 0
    for length in lengths:
        @pl.when((j >= prev // tq) & (j < length // tq))
        def _(length=length):
            _dsa_prompt_body(length, *refs, n_sel)
        prev = length


def _dsa_prompt(aq, iq, iw, kb, vb, ikb, b, t, mt):
    tq = min(Q_BLOCK, t)
    nq = t // tq
    n_sel = min(TOPK_MAX, t // 4)
    step = min(t, 512)
    lengths = tuple(range(step, t + 1, step))
    assert t % step == 0 and step % tq == 0 and step >= n_sel
    qrow = lambda w: pl.BlockSpec((tq, w), lambda bi, j: (bi * nq + j, 0))
    krow = lambda w: pl.BlockSpec((t, w), lambda bi, j: (bi, 0))
    return pl.pallas_call(
        functools.partial(_dsa_prompt_kernel, n_sel=n_sel, lengths=lengths),
        grid=(b, nq),
        in_specs=[qrow(aq.shape[1]), qrow(iq.shape[1]), qrow(LANE),
                  krow(kb.shape[1]), krow(vb.shape[1]), krow(HD)],
        out_specs=qrow(aq.shape[1]),
        out_shape=jax.ShapeDtypeStruct((mt, aq.shape[1]), BF16),
        scratch_shapes=[pltpu.VMEM((tq, t), jnp.int32), pltpu.VMEM((tq, t), F32)],
        compiler_params=_params(("parallel", "arbitrary")),
        name="dsa_prompt",
    )(aq, iq, iw, kb, vb, ikb)


def _dec_score_kernel(pt_ref, iq_ref, iw_ref, ikn_ref, *refs):
    pages = refs[:PAGES_PER_STEP]
    sc_ref, new_ref = refs[PAGES_PER_STEP:]
    iq = iq_ref[...].astype(BF16)
    iw = iw_ref[...]
    ik = jnp.concatenate([p[...].astype(BF16) for p in pages], axis=0)
    s = lax.dot_general(iq, ik, _NT, preferred_element_type=F32)
    sc_ref[...] = jnp.sum(jnp.maximum(s, 0.0) * iw, axis=0, keepdims=True)

    @pl.when(pl.program_id(1) == 0)
    def _():
        sn = lax.dot_general(iq, jnp.broadcast_to(ikn_ref[...], (8, HD)).astype(BF16), _NT,
                             preferred_element_type=F32)[:, 0:1]
        new_ref[...] = jnp.broadcast_to(jnp.sum(jnp.maximum(sn, 0.0) * iw, axis=0, keepdims=True), (1, LANE))


def _dec_scores(page_table, iq, iw, ik_new, cache_ik, l):
    nb, n_pages = page_table.shape
    steps = n_pages // PAGES_PER_STEP
    ih = iq.shape[1]
    page = lambda r: pl.BlockSpec((None, None, PAGE_SIZE, HD),
                                  lambda bi, p, pt: (l, pt[bi, p * PAGES_PER_STEP + r], 0, 0))
    grid_spec = pltpu.PrefetchScalarGridSpec(
        num_scalar_prefetch=1,
        grid=(nb, steps),
        in_specs=[pl.BlockSpec((None, ih, HD), lambda bi, p, pt: (bi, 0, 0)),
                  pl.BlockSpec((None, ih, 1), lambda bi, p, pt: (bi, 0, 0)),
                  pl.BlockSpec((None, 1, HD), lambda bi, p, pt: (bi, 0, 0))]
                 + [page(r) for r in range(PAGES_PER_STEP)],
        out_specs=[pl.BlockSpec((None, 1, PAGES_PER_STEP * PAGE_SIZE), lambda bi, p, pt: (bi, 0, p)),
                   pl.BlockSpec((None, 1, LANE), lambda bi, p, pt: (bi, 0, 0))],
    )
    return pl.pallas_call(
        _dec_score_kernel,
        grid_spec=grid_spec,
        out_shape=[jax.ShapeDtypeStruct((nb, 1, n_pages * PAGE_SIZE), F32),
                   jax.ShapeDtypeStruct((nb, 1, LANE), F32)],
        compiler_params=_params(("parallel", "arbitrary")),
        name="dec_scores",
    )(page_table, iq, iw, ik_new, *([cache_ik] * PAGES_PER_STEP))


def _dec_select_kernel(sc_ref, new_ref, tau_ref, bnd_ref, seln_ref, *, n_sel, nbits):
    key = _sort_key(sc_ref[...])
    key_new = _sort_key(new_ref[...])[:, 0:1]
    nb, s_len = key.shape
    idx = lax.broadcasted_iota(jnp.int32, key.shape, 1)
    tau = _nth_largest_key(lambda c: _count(key >= c) + (key_new >= c).astype(F32), n_sel, (nb, 1))
    eq = key == tau
    need = n_sel - _count(key > tau) - (key_new > tau).astype(F32)
    n_eq = _count(eq)
    bound = lax.cond(
        jnp.max(n_eq - need) > 0.0,
        lambda: _tie_bound(lambda c: _count(eq & (idx < c)), need, nbits, (nb, 1)),
        lambda: jnp.full((nb, 1), s_len, jnp.int32))
    sel_new = (key_new > tau) | ((key_new == tau) & (n_eq < need))
    tau_ref[...] = jnp.broadcast_to(tau, tau_ref.shape)
    bnd_ref[...] = jnp.broadcast_to(bound, bnd_ref.shape)
    seln_ref[...] = jnp.broadcast_to(sel_new.astype(F32), seln_ref.shape)


def _dec_select(scores, score_new, n_sel):
    nb, s_len = scores.shape
    out = lambda dt: jax.ShapeDtypeStruct((nb, LANE), dt)
    return pl.pallas_call(
        functools.partial(_dec_select_kernel, n_sel=n_sel, nbits=s_len.bit_length()),
        out_shape=[out(jnp.int32), out(jnp.int32), out(F32)],
        compiler_params=pltpu.CompilerParams(vmem_limit_bytes=VMEM_LIMIT),
        name="dec_select",
    )(scores, score_new)


def _dec_attn_kernel(pt_ref, tau_ref, bnd_ref, seln_ref, sc_ref, q_ref, kn_ref, vn_ref, *refs, steps):
    kp = refs[:PAGES_PER_STEP]
    vp = refs[PAGES_PER_STEP:2 * PAGES_PER_STEP]
    o_ref, m_scr, l_scr, acc_scr = refs[2 * PAGES_PER_STEP:]
    p_id = pl.program_id(1)
    chunk = PAGES_PER_STEP * PAGE_SIZE
    scale = ATT_HEAD_DIM ** -0.5

    @pl.when(p_id == 0)
    def _():
        m_scr[...] = jnp.full_like(m_scr, NEG_BIG)
        l_scr[...] = jnp.zeros_like(l_scr)
        acc_scr[...] = jnp.zeros_like(acc_scr)

    tau = tau_ref[:, 0:1]
    bound = bnd_ref[:, 0:1]
    key_c = _sort_key(sc_ref[...])
    idx_c = p_id * chunk + lax.broadcasted_iota(jnp.int32, key_c.shape, 1)
    sel = (key_c > tau) | ((key_c == tau) & (idx_c <= bound))
    q = q_ref[...].astype(BF16)
    heads = lambda pages, n: jnp.concatenate(
        [r[pl.ds(n, PAGE_SIZE, stride=ATT_KV_HEADS), :].astype(BF16) for r in pages], axis=0)
    s = sum(lax.dot_general(q[:, n * HD:(n + 1) * HD], heads(kp, n), _NT, preferred_element_type=F32)
            for n in range(ATT_KV_HEADS)) * scale
    s = jnp.where(sel, s, NEG_BIG)
    m_old = m_scr[:, 0:1]
    m_new = jnp.maximum(m_old, jnp.max(s, axis=-1, keepdims=True))
    alpha = jnp.exp(m_old - m_new)
    p = jnp.where(sel, jnp.exp(s - m_new), 0.0)
    l_new = alpha * l_scr[:, 0:1] + jnp.sum(p, axis=-1, keepdims=True)
    pb = p.astype(BF16)
    acc_new = alpha * acc_scr[...] + jnp.concatenate(
        [jnp.dot(pb, heads(vp, n), preferred_element_type=F32) for n in range(ATT_KV_HEADS)], axis=1)
    m_scr[...] = jnp.broadcast_to(m_new, m_scr.shape)
    l_scr[...] = jnp.broadcast_to(l_new, l_scr.shape)
    acc_scr[...] = acc_new

    @pl.when(p_id == steps - 1)
    def _():
        sel_new = seln_ref[:, 0:1] > 0.0
        qf = q_ref[...]
        s_new = jnp.sum(qf * kn_ref[...], axis=-1, keepdims=True) * scale
        s_new = jnp.where(sel_new, s_new, NEG_BIG)
        m_fin = jnp.maximum(m_new, s_new)
        a2 = jnp.exp(m_new - m_fin)
        p_new = jnp.where(sel_new, jnp.exp(s_new - m_fin), 0.0)
        l_fin = a2 * l_new + p_new
        acc_fin = (a2 * acc_new + p_new * vn_ref[...]) / l_fin
        group = ATT_HEADS // ATT_KV_HEADS
        for h in range(ATT_HEADS):
            n = h // group
            o_ref[:, h * HD:(h + 1) * HD] = acc_fin[h:h + 1, n * HD:(n + 1) * HD]


def _dec_attn(page_table, scores, score_new, q_bd, k_new, v_new, cache_k, cache_v, l):
    nb, n_pages = page_table.shape
    steps = n_pages // PAGES_PER_STEP
    s_len = n_pages * PAGE_SIZE
    chunk = PAGES_PER_STEP * PAGE_SIZE
    n_sel = min(TOPK_MAX, (s_len + 1) // 4)
    tau, bound, sel_new = _dec_select(scores.reshape(nb, s_len), score_new.reshape(nb, LANE), n_sel)
    row = lambda a: a.reshape(nb, 1, LANE)
    row_spec = pl.BlockSpec((None, 1, LANE), lambda bi, p, pt: (bi, 0, 0))
    kvw = ATT_KV_HEADS * HD
    hp = q_bd.shape[1]
    page = lambda r: pl.BlockSpec((None, None, PAGE_SIZE * ATT_KV_HEADS, HD),
                                  lambda bi, p, pt: (l, pt[bi, p * PAGES_PER_STEP + r], 0, 0))
    grid_spec = pltpu.PrefetchScalarGridSpec(
        num_scalar_prefetch=1,
        grid=(nb, steps),
        in_specs=[row_spec, row_spec, row_spec,
                  pl.BlockSpec((None, 1, chunk), lambda bi, p, pt: (bi, 0, p)),
                  pl.BlockSpec((None, hp, kvw), lambda bi, p, pt: (bi, 0, 0)),
                  pl.BlockSpec((None, 1, kvw), lambda bi, p, pt: (bi, 0, 0)),
                  pl.BlockSpec((None, 1, kvw), lambda bi, p, pt: (bi, 0, 0))]
                 + [page(r) for r in range(PAGES_PER_STEP)] * 2,
        out_specs=pl.BlockSpec((None, 1, ATT_HEADS * HD), lambda bi, p, pt: (bi, 0, 0)),
        scratch_shapes=[pltpu.VMEM((hp, LANE), F32), pltpu.VMEM((hp, LANE), F32),
                        pltpu.VMEM((hp, kvw), F32)],
    )
    return pl.pallas_call(
        functools.partial(_dec_attn_kernel, steps=steps),
        grid_spec=grid_spec,
        out_shape=jax.ShapeDtypeStruct((nb, 1, ATT_HEADS * HD), F32),
        compiler_params=_params(("parallel", "arbitrary")),
        name="dec_attn",
    )(page_table, row(tau), row(bound), row(sel_new), scores, q_bd, k_new, v_new,
      *([cache_k] * PAGES_PER_STEP), *([cache_v] * PAGES_PER_STEP)).reshape(nb, ATT_HEADS * HD)


def _ffn(x, xb, ssq, w1, w3, w2, l, tm, next_gain):
    tm_up = 2 * tm if (x.shape[0] // tm) % 2 == 0 else tm
    g, w2b = _matmul(xb, [w1, w3], l, tm=tm_up, epilogue='swiglu', out_dtype=BF16, ssq=ssq, cast_src=(w2, l))
    return _matmul(g, [w2b[None]], 0, tm=tm, epilogue='resid', res=x, scale=0.5, next_gain=next_gain)


def _token_tiles(n_tokens):
    n_tiles = max(1, n_tokens // 1024)
    tm = -(-n_tokens // (16 * n_tiles)) * 16
    return tm, n_tiles


def kernel(x_prompt, x_sample, state_ret, state_hgrn, cache_k, cache_v, cache_idx_k, page_table, ffn1_norm, ffn1_w1, ffn1_w3, ffn1_w2, mix_norm, w_in, ret_norm, q_norm, k_norm, idx_k_g, idx_k_b, hg_lb_raw, hg_norm, w_up_ret, w_up_att, w_up_hg, w_out, ffn2_norm, ffn2_w1, ffn2_w3, ffn2_w2):
    b, t, d = x_prompt.shape
    nb = x_sample.shape[0]
    depth = w_in.shape[0]
    kvw = ATT_KV_HEADS * ATT_HEAD_DIM
    group = ATT_HEADS // ATT_KV_HEADS

    lb_soft = jax.nn.softmax(hg_lb_raw.astype(F32), axis=0)
    lb_all = jnp.cumsum(lb_soft, axis=0) - lb_soft[0]
    cast = lambda w: w.astype(BF16)
    w_up_ret, w_up_att, w_up_hg, w_out = cast(w_up_ret), cast(w_up_att), cast(w_up_hg), cast(w_out)

    pos_p = jnp.arange(t, dtype=jnp.int32)
    pos_s = jnp.full((nb,), PAST_LEN, jnp.int32)
    ret_f = 1.0 / (ROPE_THETA ** jnp.linspace(0.0, 1.0, RET_DK // 2, dtype=F32))
    att_f = ROPE_THETA ** (-jnp.arange(0, ATT_HEAD_DIM, 2, dtype=F32) / ATT_HEAD_DIM)
    idx_f = ROPE_THETA ** (-jnp.arange(0, IDX_ROPE_DIM, 2, dtype=F32) / IDX_ROPE_DIM)
    tabs_p = _rope_tables(pos_p, att_f, ATT_HEAD_DIM) + _rope_tables(pos_p, idx_f, IDX_ROPE_DIM)
    tabs_s = _rope_tables(pos_s, att_f, ATT_HEAD_DIM) + _rope_tables(pos_s, idx_f, IDX_ROPE_DIM)
    ret_tab_p = _rope_tables(pos_p, ret_f, RET_DK)
    ret_tab_s = _rope_tables(pos_s[:1], ret_f, RET_DK)

    ck = cache_k.reshape(cache_k.shape[:2] + (PAGE_SIZE * ATT_KV_HEADS, HD))
    cv = cache_v.reshape(cache_v.shape[:2] + (PAGE_SIZE * ATT_KV_HEADS, HD))

    mp = b * t
    tm, n_tiles = _token_tiles(mp + nb)
    mt = tm * n_tiles
    x = jnp.concatenate([x_prompt.reshape(mp, d), x_sample.reshape(nb, d), jnp.zeros((mt - mp - nb, d), F32)])

    def with_tail(a, rows):
        tail = jnp.concatenate([rows.astype(a.dtype), jnp.zeros((mt - mp - nb, a.shape[1]), a.dtype)])
        return lax.dynamic_update_slice(a, tail, (mp, 0))

    outs = {n: [] for n in ('rp', 'rs', 'hp', 'hs', 'kp', 'vp', 'ip', 'ks', 'vs', 'is')}
    xb, ssq = _row_prep(x, ffn1_norm, 0)
    for l in range(depth):
        x, xb, ssq = _ffn(x, xb, ssq, ffn1_w1, ffn1_w3, ffn1_w2, l, tm, (mix_norm, l))
        z = _in_proj(xb, ssq, w_in, l, 2 * tm if n_tiles % 2 == 0 else tm)

        aq, ak, av, akb, avb, iq, ik, ikb, iw = _prelude(z, mp, l, tabs_p, q_norm, k_norm, idx_k_g, idx_k_b, t)
        a_ret, r_p = _retention(z, l, ret_tab_p[0], ret_tab_p[1], ret_norm, b, t)
        a_hg, h_p = _gla(z, l, lb_all, hg_norm, b, t)
        a_att = _dsa_prompt(aq, iq, iw, akb, avb, ikb, b, t, mt)
        outs['rp'].append(r_p); outs['hp'].append(h_p)
        outs['kp'].append(ak.reshape(b, t, ATT_KV_HEADS, ATT_HEAD_DIM))
        outs['vp'].append(av.reshape(b, t, ATT_KV_HEADS, ATT_HEAD_DIM))
        outs['ip'].append(ik.reshape(b, t, IDX_DIM))

        zs = z[mp:mp + nb]
        aq, ak, av, _, _, iq, ik, _, iw = _prelude(zs, nb, l, tabs_s, q_norm, k_norm, idx_k_g, idx_k_b, 1)
        s_ret, s_hg, r_s, h_s = _decode_rec(zs, l, ret_tab_s[0], ret_tab_s[1], ret_norm, hg_norm, lb_all,
                                            state_ret, state_hgrn)
        scores, score_new = _dec_scores(page_table, iq.reshape(nb, IDX_HEADS, IDX_DIM),
                                        iw[:, :IDX_HEADS].reshape(nb, IDX_HEADS, 1),
                                        ik.reshape(nb, 1, IDX_DIM), cache_idx_k, l)
        qh = aq.reshape(nb, ATT_KV_HEADS, group, ATT_HEAD_DIM)
        q_bd = jnp.concatenate(
            [jnp.concatenate([qh[:, n] if m == n else jnp.zeros_like(qh[:, n]) for m in range(ATT_KV_HEADS)], axis=-1)
             for n in range(ATT_KV_HEADS)]
            + [jnp.zeros((nb, max(0, 16 - ATT_HEADS), kvw), aq.dtype)], axis=1)
        s_att = _dec_attn(page_table, scores, score_new, q_bd, ak.reshape(nb, 1, kvw), av.reshape(nb, 1, kvw),
                          ck, cv, l)
        outs['rs'].append(r_s); outs['hs'].append(h_s)
        outs['ks'].append(ak.reshape(nb, 1, ATT_KV_HEADS, ATT_HEAD_DIM))
        outs['vs'].append(av.reshape(nb, 1, ATT_KV_HEADS, ATT_HEAD_DIM))
        outs['is'].append(ik.reshape(nb, 1, IDX_DIM))

        merged = _merge(with_tail(a_ret, s_ret.reshape(nb, -1)), with_tail(a_att, s_att),
                        with_tail(a_hg, s_hg.reshape(nb, -1)), w_up_ret, w_up_att, w_up_hg, z, l, BF16, tm)
        x, xb, ssq = _matmul(merged, [w_out], l, tm=tm, epilogue='resid', res=x, scale=1.0,
                             next_gain=(ffn2_norm, l))
        if l + 1 < depth:
            x, xb, ssq = _ffn(x, xb, ssq, ffn2_w1, ffn2_w3, ffn2_w2, l, tm, (ffn1_norm, l + 1))
        else:
            x = _ffn(x, xb, ssq, ffn2_w1, ffn2_w3, ffn2_w2, l, tm, None)

    st = lambda n: jnp.stack(outs[n])
    return (x[:mp].reshape(b, t, d), x[mp:mp + nb].reshape(nb, 1, d),
            st('rp').astype(state_ret.dtype), st('rs').astype(state_ret.dtype),
            st('hp').astype(state_hgrn.dtype), st('hs').astype(state_hgrn.dtype),
            st('kp'), st('vp'), st('ip'), st('ks'), st('vs'), st('is'))
```
